```python
import math
import jax
import jax.numpy as jnp
from jax import lax
import numpy as np

D_MODEL = 1024
BATCH = 8
SEQ = 4096
DEPTH = 4

GRID_W = 64
CTX_LEN = 256
N_MIXERS = 3
MIX_POOL = 0
MIX_SSM = 1
MIX_ATTN = 2
NORM_EPS = 1e-6
N_MOD = 6

POOL_WINDOWS = (2, 4, 8, 16)
POOL_GROUPS = len(POOL_WINDOWS)
POOL_GC = D_MODEL // POOL_GROUPS

SSM_H = 16
SSM_G = D_MODEL // SSM_H
SSM_P = 64
SSM_CHUNK = 128
SSM_DT_MIN = 0.001
SSM_DT_MAX = 0.1

HEAD_DIM = 128
N_HEADS = D_MODEL // HEAD_DIM
N_KV_HEADS = 2
KV_REP = N_HEADS // N_KV_HEADS
QKV_DIM = D_MODEL + 2 * N_KV_HEADS * HEAD_DIM
Q_BLOCK = 128
ROPE_F = HEAD_DIM // 4
ROPE_THETA = 10000.0

N_EXPERTS = 32
TOP_K = 4
D_FF = D_MODEL
SWIGLU_ALPHA = 1.702
SWIGLU_LIMIT = 7.0
MOE_BLOCK = 256

kernel_name = 'hybrid_pool_s5_gqa_moe_prefix_dit'


def _layers_of(kind):
    return len(range(kind, DEPTH, N_MIXERS))


def _rms(x, g):
    x32 = x.astype(jnp.float32)
    y = x32 * lax.rsqrt(jnp.mean(x32 * x32, axis=-1, keepdims=True) + NORM_EPS)
    return (y * g.astype(jnp.float32)).astype(x.dtype)


def _modulate(h, shift, scale):
    return h * (1 + scale) + shift


def _pool_mixer(h, w, scale):
    b, n, d = h.shape
    h32 = h.astype(jnp.float32)
    csum = jnp.concatenate([jnp.zeros((b, 1, d), jnp.float32), jnp.cumsum(h32, axis=1)], axis=1)
    t = jnp.arange(n)
    diffs = []
    for g, win in enumerate(POOL_WINDOWS):
        lo = jnp.clip(t - win // 2, 0, n)
        hi = jnp.clip(t + win // 2, 0, n)
        sl = slice(g * POOL_GC, (g + 1) * POOL_GC)
        total = csum[:, hi, sl] - csum[:, lo, sl]
        cnt = (hi - lo).astype(jnp.float32)[None, :, None]
        diffs.append(total / cnt - h32[:, :, sl])
    diff = jnp.stack(diffs, axis=2).astype(h.dtype)
    y = jnp.einsum('bngc,gce->bnge', diff, w).reshape(b, n, d)
    return y * scale


def _s5_discretise(lam_re, lam_im, log_dt, b_re, b_im):
    f32 = jnp.float32
    lam_re, lam_im = lam_re.astype(f32), lam_im.astype(f32)
    b_re, b_im = b_re.astype(f32), b_im.astype(f32)
    dt = jnp.exp(log_dt.astype(f32))[:, None]
    zr, zi = lam_re * dt, lam_im * dt
    mag = jnp.exp(zr)
    ar, ai = mag * jnp.cos(zi), mag * jnp.sin(zi)
    den = lam_re * lam_re + lam_im * lam_im
    cr = ((ar - 1.0) * lam_re + ai * lam_im) / den
    ci = (ai * lam_re - (ar - 1.0) * lam_im) / den
    bbr = cr[..., None] * b_re - ci[..., None] * b_im
    bbi = cr[..., None] * b_im + ci[..., None] * b_re
    return zr, zi, bbr, bbi


def _ssm_combine(e1, e2):
    a1r, a1i, b1r, b1i = e1
    a2r, a2i, b2r, b2i = e2
    return (a1r * a2r - a1i * a2i,
            a1r * a2i + a1i * a2r,
            a2r * b1r - a2i * b1i + b2r,
            a2r * b1i + a2i * b1r + b2i)


def _s5_scan(u, zr, zi, bbr, bbi, c_re, c_im, h0r, h0i, readout):
    n, b = u.shape[0], u.shape[1]
    tau = jnp.arange(1, SSM_CHUNK + 1, dtype=jnp.float32)[:, None, None]
    pmag = jnp.exp(tau * zr)
    pr = (pmag * jnp.cos(tau * zi))[:, None]
    pi_ = (pmag * jnp.sin(tau * zi))[:, None]
    full = (SSM_CHUNK, b, SSM_G, SSM_P)
    a_r = jnp.broadcast_to(pr[:1], full)
    a_i = jnp.broadcast_to(pi_[:1], full)

    def step(carry, u_blk):
        hr0, hi0 = carry
        bur = jnp.einsum('tbgh,gph->tbgp', u_blk, bbr)
        bui = jnp.einsum('tbgh,gph->tbgp', u_blk, bbi)
        _, _, hr, hi = lax.associative_scan(_ssm_combine, (a_r, a_i, bur, bui), axis=0)
        hr, hi = hr + pr * hr0 - pi_ * hi0, hi + pr * hi0 + pi_ * hr0
        y = None
        if readout:
            y = jnp.einsum('tbgp,ghp->tbgh', hr, c_re) - jnp.einsum('tbgp,ghp->tbgh', hi, c_im)
        return (hr[-1], hi[-1]), y

    u_chunks = u.reshape((n // SSM_CHUNK, SSM_CHUNK) + u.shape[1:])
    h_last, ys = lax.scan(step, (h0r, h0i), u_chunks)
    if readout:
        ys = ys.reshape(u.shape)
    return ys, h_last


def _s5_mixer(h, hc, lam_re, lam_im, log_dt, b_re, b_im, c_re, c_im, d_skip, glu_w, glu_b, need_ctx_out):
    f32 = jnp.float32
    b = h.shape[0]

    def to_seq(t):
        return jnp.transpose(t.astype(f32).reshape(t.shape[0], t.shape[1], SSM_G, SSM_H), (1, 0, 2, 3))

    u, uc = to_seq(h), to_seq(hc)
    zeros = jnp.zeros((b, SSM_G, SSM_P), f32)
    y_lat, y_ctx = [], []
    for direction in range(2):
        zr, zi, bbr, bbi = _s5_discretise(lam_re[direction], lam_im[direction], log_dt[direction],
                                          b_re[direction], b_im[direction])
        cr, ci = c_re[direction].astype(f32), c_im[direction].astype(f32)
        rev = direction == 1
        uc_d = uc[::-1] if rev else uc
        u_d = u[::-1] if rev else u
        yc_d, (hr, hi) = _s5_scan(uc_d, zr, zi, bbr, bbi, cr, ci, zeros, zeros, need_ctx_out)
        yl_d, _ = _s5_scan(u_d, zr, zi, bbr, bbi, cr, ci, hr, hi, True)
        y_lat.append(yl_d[::-1] if rev else yl_d)
        if need_ctx_out:
            y_ctx.append(yc_d[::-1] if rev else yc_d)

    def readout(ys, t):
        ysum = ys[0] + ys[1]
        y = jnp.transpose(ysum, (1, 0, 2, 3)).reshape(t.shape) + d_skip.astype(f32) * t.astype(f32)
        g = jax.nn.gelu(y).astype(t.dtype)
        z = g @ glu_w + glu_b
        val, gate = jnp.split(z, 2, axis=-1)
        return val * jax.nn.sigmoid(gate)

    y = readout(y_lat, h)
    yc = readout(y_ctx, hc) if need_ctx_out else None
    return y, yc


def _grid_angles(n):
    rows = n // GRID_W
    row = jnp.repeat(jnp.arange(rows), GRID_W)
    col = jnp.tile(jnp.arange(GRID_W), rows)
    pos = jnp.stack([row, col], axis=-1).astype(jnp.float32)
    inv_freq = ROPE_THETA ** (-jnp.arange(ROPE_F, dtype=jnp.float32) / ROPE_F)
    ang = pos[:, :, None] * inv_freq
    return jnp.cos(ang), jnp.sin(ang)


def _rope2d(x, cos, sin):
    xr = x.reshape(x.shape[:-1] + (2, 2, ROPE_F))
    x1, x2 = xr[..., 0, :], xr[..., 1, :]
    shape = (1, cos.shape[0]) + (1,) * (x1.ndim - 4) + (2, ROPE_F)
    cs = cos.reshape(shape).astype(x.dtype)
    sn = sin.reshape(shape).astype(x.dtype)
    out = jnp.stack([x1 * cs - x2 * sn, x2 * cs + x1 * sn], axis=-2)
    return out.reshape(x.shape)


def _attend(q, k, v):
    s = jnp.einsum('bqgrd,bkgd->bgrqk', q, k).astype(jnp.float32) * (HEAD_DIM ** -0.5)
    p = jax.nn.softmax(s, axis=-1).astype(v.dtype)
    return jnp.einsum('bgrqk,bkgd->bqgrd', p, v)


def _attn_mixer(h, hc, wqkv, q_g, k_g, wo, need_ctx_out):
    b, n, _ = h.shape
    l = hc.shape[1]

    def project(t, with_q):
        m = t.shape[1]
        kv_w = N_KV_HEADS * HEAD_DIM
        z = t @ (wqkv if with_q else wqkv[:, D_MODEL:])
        q = None
        if with_q:
            q, z = z[..., :D_MODEL], z[..., D_MODEL:]
            q = _rms(q.reshape(b, m, N_KV_HEADS, KV_REP, HEAD_DIM), q_g)
        k = _rms(z[..., :kv_w].reshape(b, m, N_KV_HEADS, HEAD_DIM), k_g)
        v = z[..., kv_w:].reshape(b, m, N_KV_HEADS, HEAD_DIM)
        return q, k, v

    q, k, v = project(h, True)
    cos, sin = _grid_angles(n)
    q, k = _rope2d(q, cos, sin), _rope2d(k, cos, sin)
    qc, kc, vc = project(hc, need_ctx_out)
    keys = jnp.concatenate([kc, k], axis=1)
    vals = jnp.concatenate([vc, v], axis=1)
    nb = n // Q_BLOCK
    q_blocks = jnp.moveaxis(q.reshape(b, nb, Q_BLOCK, N_KV_HEADS, KV_REP, HEAD_DIM), 1, 0)
    o = lax.map(lambda qb: _attend(qb, keys, vals), q_blocks)
    o = jnp.moveaxis(o, 0, 1).reshape(b, n, D_MODEL)
    y = o @ wo
    yc = _attend(qc, kc, vc).reshape(b, l, D_MODEL) @ wo if need_ctx_out else None
    return y, yc


def _moe(h, router_w, router_b, w1, b1, w2, b2):
    t, d = h.shape
    logits = (h @ router_w + router_b).astype(jnp.float32)
    vals, idx = lax.top_k(logits, TOP_K)
    gates = jax.nn.softmax(vals, axis=-1)
    flat_e = idx.reshape(-1)
    flat_tok = jnp.repeat(jnp.arange(t, dtype=jnp.int32), TOP_K)
    flat_g = gates.reshape(-1)
    order = jnp.argsort(flat_e)
    sorted_e = flat_e[order]
    counts = jnp.bincount(flat_e, length=N_EXPERTS)
    padded = (counts + MOE_BLOCK - 1) // MOE_BLOCK * MOE_BLOCK
    ends = jnp.cumsum(padded)
    pad_start = ends - padded
    grp_start = jnp.cumsum(counts) - counts
    rank = jnp.arange(t * TOP_K) - grp_start[sorted_e]
    dest = pad_start[sorted_e] + rank
    n_blocks = -(-(t * TOP_K) // MOE_BLOCK) + N_EXPERTS
    cap = n_blocks * MOE_BLOCK
    tok_buf = jnp.full((cap,), t, jnp.int32).at[dest].set(flat_tok[order])
    gate_buf = jnp.zeros((cap,), jnp.float32).at[dest].set(flat_g[order])
    h_pad = jnp.concatenate([h, jnp.zeros((1, d), h.dtype)], axis=0)
    x_blocks = h_pad[tok_buf].reshape(n_blocks, MOE_BLOCK, d)
    block_e = jnp.minimum(jnp.searchsorted(ends, jnp.arange(n_blocks) * MOE_BLOCK, side='right'), N_EXPERTS - 1)

    def expert(args):
        xb, e = args
        z = xb @ w1[e] + b1[e]
        glu, lin = z[:, :D_FF], z[:, D_FF:]
        glu = jnp.minimum(glu, SWIGLU_LIMIT)
        lin = jnp.clip(lin, -SWIGLU_LIMIT, SWIGLU_LIMIT)
        act = glu * jax.nn.sigmoid(SWIGLU_ALPHA * glu) * (lin + 1)
        return act @ w2[e] + b2[e]

    y = lax.map(expert, (x_blocks, block_e)).reshape(cap, d)
    out = jax.ops.segment_sum(y * gate_buf[:, None].astype(y.dtype), tok_buf, num_segments=t + 1)
    return out[:t]


def setup_inputs(seed: int = 0) -> dict:
    key = jax.random.key(seed)
    keys = iter(jax.random.split(key, 32))
    f32 = jnp.float32

    def nrm(shape, s=1.0):
        return jax.random.normal(next(keys), shape, f32) * s

    n_pool, n_ssm, n_attn = _layers_of(MIX_POOL), _layers_of(MIX_SSM), _layers_of(MIX_ATTN)
    d = D_MODEL
    return {
        'x': nrm((BATCH, SEQ, d)),
        'c': nrm((BATCH, d)),
        'ctx': nrm((BATCH, CTX_LEN, d)),
        'c_ctx': nrm((d,)),
        'ada_w': nrm((DEPTH, d, N_MOD * d), 0.5 * d ** -0.5),
        'ada_b': nrm((DEPTH, N_MOD * d), 0.01),
        'norm1_g': 1.0 + nrm((DEPTH, d), 0.01),
        'norm2_g': 1.0 + nrm((DEPTH, d), 0.01),
        'pool_w': nrm((n_pool, POOL_GROUPS, POOL_GC, POOL_GC), POOL_GC ** -0.5),
        'pool_scale': 1.0 + nrm((n_pool, d), 0.02),
        'ssm_lam_re': -0.5 + nrm((n_ssm, 2, SSM_G, SSM_P), 0.01),
        'ssm_lam_im': jnp.pi * jnp.arange(SSM_P, dtype=f32) + nrm((n_ssm, 2, SSM_G, SSM_P), 0.01),
        'ssm_log_dt': jax.random.uniform(next(keys), (n_ssm, 2, SSM_G), f32,
                                         math.log(SSM_DT_MIN), math.log(SSM_DT_MAX)),
        'ssm_b_re': nrm((n_ssm, 2, SSM_G, SSM_P, SSM_H), (2 * SSM_H) ** -0.5),
        'ssm_b_im': nrm((n_ssm, 2, SSM_G, SSM_P, SSM_H), (2 * SSM_H) ** -0.5),
        'ssm_c_re': nrm((n_ssm, 2, SSM_G, SSM_H, SSM_P), SSM_P ** -0.5),
        'ssm_c_im': nrm((n_ssm, 2, SSM_G, SSM_H, SSM_P), SSM_P ** -0.5),
        'ssm_d': nrm((n_ssm, d)),
        'ssm_glu_w': nrm((n_ssm, d, 2 * d), d ** -0.5),
        'ssm_glu_b': nrm((n_ssm, 2 * d), 0.01),
        'attn_wqkv': nrm((n_attn, d, QKV_DIM), d ** -0.5),
        'attn_q_g': 1.0 + nrm((n_attn, HEAD_DIM), 0.01),
        'attn_k_g': 1.0 + nrm((n_attn, HEAD_DIM), 0.01),
        'attn_wo': nrm((n_attn, d, d), d ** -0.5),
        'router_w': nrm((DEPTH, d, N_EXPERTS), d ** -0.5),
        'router_b': nrm((DEPTH, N_EXPERTS), 0.01),
        'moe_w1': nrm((DEPTH, N_EXPERTS, d, 2 * D_FF), d ** -0.5),
        'moe_b1': nrm((DEPTH, N_EXPERTS, 2 * D_FF), 0.01),
        'moe_w2': nrm((DEPTH, N_EXPERTS, D_FF, d), D_FF ** -0.5),
        'moe_b2': nrm((DEPTH, N_EXPERTS, d), 0.01),
    }


def reference(x, c, ctx, c_ctx, ada_w, ada_b, norm1_g, norm2_g, pool_w, pool_scale,
              ssm_lam_re, ssm_lam_im, ssm_log_dt, ssm_b_re, ssm_b_im, ssm_c_re, ssm_c_im,
              ssm_d, ssm_glu_w, ssm_glu_b, attn_wqkv, attn_q_g, attn_k_g, attn_wo,
              router_w, router_b, moe_w1, moe_b1, moe_w2, moe_b2):
    b, n, d = x.shape
    xc = ctx
    s_lat = jax.nn.silu(c)
    s_ctx = jax.nn.silu(c_ctx)
    for i in range(DEPTH):
        kind, j = i % N_MIXERS, i // N_MIXERS
        last = i == DEPTH - 1
        ctx_in = (not last) or kind != MIX_POOL
        mod = [m[:, None, :] for m in jnp.split(s_lat @ ada_w[i] + ada_b[i], N_MOD, axis=-1)]
        h = _modulate(_rms(x, norm1_g[i]), mod[0], mod[1])
        hc, mod_c = None, None
        if ctx_in:
            n_c = 2 if last else N_MOD
            width = n_c * D_MODEL
            mod_c = jnp.split(s_ctx @ ada_w[i][:, :width] + ada_b[i][:width], n_c, axis=-1)
            hc = _modulate(_rms(xc, norm1_g[i]), mod_c[0], mod_c[1])
        if kind == MIX_POOL:
            y = _pool_mixer(h, pool_w[j], pool_scale[j])
            yc = None if last else _pool_mixer(hc, pool_w[j], pool_scale[j])
        elif kind == MIX_SSM:
            y, yc = _s5_mixer(h, hc, ssm_lam_re[j], ssm_lam_im[j], ssm_log_dt[j], ssm_b_re[j], ssm_b_im[j],
                              ssm_c_re[j], ssm_c_im[j], ssm_d[j], ssm_glu_w[j], ssm_glu_b[j], not last)
        else:
            y, yc = _attn_mixer(h, hc, attn_wqkv[j], attn_q_g[j], attn_k_g[j], attn_wo[j], not last)
        x = x + mod[2] * y
        h2 = _modulate(_rms(x, norm2_g[i]), mod[3], mod[4]).reshape(b * n, d)
        if last:
            f = _moe(h2, router_w[i], router_b[i], moe_w1[i], moe_b1[i], moe_w2[i], moe_b2[i])
            x = x + mod[5] * f.reshape(b, n, d)
        else:
            xc = xc + mod_c[2] * yc
            hc2 = _modulate(_rms(xc, norm2_g[i]), mod_c[3], mod_c[4]).reshape(-1, d)
            f = _moe(jnp.concatenate([h2, hc2], axis=0),
                     router_w[i], router_b[i], moe_w1[i], moe_b1[i], moe_w2[i], moe_b2[i])
            x = x + mod[5] * f[:b * n].reshape(b, n, d)
            xc = xc + mod_c[5] * f[b * n:].reshape(xc.shape)
    return x
```

```python
import functools
import math

import numpy as np
import jax
import jax.numpy as jnp
from jax import lax
from jax.experimental import pallas as pl
from jax.experimental.pallas import tpu as pltpu

F32 = jnp.float32
BF16 = jnp.bfloat16

GRID_W = 64
NORM_EPS = 1e-6
POOL_WINDOWS = (2, 4, 8, 16)
SSM_H = 16
SSM_P = 64
SSM_SET = 8
SSM_TT = 32
HEAD_DIM = 128
N_KV_HEADS = 2
ROPE_F = HEAD_DIM // 4
ROPE_THETA = 10000.0
TOP_K = 4
SWIGLU_ALPHA = 1.702
SWIGLU_LIMIT = 7.0

TM = 256
BLK = 256
SEG_ALIGN = 8
SEG_BITS = (256, 128, 64, 32, 16, 8)
VMEM_LIMIT = 56 * 1024 * 1024


def _cparams(sem, vmem=VMEM_LIMIT):
    return pltpu.CompilerParams(dimension_semantics=sem, vmem_limit_bytes=vmem)


def _rms(x, g):
    return x * lax.rsqrt(jnp.mean(x * x, axis=-1, keepdims=True) + NORM_EPS) * g


def _sigmoid(x):
    return 1.0 / (1.0 + jnp.exp(-x))


def _dot(a, b):
    return jnp.dot(a, b, preferred_element_type=F32)


def _dot_nt(a, b):
    return lax.dot_general(a, b, (((1,), (1,)), ((), ())), preferred_element_type=F32)


def _dot_tn(a, b):
    return lax.dot_general(a, b, (((0,), (0,)), ((), ())), preferred_element_type=F32)


def _ada_kernel(c_ref, w_ref, b_ref, o_ref):
    c = c_ref[...]
    s = c * _sigmoid(c)
    o_ref[0] = jnp.dot(s, w_ref[0], preferred_element_type=F32,
                       precision=lax.Precision.HIGHEST) + b_ref[0]


def _ada_mods(c16, ada_w, ada_b):
    depth, d, six_d = ada_w.shape
    tn = d
    return pl.pallas_call(
        _ada_kernel,
        grid=(depth, six_d // tn),
        in_specs=[
            pl.BlockSpec((16, d), lambda l, j: (0, 0)),
            pl.BlockSpec((1, d, tn), lambda l, j: (l, 0, j)),
            pl.BlockSpec((1, 1, tn), lambda l, j: (l, 0, j)),
        ],
        out_specs=pl.BlockSpec((1, 16, tn), lambda l, j: (l, 0, j)),
        out_shape=jax.ShapeDtypeStruct((depth, 16, six_d), F32),
        compiler_params=_cparams(("arbitrary", "arbitrary")),
        name="ada_mods",
    )(c16, ada_w, ada_b.reshape(depth, 1, six_d))


def _post(x, y, mod, g2, rwh, rwl, rb, x1_ref, h2_ref, route_ref, cnt_ref):
    n_e = rwh.shape[0]
    x1 = x + mod[2:3] * y
    x1_ref[...] = x1
    h2 = _rms(x1, g2) * (1.0 + mod[4:5]) + mod[3:4]
    h2_ref[...] = h2.astype(BF16)
    hh = h2.astype(BF16)
    hl = (h2 - hh.astype(F32)).astype(BF16)
    logits = _dot_nt(rwh, hh) + _dot_nt(rwh, hl) + _dot_nt(rwl, hh) + rb
    iota_e = lax.broadcasted_iota(jnp.int32, (n_e, TM), 0)
    vals, onehots = [], []
    l = logits
    for _ in range(TOP_K):
        m = jnp.max(l, axis=0, keepdims=True)
        idx = jnp.min(jnp.where(l == m, iota_e, n_e), axis=0, keepdims=True)
        sel = iota_e == idx
        vals.append(m)
        onehots.append(sel)
        l = jnp.where(sel, -jnp.inf, l)
    ex = [jnp.exp(v - vals[0]) for v in vals]
    den = ex[0] + ex[1] + ex[2] + ex[3]
    gates = [e / den for e in ex]
    member = jnp.zeros((n_e, TM), F32)
    for sel in onehots:
        member = member + jnp.where(sel, 1.0, 0.0)
    r_i = lax.broadcasted_iota(jnp.int32, (TM, TM), 0)
    c_i = lax.broadcasted_iota(jnp.int32, (TM, TM), 1)
    upper = jnp.where(r_i < c_i, 1.0, 0.0).astype(BF16)
    cum = _dot(member.astype(BF16), upper)
    cnt = jnp.sum(member, axis=1, keepdims=True)
    cnt_ref[0] = cnt
    c8 = jnp.floor((cnt + (SEG_ALIGN - 1)) * (1.0 / SEG_ALIGN)) * SEG_ALIGN
    e_r = lax.broadcasted_iota(jnp.int32, (n_e, n_e), 0)
    e_c = lax.broadcasted_iota(jnp.int32, (n_e, n_e), 1)
    lower = jnp.where(e_c < e_r, 1.0, 0.0).astype(BF16)
    seg = _dot(lower, jnp.broadcast_to(c8, (n_e, TM)).astype(BF16))
    base = seg + cum
    rows = []
    for sel in onehots:
        rows.append(jnp.sum(jnp.where(sel, base, 0.0), axis=0, keepdims=True))
    route_ref[0] = jnp.concatenate(rows + gates, axis=0)


def _post_specs(n_steps, d, n_e):
    specs = [
        pl.BlockSpec((TM, d), lambda i, *_: (i, 0)),
        pl.BlockSpec((TM, d), lambda i, *_: (i, 0)),
        pl.BlockSpec((1, 2 * TOP_K, TM), lambda i, *_: (i, 0, 0)),
        pl.BlockSpec((1, n_e, 1), lambda i, *_: (i, 0, 0)),
    ]
    shapes = [
        jax.ShapeDtypeStruct((n_steps * TM, d), F32),
        jax.ShapeDtypeStruct((n_steps * TM, d), BF16),
        jax.ShapeDtypeStruct((n_steps, 2 * TOP_K, TM), F32),
        jax.ShapeDtypeStruct((n_steps, n_e, 1), F32),
    ]
    return specs, shapes


def _const_spec(shape):
    nd = len(shape)
    return pl.BlockSpec(shape, lambda i, *_: (0,) * nd)


def _pool_kernel(src, mrow, hp, hn, x_ref, xp_ref, xn_ref, mod_ref, g1_ref, pw_ref, ps_ref,
                 g2_ref, rwh_ref, rwl_ref, rb_ref, x1_ref, h2_ref, route_ref, cnt_ref, hh_scr):
    i = pl.program_id(0)
    d = x_ref.shape[1]
    gc = d // len(POOL_WINDOWS)
    halo = SEG_ALIGN
    mod = mod_ref[0]
    g1 = g1_ref[...]

    def pre(v):
        return _rms(v, g1) * (1.0 + mod[1:2]) + mod[0:1]

    has_prev = hp[i] > 0
    has_next = hn[i] > 0
    x = x_ref[...]
    h = pre(x)
    hh_scr[0:halo, :] = jnp.where(has_prev, pre(xp_ref[...]), 0.0)
    hh_scr[halo:halo + TM, :] = h
    hh_scr[halo + TM:2 * halo + TM, :] = jnp.where(has_next, pre(xn_ref[...]), 0.0)
    row = lax.broadcasted_iota(jnp.int32, (TM, 1), 0)
    ys = []
    for g, win in enumerate(POOL_WINDOWS):
        half = win // 2
        c0 = g * gc
        acc = hh_scr[pl.ds(halo - half, TM), c0:c0 + gc]
        for j in range(-half + 1, half):
            acc = acc + hh_scr[pl.ds(halo + j, TM), c0:c0 + gc]
        lo_clip = jnp.where(has_prev, 0, jnp.maximum(half - row, 0))
        hi_clip = jnp.where(has_next, 0, jnp.maximum(row + half - TM, 0))
        cnt = (win - lo_clip - hi_clip).astype(F32)
        diff = acc / cnt - h[:, c0:c0 + gc]
        ys.append(_dot(diff.astype(BF16), pw_ref[g]))
    y = jnp.concatenate(ys, axis=1) * ps_ref[...]
    _post(x, y, mod, g2_ref[...], rwh_ref[...], rwl_ref[...], rb_ref[...],
          x1_ref, h2_ref, route_ref, cnt_ref)


def _pool_layer(xu, tinfo, mods, g1, pool_w, pool_scale, g2, rwh, rwl, rb):
    rows, d = xu.shape
    n_steps = tinfo[0].shape[0]
    n_e = rwh.shape[0]
    gc = pool_w.shape[-1]
    rpb = TM // SEG_ALIGN
    out_specs, out_shapes = _post_specs(n_steps, d, n_e)
    grid_spec = pltpu.PrefetchScalarGridSpec(
        num_scalar_prefetch=4,
        grid=(n_steps,),
        in_specs=[
            pl.BlockSpec((TM, d), lambda i, src, mrow, hp, hn: (src[i], 0)),
            pl.BlockSpec((SEG_ALIGN, d), lambda i, src, mrow, hp, hn: (jnp.maximum(src[i] * rpb - 1, 0), 0)),
            pl.BlockSpec((SEG_ALIGN, d),
                         lambda i, src, mrow, hp, hn: (jnp.minimum((src[i] + 1) * rpb, rows // SEG_ALIGN - 1), 0)),
            pl.BlockSpec((1, 6, d), lambda i, src, mrow, hp, hn: (mrow[i], 0, 0)),
            _const_spec((1, d)),
            _const_spec((len(POOL_WINDOWS), gc, gc)),
            _const_spec((1, d)),
            _const_spec((1, d)),
            _const_spec((n_e, d)),
            _const_spec((n_e, d)),
            _const_spec((n_e, 1)),
        ],
        out_specs=out_specs,
        scratch_shapes=[pltpu.VMEM((TM + 2 * SEG_ALIGN, d), F32)],
    )
    return pl.pallas_call(
        _pool_kernel, grid_spec=grid_spec, out_shape=out_shapes,
        compiler_params=_cparams(("arbitrary",)), name="pool_mixer",
    )(*tinfo, xu, xu, xu, mods, g1, pool_w.astype(BF16), pool_scale, g2, rwh, rwl, rb)


def _ssm_kernel(tile_of, x_ref, mod_ref, g1_ref, are_ref, aim_ref, bb_ref, cb_ref, y_ref,
                u_scr, xs_scr, y_scr, h_scr, *, n_ctx_tiles):
    dr = pl.program_id(0)
    i = pl.program_id(1)
    nb, tt, d = x_ref.shape
    n_sets = bb_ref.shape[1]
    sw = bb_ref.shape[3]
    hw = sw // 2
    uw = bb_ref.shape[2]
    is_ctx = tile_of[dr * pl.num_programs(1) + i] < n_ctx_tiles

    @pl.when(i == 0)
    def _():
        h_scr[...] = jnp.zeros_like(h_scr)

    g1 = g1_ref[...]
    for b in range(nb):
        shift = jnp.where(is_ctx, mod_ref[nb, 0:1, :], mod_ref[b, 0:1, :])
        scale = jnp.where(is_ctx, mod_ref[nb, 1:2, :], mod_ref[b, 1:2, :])
        hb = _rms(x_ref[b], g1) * (1.0 + scale) + shift
        for j in range(n_sets):
            u_scr[j, pl.ds(b, tt, stride=nb), :] = hb[:, j * uw:(j + 1) * uw]
    for j in range(n_sets):
        xs_scr[:, j * sw:(j + 1) * sw] = _dot(u_scr[j].astype(BF16), bb_ref[0, j])
    unroll = 4
    for j in range(n_sets):
        c_re = slice(j * sw, j * sw + hw)
        c_im = slice(j * sw + hw, (j + 1) * sw)
        ar = jnp.broadcast_to(are_ref[0, j:j + 1, :], (nb, hw))
        ai = jnp.broadcast_to(aim_ref[0, j:j + 1, :], (nb, hw))

        def body(s, carry, c_re=c_re, c_im=c_im, ar=ar, ai=ai):
            hr, hi = carry
            for q in range(unroll):
                step = s * unroll + q
                t = step + dr * (tt - 1 - 2 * step)
                r0 = pl.multiple_of(t * nb, nb)
                xr = xs_scr[pl.ds(r0, nb), c_re]
                xi = xs_scr[pl.ds(r0, nb), c_im]
                nhr = ar * hr - ai * hi + xr
                nhi = ar * hi + ai * hr + xi
                xs_scr[pl.ds(r0, nb), c_re] = nhr
                xs_scr[pl.ds(r0, nb), c_im] = nhi
                hr, hi = nhr, nhi
            return hr, hi

        hr, hi = lax.fori_loop(0, tt // unroll, body, (h_scr[:, c_re], h_scr[:, c_im]))
        h_scr[:, c_re] = hr
        h_scr[:, c_im] = hi
    for j in range(n_sets):
        y_scr[j] = _dot(xs_scr[:, j * sw:(j + 1) * sw].astype(BF16), cb_ref[0, j])
    for b in range(nb):
        for j in range(n_sets):
            y_ref[0, b, :, j * uw:(j + 1) * uw] = y_scr[j, pl.ds(b, tt, stride=nb), :]


def _ssm_scan(x3, mods, g1, a_re, a_im, bblk, cblk, n_ctx):
    nb, s, d = x3.shape
    tt = SSM_TT
    nt = s // tt
    nct = n_ctx // tt
    fwd = np.arange(nt)
    bwd = np.concatenate([np.arange(nct)[::-1], np.arange(nct, nt)[::-1]])
    tile_of = jnp.asarray(np.concatenate([fwd, bwd]), jnp.int32)
    n_sets, uw, sw = bblk.shape[1:]
    grid_spec = pltpu.PrefetchScalarGridSpec(
        num_scalar_prefetch=1,
        grid=(2, nt),
        in_specs=[
            pl.BlockSpec((nb, tt, d), lambda dr, i, to: (0, to[dr * nt + i], 0)),
            pl.BlockSpec(mods.shape, lambda dr, i, to: (0, 0, 0)),
            pl.BlockSpec((1, d), lambda dr, i, to: (0, 0)),
            pl.BlockSpec((1, n_sets, sw // 2), lambda dr, i, to: (dr, 0, 0)),
            pl.BlockSpec((1, n_sets, sw // 2), lambda dr, i, to: (dr, 0, 0)),
            pl.BlockSpec((1, n_sets, uw, sw), lambda dr, i, to: (dr, 0, 0, 0)),
            pl.BlockSpec((1, n_sets, sw, uw), lambda dr, i, to: (dr, 0, 0, 0)),
        ],
        out_specs=pl.BlockSpec((1, nb, tt, d), lambda dr, i, to: (dr, 0, to[dr * nt + i], 0)),
        scratch_shapes=[
            pltpu.VMEM((n_sets, tt * nb, uw), F32),
            pltpu.VMEM((tt * nb, n_sets * sw), F32),
            pltpu.VMEM((n_sets, tt * nb, uw), F32),
            pltpu.VMEM((nb, n_sets * sw), F32),
        ],
    )
    return pl.pallas_call(
        functools.partial(_ssm_kernel, n_ctx_tiles=nct), grid_spec=grid_spec,
        out_shape=jax.ShapeDtypeStruct((2, nb, s, d), F32),
        compiler_params=_cparams(("arbitrary", "arbitrary")), name="s5_scan",
    )(tile_of, x3, mods, g1, a_re, a_im, bblk, cblk)


def _ssm_params(lam_re, lam_im, log_dt, b_re, b_im, c_re, c_im):
    g, p = lam_re.shape[1:]
    h = b_re.shape[-1]
    ns = g // SSM_SET
    eye = jnp.eye(SSM_SET, dtype=F32)
    outs = []
    for dr in range(2):
        lr, li = lam_re[dr].astype(F32), lam_im[dr].astype(F32)
        br, bi = b_re[dr].astype(F32), b_im[dr].astype(F32)
        dt = jnp.exp(log_dt[dr].astype(F32))[:, None]
        zr, zi = lr * dt, li * dt
        mag = jnp.exp(zr)
        ar, ai = mag * jnp.cos(zi), mag * jnp.sin(zi)
        den = lr * lr + li * li
        cr = ((ar - 1.0) * lr + ai * li) / den
        ci = (ai * lr - (ar - 1.0) * li) / den
        bbr = cr[..., None] * br - ci[..., None] * bi
        bbi = cr[..., None] * bi + ci[..., None] * br

        def blk_b(w):
            w = jnp.transpose(w, (0, 2, 1)).reshape(ns, SSM_SET, h, p)
            return jnp.einsum('ab,jahp->jahbp', eye, w).reshape(ns, SSM_SET * h, SSM_SET * p)

        def blk_c(w):
            w = jnp.transpose(w.reshape(ns, SSM_SET, h, p), (0, 1, 3, 2))
            return jnp.einsum('ab,japh->japbh', eye, w).reshape(ns, SSM_SET * p, SSM_SET * h)

        bblk = jnp.concatenate([blk_b(bbr), blk_b(bbi)], axis=2)
        cblk = jnp.concatenate([blk_c(c_re[dr].astype(F32)), -blk_c(c_im[dr].astype(F32))], axis=1)
        outs.append((ar.reshape(ns, SSM_SET * p), ai.reshape(ns, SSM_SET * p), bblk, cblk))
    a_re = jnp.stack([o[0] for o in outs])
    a_im = jnp.stack([o[1] for o in outs])
    bblk = jnp.stack([o[2] for o in outs]).astype(BF16)
    cblk = jnp.stack([o[3] for o in outs]).astype(BF16)
    return a_re, a_im, bblk, cblk


def _glu_kernel(src, mrow, x_ref, yf_ref, yb_ref, mod_ref, g1_ref, dsk_ref, gw_ref, gb_ref,
                g2_ref, rwh_ref, rwl_ref, rb_ref, x1_ref, h2_ref, route_ref, cnt_ref):
    d = x_ref.shape[1]
    mod = mod_ref[0]
    x = x_ref[...]
    h = _rms(x, g1_ref[...]) * (1.0 + mod[1:2]) + mod[0:1]
    y = yf_ref[0] + yb_ref[0] + dsk_ref[...] * h
    gl = 0.5 * y * (1.0 + jnp.tanh(math.sqrt(2.0 / math.pi) * (y + 0.044715 * (y * y * y))))
    z = _dot(gl.astype(BF16), gw_ref[...]) + gb_ref[...]
    out = z[:, :d] * _sigmoid(z[:, d:])
    _post(x, out, mod, g2_ref[...], rwh_ref[...], rwl_ref[...], rb_ref[...],
          x1_ref, h2_ref, route_ref, cnt_ref)


def _glu_layer(xu, y2, tinfo, mods, g1, d_skip, glu_w, glu_b, g2, rwh, rwl, rb):
    rows, d = xu.shape
    n_steps = rows // TM
    n_e = rwh.shape[0]
    out_specs, out_shapes = _post_specs(n_steps, d, n_e)
    grid_spec = pltpu.PrefetchScalarGridSpec(
        num_scalar_prefetch=2,
        grid=(n_steps,),
        in_specs=[
            pl.BlockSpec((TM, d), lambda i, src, mrow: (i, 0)),
            pl.BlockSpec((1, TM, d), lambda i, src, mrow: (0, i, 0)),
            pl.BlockSpec((1, TM, d), lambda i, src, mrow: (1, i, 0)),
            pl.BlockSpec((1, 6, d), lambda i, src, mrow: (mrow[i], 0, 0)),
            _const_spec((1, d)),
            _const_spec((1, d)),
            _const_spec((d, 2 * d)),
            _const_spec((1, 2 * d)),
            _const_spec((1, d)),
            _const_spec((n_e, d)),
            _const_spec((n_e, d)),
            _const_spec((n_e, 1)),
        ],
        out_specs=out_specs,
    )
    return pl.pallas_call(
        _glu_kernel, grid_spec=grid_spec, out_shape=out_shapes,
        compiler_params=_cparams(("arbitrary",)), name="s5_glu",
    )(tinfo[0], tinfo[1], xu, y2, y2, mods, g1, d_skip, glu_w.astype(BF16), glu_b, g2, rwh, rwl, rb)


def _rope(v, cos, sin_signed, first_half):
    partner = jnp.where(first_half, pltpu.roll(v, HEAD_DIM - HEAD_DIM // 4, axis=1),
                        pltpu.roll(v, HEAD_DIM // 4, axis=1))
    return v * cos + partner * sin_signed


def _qkv_kernel(src, mrow, trow, x_ref, mod_ref, g1_ref, w_ref, qg_ref, kg_ref, cos_ref, sin_ref,
                q_ref, k_ref, v_ref):
    d = x_ref.shape[1]
    kvw = k_ref.shape[1]
    mod = mod_ref[0]
    h = _rms(x_ref[...], g1_ref[...]) * (1.0 + mod[1:2]) + mod[0:1]
    z = _dot(h.astype(BF16), w_ref[...])
    cos = cos_ref[...]
    sin = sin_ref[...]
    lane = lax.broadcasted_iota(jnp.int32, (TM, HEAD_DIM), 1)
    first_half = (lane % (HEAD_DIM // 2)) < (HEAD_DIM // 4)
    q_scale = HEAD_DIM ** -0.5
    for hd in range(d // HEAD_DIM):
        zh = z[:, hd * HEAD_DIM:(hd + 1) * HEAD_DIM]
        zh = _rope(_rms(zh, qg_ref[...]), cos, sin, first_half) * q_scale
        q_ref[:, hd * HEAD_DIM:(hd + 1) * HEAD_DIM] = zh.astype(BF16)
    for hd in range(kvw // HEAD_DIM):
        zh = z[:, d + hd * HEAD_DIM:d + (hd + 1) * HEAD_DIM]
        zh = _rope(_rms(zh, kg_ref[...]), cos, sin, first_half)
        k_ref[:, hd * HEAD_DIM:(hd + 1) * HEAD_DIM] = zh.astype(BF16)
    v_ref[...] = z[:, d + kvw:].astype(BF16)


def _attn_kernel(q_ref, k_ref, v_ref, o_ref, *, n_ctx):
    qt = pl.program_id(2)
    q = q_ref[...]

    def attend(k, v):
        s = _dot_nt(q, k)
        m = jnp.max(s, axis=-1, keepdims=True)
        p = jnp.exp(s - m)
        den = jnp.sum(p, axis=-1, keepdims=True)
        return _dot(p.astype(BF16), v) / den

    @pl.when(qt < n_ctx // TM)
    def _():
        o_ref[...] = attend(k_ref[0:n_ctx, :], v_ref[0:n_ctx, :]).astype(BF16)

    @pl.when(qt >= n_ctx // TM)
    def _():
        o_ref[...] = attend(k_ref[...], v_ref[...]).astype(BF16)


def _wo_kernel(src, mrow, x_ref, o_ref, mod_ref, wo_ref, g2_ref, rwh_ref, rwl_ref, rb_ref,
               x1_ref, h2_ref, route_ref, cnt_ref):
    y = _dot(o_ref[...], wo_ref[...])
    _post(x_ref[...], y, mod_ref[0], g2_ref[...], rwh_ref[...], rwl_ref[...], rb_ref[...],
          x1_ref, h2_ref, route_ref, cnt_ref)


def _rope_tables(n_ctx, n_lat):
    rows = n_lat // GRID_W
    row = jnp.repeat(jnp.arange(rows), GRID_W)
    col = jnp.tile(jnp.arange(GRID_W), rows)
    pos = jnp.stack([row, col], axis=-1).astype(F32)
    inv_freq = ROPE_THETA ** (-jnp.arange(ROPE_F, dtype=F32) / ROPE_F)
    ang = pos[:, :, None] * inv_freq
    cos, sin = jnp.cos(ang), jnp.sin(ang)
    cos_t = jnp.concatenate([cos, cos], axis=-1).reshape(n_lat, HEAD_DIM)
    sin_t = jnp.concatenate([-sin, sin], axis=-1).reshape(n_lat, HEAD_DIM)
    cos_t = jnp.concatenate([jnp.ones((n_ctx, HEAD_DIM), F32), cos_t], axis=0)
    sin_t = jnp.concatenate([jnp.zeros((n_ctx, HEAD_DIM), F32), sin_t], axis=0)
    return cos_t, sin_t


def _attn_layer(xu, tinfo, trow, mods, g1, wqkv, q_g, k_g, wo, g2, rwh, rwl, rb, nb, n_ctx):
    rows, d = xu.shape
    s = rows // nb
    n_steps = rows // TM
    n_e = rwh.shape[0]
    kvw = N_KV_HEADS * HEAD_DIM
    n_heads = d // HEAD_DIM
    rep = n_heads // N_KV_HEADS
    cos_t, sin_t = _rope_tables(n_ctx, s - n_ctx)
    grid_spec = pltpu.PrefetchScalarGridSpec(
        num_scalar_prefetch=3,
        grid=(n_steps,),
        in_specs=[
            pl.BlockSpec((TM, d), lambda i, src, mrow, tr: (i, 0)),
            pl.BlockSpec((1, 6, d), lambda i, src, mrow, tr: (mrow[i], 0, 0)),
            _const_spec((1, d)),
            _const_spec((d, d + 2 * kvw)),
            _const_spec((1, HEAD_DIM)),
            _const_spec((1, HEAD_DIM)),
            pl.BlockSpec((TM, HEAD_DIM), lambda i, src, mrow, tr: (tr[i], 0)),
            pl.BlockSpec((TM, HEAD_DIM), lambda i, src, mrow, tr: (tr[i], 0)),
        ],
        out_specs=[
            pl.BlockSpec((TM, d), lambda i, *_: (i, 0)),
            pl.BlockSpec((TM, kvw), lambda i, *_: (i, 0)),
            pl.BlockSpec((TM, kvw), lambda i, *_: (i, 0)),
        ],
    )
    q, k, v = pl.pallas_call(
        _qkv_kernel, grid_spec=grid_spec,
        out_shape=[jax.ShapeDtypeStruct((rows, d), BF16),
                   jax.ShapeDtypeStruct((rows, kvw), BF16),
                   jax.ShapeDtypeStruct((rows, kvw), BF16)],
        compiler_params=_cparams(("arbitrary",)), name="attn_qkv",
    )(tinfo[0], tinfo[1], trow, xu, mods, g1, wqkv.astype(BF16), q_g, k_g, cos_t, sin_t)

    tpb = s // TM
    o = pl.pallas_call(
        functools.partial(_attn_kernel, n_ctx=n_ctx),
        grid=(nb, n_heads, tpb),
        in_specs=[
            pl.BlockSpec((TM, HEAD_DIM), lambda b, hd, t: (b * tpb + t, hd)),
            pl.BlockSpec((s, HEAD_DIM), lambda b, hd, t: (b, hd // rep)),
            pl.BlockSpec((s, HEAD_DIM), lambda b, hd, t: (b, hd // rep)),
        ],
        out_specs=pl.BlockSpec((TM, HEAD_DIM), lambda b, hd, t: (b * tpb + t, hd)),
        out_shape=jax.ShapeDtypeStruct((rows, d), BF16),
        compiler_params=_cparams(("arbitrary", "arbitrary", "arbitrary")), name="attn_core",
    )(q, k, v)

    out_specs, out_shapes = _post_specs(n_steps, d, n_e)
    grid_spec = pltpu.PrefetchScalarGridSpec(
        num_scalar_prefetch=2,
        grid=(n_steps,),
        in_specs=[
            pl.BlockSpec((TM, d), lambda i, src, mrow: (i, 0)),
            pl.BlockSpec((TM, d), lambda i, src, mrow: (i, 0)),
            pl.BlockSpec((1, 6, d), lambda i, src, mrow: (mrow[i], 0, 0)),
            _const_spec((d, d)),
            _const_spec((1, d)),
            _const_spec((n_e, d)),
            _const_spec((n_e, d)),
            _const_spec((n_e, 1)),
        ],
        out_specs=out_specs,
    )
    return pl.pallas_call(
        _wo_kernel, grid_spec=grid_spec, out_shape=out_shapes,
        compiler_params=_cparams(("arbitrary",)), name="attn_out",
    )(tinfo[0], tinfo[1], xu, o, mods, wo.astype(BF16), g2, rwh, rwl, rb)


def _loc_rows(n_e):
    return -(-(TM * TOP_K + (SEG_ALIGN - 1) * n_e) // 256) * 256


def _segment_copies(i, n_e, loc, len8, dst, make_copy, sems, wait):
    def per_expert(e, carry):
        a = loc[i * n_e + e]
        n = len8[i * n_e + e]
        g = dst[i * n_e + e]
        for bi, bit in enumerate(SEG_BITS):
            done = n & (-2 * bit)

            @pl.when((n & bit) != 0)
            def _(bi=bi, bit=bit, done=done):
                cp = make_copy(pl.multiple_of(a + done, SEG_ALIGN), pl.multiple_of(g + done, SEG_ALIGN),
                               bit, sems.at[e, bi])
                if wait:
                    cp.wait()
                else:
                    cp.start()
        return carry

    lax.fori_loop(0, n_e, per_expert, 0)


def _dispatch_kernel(loc, len8, dst, h2_ref, route_ref, xs_ref, buf, sems, *, n_e):
    i = pl.program_id(0)
    lrows = buf.shape[0]
    pos = route_ref[0, 0:TOP_K, :]
    iota_p = lax.broadcasted_iota(jnp.int32, (lrows, TM), 0).astype(F32)
    perm = jnp.zeros((lrows, TM), F32)
    for k in range(TOP_K):
        perm = perm + jnp.where(iota_p == pos[k:k + 1], 1.0, 0.0)
    buf[...] = _dot(perm.astype(BF16), h2_ref[...])

    def make_copy(a, g, size, sem):
        return pltpu.make_async_copy(buf.at[pl.ds(a, size)], xs_ref.at[pl.ds(g, size)], sem)

    _segment_copies(i, n_e, loc, len8, dst, make_copy, sems, wait=False)
    _segment_copies(i, n_e, loc, len8, dst, make_copy, sems, wait=True)


def _expert_kernel(blk, blk_e, valid, nused, xs_ref, w1_ref, b1_ref, w2_ref, b2_ref, ys_ref, w1b, w2b):
    i = pl.program_id(0)
    f = w2_ref.shape[1]

    @pl.when(i < nused[0])
    def _():
        first = jnp.logical_or(i == 0, blk_e[i] != blk_e[jnp.maximum(i - 1, 0)])

        @pl.when(first)
        def _():
            w1b[...] = w1_ref[0].astype(BF16)
            w2b[...] = w2_ref[0].astype(BF16)

        row = lax.broadcasted_iota(jnp.int32, (BLK, 1), 0)
        x = jnp.where(row < valid[i], xs_ref[...], 0.0).astype(BF16)
        z = _dot(x, w1b[...]) + b1_ref[0]
        glu = jnp.minimum(z[:, :f], SWIGLU_LIMIT)
        lin = jnp.clip(z[:, f:], -SWIGLU_LIMIT, SWIGLU_LIMIT)
        act = glu * _sigmoid(SWIGLU_ALPHA * glu) * (lin + 1.0)
        ys_ref[...] = _dot(act.astype(BF16), w2b[...]) + b2_ref[0]


def _combine_kernel(loc, len8, dst, mrow, ys_ref, route_ref, x1_ref, mod_ref, out_ref, buf, sems, *, n_e):
    i = pl.program_id(0)
    lrows = buf.shape[0]

    def make_copy(a, g, size, sem):
        return pltpu.make_async_copy(ys_ref.at[pl.ds(g, size)], buf.at[pl.ds(a, size)], sem)

    _segment_copies(i, n_e, loc, len8, dst, make_copy, sems, wait=False)
    _segment_copies(i, n_e, loc, len8, dst, make_copy, sems, wait=True)
    total = loc[i * n_e + n_e - 1] + len8[i * n_e + n_e - 1]
    pos = route_ref[0, 0:TOP_K, :]
    gate = route_ref[0, TOP_K:2 * TOP_K, :]
    iota_p = lax.broadcasted_iota(jnp.int32, (lrows, TM), 0).astype(F32)
    gt = jnp.zeros((lrows, TM), F32)
    for k in range(TOP_K):
        gt = gt + jnp.where(iota_p == pos[k:k + 1], gate[k:k + 1], 0.0)
    row = lax.broadcasted_iota(jnp.int32, (lrows, 1), 0)
    ysl = jnp.where(row < total, buf[...], 0.0).astype(BF16)
    f = _dot_tn(gt.astype(BF16), ysl)
    out_ref[...] = x1_ref[...] + mod_ref[0][5:6] * f


def _moe(x1, h2, route, cnt, mods, mrow, w1, b1, w2, b2):
    rows, d = x1.shape
    n_tiles = rows // TM
    n_e, _, f2 = w1.shape
    lrows = _loc_rows(n_e)
    nb_max = (rows * TOP_K + (SEG_ALIGN - 1) * n_tiles * n_e) // BLK + n_e
    cap = nb_max * BLK

    cnt = cnt.reshape(n_tiles, n_e).astype(jnp.int32)
    c8 = (cnt + SEG_ALIGN - 1) // SEG_ALIGN * SEG_ALIGN
    loc = jnp.cumsum(c8, axis=1) - c8
    tot = jnp.sum(c8, axis=0)
    nblk = (tot + BLK - 1) // BLK
    blk_end = jnp.cumsum(nblk)
    blk_start = blk_end - nblk
    dst = (blk_start * BLK)[None, :] + jnp.cumsum(c8, axis=0) - c8
    nused = blk_end[-1]
    bid = jnp.arange(nb_max, dtype=jnp.int32)
    bidc = jnp.minimum(bid, nused - 1)
    blk_e = jnp.minimum(jnp.searchsorted(blk_end, bidc, side='right'), n_e - 1).astype(jnp.int32)
    valid = jnp.clip(tot[blk_e] - (bidc - blk_start[blk_e]) * BLK, 0, BLK).astype(jnp.int32)
    loc = loc.reshape(-1).astype(jnp.int32)
    len8 = c8.reshape(-1).astype(jnp.int32)
    dst = dst.reshape(-1).astype(jnp.int32)
    nused = nused.reshape(1).astype(jnp.int32)
    bidc = bidc.astype(jnp.int32)

    n_bits = len(SEG_BITS)
    xs = pl.pallas_call(
        functools.partial(_dispatch_kernel, n_e=n_e),
        grid_spec=pltpu.PrefetchScalarGridSpec(
            num_scalar_prefetch=3,
            grid=(n_tiles,),
            in_specs=[
                pl.BlockSpec((TM, d), lambda i, *_: (i, 0)),
                pl.BlockSpec((1, 2 * TOP_K, TM), lambda i, *_: (i, 0, 0)),
            ],
            out_specs=pl.BlockSpec(memory_space=pl.ANY),
            scratch_shapes=[pltpu.VMEM((lrows, d), F32), pltpu.SemaphoreType.DMA((n_e, n_bits))],
        ),
        out_shape=jax.ShapeDtypeStruct((cap, d), F32),
        compiler_params=_cparams(("arbitrary",)), name="moe_dispatch",
    )(loc, len8, dst, h2, route)

    ys = pl.pallas_call(
        _expert_kernel,
        grid_spec=pltpu.PrefetchScalarGridSpec(
            num_scalar_prefetch=4,
            grid=(nb_max,),
            in_specs=[
                pl.BlockSpec((BLK, d), lambda i, blk, be, va, nu: (blk[i], 0)),
                pl.BlockSpec((1, d, f2), lambda i, blk, be, va, nu: (be[i], 0, 0)),
                pl.BlockSpec((1, 1, f2), lambda i, blk, be, va, nu: (be[i], 0, 0)),
                pl.BlockSpec((1, f2 // 2, d), lambda i, blk, be, va, nu: (be[i], 0, 0)),
                pl.BlockSpec((1, 1, d), lambda i, blk, be, va, nu: (be[i], 0, 0)),
            ],
            out_specs=pl.BlockSpec((BLK, d), lambda i, blk, be, va, nu: (blk[i], 0)),
            scratch_shapes=[pltpu.VMEM((d, f2), BF16), pltpu.VMEM((f2 // 2, d), BF16)],
        ),
        out_shape=jax.ShapeDtypeStruct((cap, d), F32),
        compiler_params=_cparams(("arbitrary",)), name="moe_experts",
    )(bidc, blk_e, valid, nused, xs, w1, b1.reshape(n_e, 1, f2), w2, b2.reshape(n_e, 1, d))

    return pl.pallas_call(
        functools.partial(_combine_kernel, n_e=n_e),
        grid_spec=pltpu.PrefetchScalarGridSpec(
            num_scalar_prefetch=4,
            grid=(n_tiles,),
            in_specs=[
                pl.BlockSpec(memory_space=pl.ANY),
                pl.BlockSpec((1, 2 * TOP_K, TM), lambda i, *_: (i, 0, 0)),
                pl.BlockSpec((TM, d), lambda i, *_: (i, 0)),
                pl.BlockSpec((1, 6, d), lambda i, lo, le, ds, mr: (mr[i], 0, 0)),
            ],
            out_specs=pl.BlockSpec((TM, d), lambda i, *_: (i, 0)),
            scratch_shapes=[pltpu.VMEM((lrows, d), F32), pltpu.SemaphoreType.DMA((n_e, n_bits))],
        ),
        out_shape=jax.ShapeDtypeStruct((rows, d), F32),
        compiler_params=_cparams(("arbitrary",)), name="moe_combine",
    )(loc, len8, dst, mrow, ys, route, x1, mods)


def _tile_info(nb, n_ctx, n_lat, latent_only):
    tpb = (n_ctx + n_lat) // TM
    ct = n_ctx // TM
    src, mrow, hp, hn, trow = [], [], [], [], []
    for b in range(nb):
        for j in range(ct if latent_only else 0, tpb):
            is_ctx = j < ct
            src.append(b * tpb + j)
            mrow.append(nb if is_ctx else b)
            hp.append(0 if j in (0, ct) else 1)
            hn.append(0 if j in (ct - 1, tpb - 1) else 1)
            trow.append(j)
    mk = lambda v: jnp.asarray(np.asarray(v, np.int32))
    return (mk(src), mk(mrow), mk(hp), mk(hn)), mk(trow)


def kernel(x, c, ctx, c_ctx, ada_w, ada_b, norm1_g, norm2_g, pool_w, pool_scale, ssm_lam_re, ssm_lam_im,
           ssm_log_dt, ssm_b_re, ssm_b_im, ssm_c_re, ssm_c_im, ssm_d, ssm_glu_w, ssm_glu_b, attn_wqkv,
           attn_q_g, attn_k_g, attn_wo, router_w, router_b, moe_w1, moe_b1, moe_w2, moe_b2):
    nb, n_lat, d = x.shape
    n_ctx = ctx.shape[1]
    depth = ada_w.shape[0]
    n_e = router_w.shape[-1]
    assert nb == 8 and n_ctx % TM == 0 and n_lat % TM == 0 and n_lat % GRID_W == 0
    s = n_ctx + n_lat

    c16 = jnp.concatenate([c, c_ctx[None, :], jnp.zeros((16 - nb - 1, d), F32)], axis=0)
    mods_all = _ada_mods(c16, ada_w, ada_b).reshape(depth, 16, 6, d)

    xu = jnp.concatenate([ctx, x], axis=1).reshape(nb * s, d)
    tinfo_u, trow_u = _tile_info(nb, n_ctx, n_lat, latent_only=False)
    tinfo_l, _ = _tile_info(nb, n_ctx, n_lat, latent_only=True)

    for i in range(depth):
        kind, j = i % 3, i // 3
        last = i == depth - 1
        mods = mods_all[i]
        g1 = norm1_g[i].reshape(1, d)
        g2 = norm2_g[i].reshape(1, d)
        rwt = router_w[i].T
        rwh = rwt.astype(BF16)
        rwl = (rwt - rwh.astype(F32)).astype(BF16)
        rb = router_b[i].reshape(n_e, 1)
        if kind == 0:
            tinfo = tinfo_l if last else tinfo_u
            x1, h2, route, cnt = _pool_layer(xu, tinfo, mods, g1, pool_w[j], pool_scale[j].reshape(1, d),
                                             g2, rwh, rwl, rb)
            mrow = tinfo[1]
        elif kind == 1:
            assert not last
            a_re, a_im, bblk, cblk = _ssm_params(ssm_lam_re[j], ssm_lam_im[j], ssm_log_dt[j], ssm_b_re[j],
                                                 ssm_b_im[j], ssm_c_re[j], ssm_c_im[j])
            y2 = _ssm_scan(xu.reshape(nb, s, d), mods, g1, a_re, a_im, bblk, cblk, n_ctx)
            x1, h2, route, cnt = _glu_layer(xu, y2.reshape(2, nb * s, d), tinfo_u, mods, g1,
                                            ssm_d[j].reshape(1, d), ssm_glu_w[j], ssm_glu_b[j].reshape(1, 2 * d),
                                            g2, rwh, rwl, rb)
            mrow = tinfo_u[1]
        else:
            assert not last
            x1, h2, route, cnt = _attn_layer(xu, tinfo_u, trow_u, mods, g1, attn_wqkv[j],
                                             attn_q_g[j].reshape(1, HEAD_DIM), attn_k_g[j].reshape(1, HEAD_DIM),
                                             attn_wo[j], g2, rwh, rwl, rb, nb, n_ctx)
            mrow = tinfo_u[1]
        xu = _moe(x1, h2, route, cnt, mods, mrow, moe_w1[i], moe_b1[i], moe_w2[i], moe_b2[i])
    if xu.shape[0] == nb * n_lat:
        return xu.reshape(nb, n_lat, d)
    return xu.reshape(nb, s, d)[:, n_ctx:, :]
```

```python
import functools
import math

import numpy as np
import jax
import jax.numpy as jnp
from jax import lax
from jax.experimental import pallas as pl
from jax.experimental.pallas import tpu as pltpu

F32 = jnp.float32
BF16 = jnp.bfloat16

GRID_W = 64
NORM_EPS = 1e-6
POOL_WINDOWS = (2, 4, 8, 16)
SSM_H = 16
SSM_P = 64
SSM_SET = 8
SSM_TT = 32
HEAD_DIM = 128
N_KV_HEADS = 2
ROPE_F = HEAD_DIM // 4
ROPE_THETA = 10000.0
TOP_K = 4
SWIGLU_ALPHA = 1.702
SWIGLU_LIMIT = 7.0

TM = 256
BLK = 512
SEG_ALIGN = 8
VMEM_LIMIT = 56 * 1024 * 1024


def _cparams(sem, vmem=VMEM_LIMIT):
    return pltpu.CompilerParams(dimension_semantics=sem, vmem_limit_bytes=vmem)


def _rms(x, g):
    return x * lax.rsqrt(jnp.mean(x * x, axis=-1, keepdims=True) + NORM_EPS) * g


def _sigmoid(x):
    return 1.0 / (1.0 + jnp.exp(-x))


def _dot(a, b):
    return jnp.dot(a, b, preferred_element_type=F32)


def _dot_nt(a, b):
    return lax.dot_general(a, b, (((1,), (1,)), ((), ())), preferred_element_type=F32)


def _dot_tn(a, b):
    return lax.dot_general(a, b, (((0,), (0,)), ((), ())), preferred_element_type=F32)


def _ada_kernel(c_ref, w_ref, b_ref, o_ref):
    c = c_ref[...]
    s = c * _sigmoid(c)
    o_ref[0] = jnp.dot(s, w_ref[0], preferred_element_type=F32,
                       precision=lax.Precision.HIGHEST) + b_ref[0]


def _ada_mods(c16, ada_w, ada_b):
    depth, d, six_d = ada_w.shape
    tn = d
    return pl.pallas_call(
        _ada_kernel,
        grid=(depth, six_d // tn),
        in_specs=[
            pl.BlockSpec((16, d), lambda l, j: (0, 0)),
            pl.BlockSpec((1, d, tn), lambda l, j: (l, 0, j)),
            pl.BlockSpec((1, 1, tn), lambda l, j: (l, 0, j)),
        ],
        out_specs=pl.BlockSpec((1, 16, tn), lambda l, j: (l, 0, j)),
        out_shape=jax.ShapeDtypeStruct((depth, 16, six_d), F32),
        compiler_params=_cparams(("arbitrary", "arbitrary")),
        name="ada_mods",
    )(c16, ada_w, ada_b.reshape(depth, 1, six_d))


def _post(x, y, mod, g2, rwh, rwl, rb, x1_ref, h2_ref, route_ref, cnt_ref):
    n_e = rwh.shape[0]
    x1 = x + mod[2:3] * y
    x1_ref[...] = x1
    h2 = _rms(x1, g2) * (1.0 + mod[4:5]) + mod[3:4]
    h2_ref[...] = h2.astype(BF16)
    hh = h2.astype(BF16)
    hl = (h2 - hh.astype(F32)).astype(BF16)
    logits = _dot_nt(rwh, hh) + _dot_nt(rwh, hl) + _dot_nt(rwl, hh) + rb
    iota_e = lax.broadcasted_iota(jnp.int32, (n_e, TM), 0)
    vals, onehots = [], []
    l = logits
    for _ in range(TOP_K):
        m = jnp.max(l, axis=0, keepdims=True)
        idx = jnp.min(jnp.where(l == m, iota_e, n_e), axis=0, keepdims=True)
        sel = iota_e == idx
        vals.append(m)
        onehots.append(sel)
        l = jnp.where(sel, -jnp.inf, l)
    ex = [jnp.exp(v - vals[0]) for v in vals]
    den = ex[0] + ex[1] + ex[2] + ex[3]
    gates = [e / den for e in ex]
    member = jnp.zeros((n_e, TM), F32)
    for sel in onehots:
        member = member + jnp.where(sel, 1.0, 0.0)
    r_i = lax.broadcasted_iota(jnp.int32, (TM, TM), 0)
    c_i = lax.broadcasted_iota(jnp.int32, (TM, TM), 1)
    upper = jnp.where(r_i < c_i, 1.0, 0.0).astype(BF16)
    cum = _dot(member.astype(BF16), upper)
    cnt = jnp.sum(member, axis=1, keepdims=True)
    cnt_ref[0] = cnt
    c8 = jnp.floor((cnt + (SEG_ALIGN - 1)) * (1.0 / SEG_ALIGN)) * SEG_ALIGN
    e_r = lax.broadcasted_iota(jnp.int32, (n_e, n_e), 0)
    e_c = lax.broadcasted_iota(jnp.int32, (n_e, n_e), 1)
    lower = jnp.where(e_c < e_r, 1.0, 0.0).astype(BF16)
    seg = _dot(lower, jnp.broadcast_to(c8, (n_e, TM)).astype(BF16))
    base = seg + cum
    rows = []
    for sel in onehots:
        rows.append(jnp.sum(jnp.where(sel, base, 0.0), axis=0, keepdims=True))
    route_ref[0] = jnp.concatenate(rows + gates, axis=0)


def _post_specs(n_steps, d, n_e):
    specs = [
        pl.BlockSpec((TM, d), lambda i, *_: (i, 0)),
        pl.BlockSpec((TM, d), lambda i, *_: (i, 0)),
        pl.BlockSpec((1, 2 * TOP_K, TM), lambda i, *_: (i, 0, 0)),
        pl.BlockSpec((1, n_e, 1), lambda i, *_: (i, 0, 0)),
    ]
    shapes = [
        jax.ShapeDtypeStruct((n_steps * TM, d), F32),
        jax.ShapeDtypeStruct((n_steps * TM, d), BF16),
        jax.ShapeDtypeStruct((n_steps, 2 * TOP_K, TM), F32),
        jax.ShapeDtypeStruct((n_steps, n_e, 1), F32),
    ]
    return specs, shapes


def _const_spec(shape):
    nd = len(shape)
    return pl.BlockSpec(shape, lambda i, *_: (0,) * nd)


def _pool_kernel(src, mrow, hp, hn, x_ref, xp_ref, xn_ref, mod_ref, g1_ref, pw_ref, ps_ref,
                 g2_ref, rwh_ref, rwl_ref, rb_ref, x1_ref, h2_ref, route_ref, cnt_ref, hh_scr):
    i = pl.program_id(0)
    d = x_ref.shape[1]
    gc = d // len(POOL_WINDOWS)
    halo = SEG_ALIGN
    mod = mod_ref[0]
    g1 = g1_ref[...]

    def pre(v):
        return _rms(v, g1) * (1.0 + mod[1:2]) + mod[0:1]

    has_prev = hp[i] > 0
    has_next = hn[i] > 0
    x = x_ref[...]
    h = pre(x)
    hh_scr[0:halo, :] = jnp.where(has_prev, pre(xp_ref[...]), 0.0)
    hh_scr[halo:halo + TM, :] = h
    hh_scr[halo + TM:2 * halo + TM, :] = jnp.where(has_next, pre(xn_ref[...]), 0.0)
    row = lax.broadcasted_iota(jnp.int32, (TM, 1), 0)
    ys = []
    for g, win in enumerate(POOL_WINDOWS):
        half = win // 2
        c0 = g * gc
        acc = hh_scr[pl.ds(halo - half, TM), c0:c0 + gc]
        for j in range(-half + 1, half):
            acc = acc + hh_scr[pl.ds(halo + j, TM), c0:c0 + gc]
        lo_clip = jnp.where(has_prev, 0, jnp.maximum(half - row, 0))
        hi_clip = jnp.where(has_next, 0, jnp.maximum(row + half - TM, 0))
        cnt = (win - lo_clip - hi_clip).astype(F32)
        diff = acc / cnt - h[:, c0:c0 + gc]
        ys.append(_dot(diff.astype(BF16), pw_ref[g]))
    y = jnp.concatenate(ys, axis=1) * ps_ref[...]
    _post(x, y, mod, g2_ref[...], rwh_ref[...], rwl_ref[...], rb_ref[...],
          x1_ref, h2_ref, route_ref, cnt_ref)


def _pool_layer(xu, tinfo, mods, g1, pool_w, pool_scale, g2, rwh, rwl, rb):
    rows, d = xu.shape
    n_steps = tinfo[0].shape[0]
    n_e = rwh.shape[0]
    gc = pool_w.shape[-1]
    rpb = TM // SEG_ALIGN
    out_specs, out_shapes = _post_specs(n_steps, d, n_e)
    grid_spec = pltpu.PrefetchScalarGridSpec(
        num_scalar_prefetch=4,
        grid=(n_steps,),
        in_specs=[
            pl.BlockSpec((TM, d), lambda i, src, mrow, hp, hn: (src[i], 0)),
            pl.BlockSpec((SEG_ALIGN, d), lambda i, src, mrow, hp, hn: (jnp.maximum(src[i] * rpb - 1, 0), 0)),
            pl.BlockSpec((SEG_ALIGN, d),
                         lambda i, src, mrow, hp, hn: (jnp.minimum((src[i] + 1) * rpb, rows // SEG_ALIGN - 1), 0)),
            pl.BlockSpec((1, 6, d), lambda i, src, mrow, hp, hn: (mrow[i], 0, 0)),
            _const_spec((1, d)),
            _const_spec((len(POOL_WINDOWS), gc, gc)),
            _const_spec((1, d)),
            _const_spec((1, d)),
            _const_spec((n_e, d)),
            _const_spec((n_e, d)),
            _const_spec((n_e, 1)),
        ],
        out_specs=out_specs,
        scratch_shapes=[pltpu.VMEM((TM + 2 * SEG_ALIGN, d), F32)],
    )
    return pl.pallas_call(
        _pool_kernel, grid_spec=grid_spec, out_shape=out_shapes,
        compiler_params=_cparams(("arbitrary",)), name="pool_mixer",
    )(*tinfo, xu, xu, xu, mods, g1, pool_w.astype(BF16), pool_scale, g2, rwh, rwl, rb)


def _ssm_kernel(tile_of, x_ref, mod_ref, g1_ref, are_ref, aim_ref, bb_ref, cb_ref, y_ref,
                u_scr, xs_scr, y_scr, h_scr, *, n_ctx_tiles):
    dr = pl.program_id(0)
    i = pl.program_id(1)
    nb, tt, d = x_ref.shape
    n_sets = bb_ref.shape[1]
    sw = bb_ref.shape[3]
    hw = sw // 2
    uw = bb_ref.shape[2]
    is_ctx = tile_of[dr * pl.num_programs(1) + i] < n_ctx_tiles

    @pl.when(i == 0)
    def _():
        h_scr[...] = jnp.zeros_like(h_scr)

    g1 = g1_ref[...]
    for b in range(nb):
        shift = jnp.where(is_ctx, mod_ref[nb, 0:1, :], mod_ref[b, 0:1, :])
        scale = jnp.where(is_ctx, mod_ref[nb, 1:2, :], mod_ref[b, 1:2, :])
        hb = _rms(x_ref[b], g1) * (1.0 + scale) + shift
        for j in range(n_sets):
            u_scr[j, pl.ds(b, tt, stride=nb), :] = hb[:, j * uw:(j + 1) * uw]
    for j in range(n_sets):
        xs_scr[:, j * sw:(j + 1) * sw] = _dot(u_scr[j].astype(BF16), bb_ref[0, j])
    unroll = 4
    for j in range(n_sets):
        c_re = slice(j * sw, j * sw + hw)
        c_im = slice(j * sw + hw, (j + 1) * sw)
        ar = jnp.broadcast_to(are_ref[0, j:j + 1, :], (nb, hw))
        ai = jnp.broadcast_to(aim_ref[0, j:j + 1, :], (nb, hw))

        def body(s, carry, c_re=c_re, c_im=c_im, ar=ar, ai=ai):
            hr, hi = carry
            for q in range(unroll):
                step = s * unroll + q
                t = step + dr * (tt - 1 - 2 * step)
                r0 = pl.multiple_of(t * nb, nb)
                xr = xs_scr[pl.ds(r0, nb), c_re]
                xi = xs_scr[pl.ds(r0, nb), c_im]
                nhr = ar * hr - ai * hi + xr
                nhi = ar * hi + ai * hr + xi
                xs_scr[pl.ds(r0, nb), c_re] = nhr
                xs_scr[pl.ds(r0, nb), c_im] = nhi
                hr, hi = nhr, nhi
            return hr, hi

        hr, hi = lax.fori_loop(0, tt // unroll, body, (h_scr[:, c_re], h_scr[:, c_im]))
        h_scr[:, c_re] = hr
        h_scr[:, c_im] = hi
    for j in range(n_sets):
        y_scr[j] = _dot(xs_scr[:, j * sw:(j + 1) * sw].astype(BF16), cb_ref[0, j])
    for b in range(nb):
        for j in range(n_sets):
            y_ref[0, b, :, j * uw:(j + 1) * uw] = y_scr[j, pl.ds(b, tt, stride=nb), :]


def _ssm_scan(x3, mods, g1, a_re, a_im, bblk, cblk, n_ctx):
    nb, s, d = x3.shape
    tt = SSM_TT
    nt = s // tt
    nct = n_ctx // tt
    fwd = np.arange(nt)
    bwd = np.concatenate([np.arange(nct)[::-1], np.arange(nct, nt)[::-1]])
    tile_of = jnp.asarray(np.concatenate([fwd, bwd]), jnp.int32)
    n_sets, uw, sw = bblk.shape[1:]
    grid_spec = pltpu.PrefetchScalarGridSpec(
        num_scalar_prefetch=1,
        grid=(2, nt),
        in_specs=[
            pl.BlockSpec((nb, tt, d), lambda dr, i, to: (0, to[dr * nt + i], 0)),
            pl.BlockSpec(mods.shape, lambda dr, i, to: (0, 0, 0)),
            pl.BlockSpec((1, d), lambda dr, i, to: (0, 0)),
            pl.BlockSpec((1, n_sets, sw // 2), lambda dr, i, to: (dr, 0, 0)),
            pl.BlockSpec((1, n_sets, sw // 2), lambda dr, i, to: (dr, 0, 0)),
            pl.BlockSpec((1, n_sets, uw, sw), lambda dr, i, to: (dr, 0, 0, 0)),
            pl.BlockSpec((1, n_sets, sw, uw), lambda dr, i, to: (dr, 0, 0, 0)),
        ],
        out_specs=pl.BlockSpec((1, nb, tt, d), lambda dr, i, to: (dr, 0, to[dr * nt + i], 0)),
        scratch_shapes=[
            pltpu.VMEM((n_sets, tt * nb, uw), F32),
            pltpu.VMEM((tt * nb, n_sets * sw), F32),
            pltpu.VMEM((n_sets, tt * nb, uw), F32),
            pltpu.VMEM((nb, n_sets * sw), F32),
        ],
    )
    return pl.pallas_call(
        functools.partial(_ssm_kernel, n_ctx_tiles=nct), grid_spec=grid_spec,
        out_shape=jax.ShapeDtypeStruct((2, nb, s, d), F32),
        compiler_params=_cparams(("arbitrary", "arbitrary")), name="s5_scan",
    )(tile_of, x3, mods, g1, a_re, a_im, bblk, cblk)


def _ssm_params(lam_re, lam_im, log_dt, b_re, b_im, c_re, c_im):
    g, p = lam_re.shape[1:]
    h = b_re.shape[-1]
    ns = g // SSM_SET
    eye = jnp.eye(SSM_SET, dtype=F32)
    outs = []
    for dr in range(2):
        lr, li = lam_re[dr].astype(F32), lam_im[dr].astype(F32)
        br, bi = b_re[dr].astype(F32), b_im[dr].astype(F32)
        dt = jnp.exp(log_dt[dr].astype(F32))[:, None]
        zr, zi = lr * dt, li * dt
        mag = jnp.exp(zr)
        ar, ai = mag * jnp.cos(zi), mag * jnp.sin(zi)
        den = lr * lr + li * li
        cr = ((ar - 1.0) * lr + ai * li) / den
        ci = (ai * lr - (ar - 1.0) * li) / den
        bbr = cr[..., None] * br - ci[..., None] * bi
        bbi = cr[..., None] * bi + ci[..., None] * br

        def blk_b(w):
            w = jnp.transpose(w, (0, 2, 1)).reshape(ns, SSM_SET, h, p)
            return jnp.einsum('ab,jahp->jahbp', eye, w).reshape(ns, SSM_SET * h, SSM_SET * p)

        def blk_c(w):
            w = jnp.transpose(w.reshape(ns, SSM_SET, h, p), (0, 1, 3, 2))
            return jnp.einsum('ab,japh->japbh', eye, w).reshape(ns, SSM_SET * p, SSM_SET * h)

        bblk = jnp.concatenate([blk_b(bbr), blk_b(bbi)], axis=2)
        cblk = jnp.concatenate([blk_c(c_re[dr].astype(F32)), -blk_c(c_im[dr].astype(F32))], axis=1)
        outs.append((ar.reshape(ns, SSM_SET * p), ai.reshape(ns, SSM_SET * p), bblk, cblk))
    a_re = jnp.stack([o[0] for o in outs])
    a_im = jnp.stack([o[1] for o in outs])
    bblk = jnp.stack([o[2] for o in outs]).astype(BF16)
    cblk = jnp.stack([o[3] for o in outs]).astype(BF16)
    return a_re, a_im, bblk, cblk


def _glu_kernel(src, mrow, x_ref, yf_ref, yb_ref, mod_ref, g1_ref, dsk_ref, gw_ref, gb_ref,
                g2_ref, rwh_ref, rwl_ref, rb_ref, x1_ref, h2_ref, route_ref, cnt_ref):
    d = x_ref.shape[1]
    mod = mod_ref[0]
    x = x_ref[...]
    h = _rms(x, g1_ref[...]) * (1.0 + mod[1:2]) + mod[0:1]
    y = yf_ref[0] + yb_ref[0] + dsk_ref[...] * h
    gl = 0.5 * y * (1.0 + jnp.tanh(math.sqrt(2.0 / math.pi) * (y + 0.044715 * (y * y * y))))
    z = _dot(gl.astype(BF16), gw_ref[...]) + gb_ref[...]
    out = z[:, :d] * _sigmoid(z[:, d:])
    _post(x, out, mod, g2_ref[...], rwh_ref[...], rwl_ref[...], rb_ref[...],
          x1_ref, h2_ref, route_ref, cnt_ref)


def _glu_layer(xu, y2, tinfo, mods, g1, d_skip, glu_w, glu_b, g2, rwh, rwl, rb):
    rows, d = xu.shape
    n_steps = rows // TM
    n_e = rwh.shape[0]
    out_specs, out_shapes = _post_specs(n_steps, d, n_e)
    grid_spec = pltpu.PrefetchScalarGridSpec(
        num_scalar_prefetch=2,
        grid=(n_steps,),
        in_specs=[
            pl.BlockSpec((TM, d), lambda i, src, mrow: (i, 0)),
            pl.BlockSpec((1, TM, d), lambda i, src, mrow: (0, i, 0)),
            pl.BlockSpec((1, TM, d), lambda i, src, mrow: (1, i, 0)),
            pl.BlockSpec((1, 6, d), lambda i, src, mrow: (mrow[i], 0, 0)),
            _const_spec((1, d)),
            _const_spec((1, d)),
            _const_spec((d, 2 * d)),
            _const_spec((1, 2 * d)),
            _const_spec((1, d)),
            _const_spec((n_e, d)),
            _const_spec((n_e, d)),
            _const_spec((n_e, 1)),
        ],
        out_specs=out_specs,
    )
    return pl.pallas_call(
        _glu_kernel, grid_spec=grid_spec, out_shape=out_shapes,
        compiler_params=_cparams(("arbitrary",)), name="s5_glu",
    )(tinfo[0], tinfo[1], xu, y2, y2, mods, g1, d_skip, glu_w.astype(BF16), glu_b, g2, rwh, rwl, rb)


def _rope(v, cos, sin_signed, first_half):
    partner = jnp.where(first_half, pltpu.roll(v, HEAD_DIM - HEAD_DIM // 4, axis=1),
                        pltpu.roll(v, HEAD_DIM // 4, axis=1))
    return v * cos + partner * sin_signed


def _qkv_kernel(src, mrow, trow, x_ref, mod_ref, g1_ref, w_ref, qg_ref, kg_ref, cos_ref, sin_ref,
                q_ref, k_ref, v_ref):
    d = x_ref.shape[1]
    kvw = k_ref.shape[1]
    mod = mod_ref[0]
    h = _rms(x_ref[...], g1_ref[...]) * (1.0 + mod[1:2]) + mod[0:1]
    z = _dot(h.astype(BF16), w_ref[...])
    cos = cos_ref[...]
    sin = sin_ref[...]
    lane = lax.broadcasted_iota(jnp.int32, (TM, HEAD_DIM), 1)
    first_half = (lane % (HEAD_DIM // 2)) < (HEAD_DIM // 4)
    q_scale = HEAD_DIM ** -0.5
    for hd in range(d // HEAD_DIM):
        zh = z[:, hd * HEAD_DIM:(hd + 1) * HEAD_DIM]
        zh = _rope(_rms(zh, qg_ref[...]), cos, sin, first_half) * q_scale
        q_ref[:, hd * HEAD_DIM:(hd + 1) * HEAD_DIM] = zh.astype(BF16)
    for hd in range(kvw // HEAD_DIM):
        zh = z[:, d + hd * HEAD_DIM:d + (hd + 1) * HEAD_DIM]
        zh = _rope(_rms(zh, kg_ref[...]), cos, sin, first_half)
        k_ref[:, hd * HEAD_DIM:(hd + 1) * HEAD_DIM] = zh.astype(BF16)
    v_ref[...] = z[:, d + kvw:].astype(BF16)


def _attn_kernel(q_ref, k_ref, v_ref, o_ref, *, n_ctx):
    qt = pl.program_id(2)
    q = q_ref[...]

    def attend(k, v):
        s = _dot_nt(q, k)
        m = jnp.max(s, axis=-1, keepdims=True)
        p = jnp.exp(s - m)
        den = jnp.sum(p, axis=-1, keepdims=True)
        return _dot(p.astype(BF16), v) / den

    @pl.when(qt < n_ctx // TM)
    def _():
        o_ref[...] = attend(k_ref[0:n_ctx, :], v_ref[0:n_ctx, :]).astype(BF16)

    @pl.when(qt >= n_ctx // TM)
    def _():
        o_ref[...] = attend(k_ref[...], v_ref[...]).astype(BF16)


def _wo_kernel(src, mrow, x_ref, o_ref, mod_ref, wo_ref, g2_ref, rwh_ref, rwl_ref, rb_ref,
               x1_ref, h2_ref, route_ref, cnt_ref):
    y = _dot(o_ref[...], wo_ref[...])
    _post(x_ref[...], y, mod_ref[0], g2_ref[...], rwh_ref[...], rwl_ref[...], rb_ref[...],
          x1_ref, h2_ref, route_ref, cnt_ref)


def _rope_tables(n_ctx, n_lat):
    rows = n_lat // GRID_W
    row = jnp.repeat(jnp.arange(rows), GRID_W)
    col = jnp.tile(jnp.arange(GRID_W), rows)
    pos = jnp.stack([row, col], axis=-1).astype(F32)
    inv_freq = ROPE_THETA ** (-jnp.arange(ROPE_F, dtype=F32) / ROPE_F)
    ang = pos[:, :, None] * inv_freq
    cos, sin = jnp.cos(ang), jnp.sin(ang)
    cos_t = jnp.concatenate([cos, cos], axis=-1).reshape(n_lat, HEAD_DIM)
    sin_t = jnp.concatenate([-sin, sin], axis=-1).reshape(n_lat, HEAD_DIM)
    cos_t = jnp.concatenate([jnp.ones((n_ctx, HEAD_DIM), F32), cos_t], axis=0)
    sin_t = jnp.concatenate([jnp.zeros((n_ctx, HEAD_DIM), F32), sin_t], axis=0)
    return cos_t, sin_t


def _attn_layer(xu, tinfo, trow, mods, g1, wqkv, q_g, k_g, wo, g2, rwh, rwl, rb, nb, n_ctx):
    rows, d = xu.shape
    s = rows // nb
    n_steps = rows // TM
    n_e = rwh.shape[0]
    kvw = N_KV_HEADS * HEAD_DIM
    n_heads = d // HEAD_DIM
    rep = n_heads // N_KV_HEADS
    cos_t, sin_t = _rope_tables(n_ctx, s - n_ctx)
    grid_spec = pltpu.PrefetchScalarGridSpec(
        num_scalar_prefetch=3,
        grid=(n_steps,),
        in_specs=[
            pl.BlockSpec((TM, d), lambda i, src, mrow, tr: (i, 0)),
            pl.BlockSpec((1, 6, d), lambda i, src, mrow, tr: (mrow[i], 0, 0)),
            _const_spec((1, d)),
            _const_spec((d, d + 2 * kvw)),
            _const_spec((1, HEAD_DIM)),
            _const_spec((1, HEAD_DIM)),
            pl.BlockSpec((TM, HEAD_DIM), lambda i, src, mrow, tr: (tr[i], 0)),
            pl.BlockSpec((TM, HEAD_DIM), lambda i, src, mrow, tr: (tr[i], 0)),
        ],
        out_specs=[
            pl.BlockSpec((TM, d), lambda i, *_: (i, 0)),
            pl.BlockSpec((TM, kvw), lambda i, *_: (i, 0)),
            pl.BlockSpec((TM, kvw), lambda i, *_: (i, 0)),
        ],
    )
    q, k, v = pl.pallas_call(
        _qkv_kernel, grid_spec=grid_spec,
        out_shape=[jax.ShapeDtypeStruct((rows, d), BF16),
                   jax.ShapeDtypeStruct((rows, kvw), BF16),
                   jax.ShapeDtypeStruct((rows, kvw), BF16)],
        compiler_params=_cparams(("arbitrary",)), name="attn_qkv",
    )(tinfo[0], tinfo[1], trow, xu, mods, g1, wqkv.astype(BF16), q_g, k_g, cos_t, sin_t)

    tpb = s // TM
    o = pl.pallas_call(
        functools.partial(_attn_kernel, n_ctx=n_ctx),
        grid=(nb, n_heads, tpb),
        in_specs=[
            pl.BlockSpec((TM, HEAD_DIM), lambda b, hd, t: (b * tpb + t, hd)),
            pl.BlockSpec((s, HEAD_DIM), lambda b, hd, t: (b, hd // rep)),
            pl.BlockSpec((s, HEAD_DIM), lambda b, hd, t: (b, hd // rep)),
        ],
        out_specs=pl.BlockSpec((TM, HEAD_DIM), lambda b, hd, t: (b * tpb + t, hd)),
        out_shape=jax.ShapeDtypeStruct((rows, d), BF16),
        compiler_params=_cparams(("arbitrary", "arbitrary", "arbitrary")), name="attn_core",
    )(q, k, v)

    out_specs, out_shapes = _post_specs(n_steps, d, n_e)
    grid_spec = pltpu.PrefetchScalarGridSpec(
        num_scalar_prefetch=2,
        grid=(n_steps,),
        in_specs=[
            pl.BlockSpec((TM, d), lambda i, src, mrow: (i, 0)),
            pl.BlockSpec((TM, d), lambda i, src, mrow: (i, 0)),
            pl.BlockSpec((1, 6, d), lambda i, src, mrow: (mrow[i], 0, 0)),
            _const_spec((d, d)),
            _const_spec((1, d)),
            _const_spec((n_e, d)),
            _const_spec((n_e, d)),
            _const_spec((n_e, 1)),
        ],
        out_specs=out_specs,
    )
    return pl.pallas_call(
        _wo_kernel, grid_spec=grid_spec, out_shape=out_shapes,
        compiler_params=_cparams(("arbitrary",)), name="attn_out",
    )(tinfo[0], tinfo[1], xu, o, mods, wo.astype(BF16), g2, rwh, rwl, rb)


def _loc_rows(n_e):
    return -(-(TM * TOP_K + (SEG_ALIGN - 1) * n_e) // 256) * 256


def _pack_pairs(v, rounded):
    half = v.shape[1] // 2
    lo, hi = v[:, :half], v[:, half:]
    if not rounded:
        lo, hi = lo.astype(BF16).astype(F32), hi.astype(BF16).astype(F32)
    lo = lax.bitcast_convert_type(lo, jnp.uint32)
    hi = lax.bitcast_convert_type(hi, jnp.uint32)
    return (hi & jnp.uint32(0xFFFF0000)) | (lo >> 16)


def _unpack_pairs(p):
    lo = lax.bitcast_convert_type(p << 16, F32)
    hi = lax.bitcast_convert_type(p & jnp.uint32(0xFFFF0000), F32)
    return jnp.concatenate([lo, hi], axis=1).astype(BF16)


def _start_segments(t, n_e, loc, len8, dst, make_copy):
    def per_expert(e, carry):
        n = pl.multiple_of(len8[t * n_e + e], SEG_ALIGN)

        @pl.when(n > 0)
        def _():
            make_copy(pl.multiple_of(loc[t * n_e + e], SEG_ALIGN),
                      pl.multiple_of(dst[t * n_e + e], SEG_ALIGN), n).start()
        return carry

    lax.fori_loop(0, n_e, per_expert, 0)


def _dispatch_kernel(loc, len8, dst, tot, padoff, padlen, nused, h2_ref, route_ref, xs_ref,
                     buf, zbuf, sems, zsem, *, n_e, nb_max):
    i = pl.program_id(0)
    n_steps = pl.num_programs(0)
    slot = i % 2
    lrows = buf.shape[1]
    pos = route_ref[0, 0:TOP_K, :]
    iota_p = lax.broadcasted_iota(jnp.int32, (lrows, TM), 0).astype(F32)
    perm = jnp.zeros((lrows, TM), F32)
    for k in range(TOP_K):
        perm = perm + jnp.where(iota_p == pos[k:k + 1], 1.0, 0.0)
    buf[slot] = _pack_pairs(_dot(perm.astype(BF16), h2_ref[...]), rounded=True)

    def seg_copy(s):
        def make(a, g, n):
            return pltpu.make_async_copy(buf.at[s, pl.ds(a, n)], xs_ref.at[pl.ds(g, n)], sems.at[s])
        return make

    def wait_tile(t, s):
        seg_copy(s)(0, 0, pl.multiple_of(tot[t], SEG_ALIGN)).wait()

    _start_segments(i, n_e, loc, len8, dst, seg_copy(slot))

    @pl.when(i > 0)
    def _():
        wait_tile(i - 1, 1 - slot)

    @pl.when(i == n_steps - 1)
    def _():
        wait_tile(i, slot)

    @pl.when(i == n_steps - 1)
    def _():
        zbuf[...] = jnp.zeros_like(zbuf)

        def zero_rows(g, n):
            return pltpu.make_async_copy(zbuf.at[pl.ds(0, n)], xs_ref.at[pl.ds(g, n)], zsem)

        def pad(e, total):
            n = pl.multiple_of(padlen[e], SEG_ALIGN)

            @pl.when(n > 0)
            def _():
                zero_rows(pl.multiple_of(padoff[e], SEG_ALIGN), n).start()
            return total + n

        total = lax.fori_loop(0, n_e, pad, 0)

        def blank(b, carry):
            zero_rows(pl.multiple_of(b * BLK, BLK), BLK).start()
            return carry

        lax.fori_loop(nused[0], nb_max, blank, 0)
        total = pl.multiple_of(total + (nb_max - nused[0]) * BLK, SEG_ALIGN)

        @pl.when(total > 0)
        def _():
            pltpu.make_async_copy(xs_ref.at[pl.ds(0, total)], xs_ref.at[pl.ds(0, total)], zsem).wait()


def _expert_kernel(blk, blk_e, valid, nused, xs_ref, w1_ref, b1_ref, w2_ref, b2_ref, ys_ref, w1b, w2b):
    i = pl.program_id(0)
    f = w2b.shape[0]

    @pl.when(i >= nused[0])
    def _():
        ys_ref[...] = jnp.zeros_like(ys_ref)

    @pl.when(i < nused[0])
    def _():
        first = jnp.logical_or(i == 0, blk_e[i] != blk_e[jnp.maximum(i - 1, 0)])

        @pl.when(first)
        def _():
            w1b[...] = w1_ref[0, 0].astype(BF16)
            w2b[...] = w2_ref[0, 0].astype(BF16)

        row = lax.broadcasted_iota(jnp.int32, (BLK, 1), 0)
        x = _unpack_pairs(jnp.where(row < valid[i], xs_ref[...], jnp.uint32(0)))
        z = _dot(x, w1b[...]) + b1_ref[0, 0]
        glu = jnp.minimum(z[:, :f], SWIGLU_LIMIT)
        lin = jnp.clip(z[:, f:], -SWIGLU_LIMIT, SWIGLU_LIMIT)
        act = glu * _sigmoid(SWIGLU_ALPHA * glu) * (lin + 1.0)
        ys_ref[...] = _pack_pairs(_dot(act.astype(BF16), w2b[...]) + b2_ref[0, 0], rounded=False)


def _combine_kernel(loc, len8, dst, tot, mrow, ys_ref, route_ref, x1_ref, mod_ref, out_ref, buf, sems, *, n_e):
    i = pl.program_id(0)
    n_steps = pl.num_programs(0)
    slot = i % 2
    lrows = buf.shape[1]

    def seg_copy(s):
        def make(a, g, n):
            return pltpu.make_async_copy(ys_ref.at[pl.ds(g, n)], buf.at[s, pl.ds(a, n)], sems.at[s])
        return make

    @pl.when(i == 0)
    def _():
        _start_segments(0, n_e, loc, len8, dst, seg_copy(0))

    @pl.when(i + 1 < n_steps)
    def _():
        _start_segments(i + 1, n_e, loc, len8, dst, seg_copy(1 - slot))

    total = pl.multiple_of(tot[i], SEG_ALIGN)
    seg_copy(slot)(0, 0, total).wait()

    pos = route_ref[0, 0:TOP_K, :]
    gate = route_ref[0, TOP_K:2 * TOP_K, :]
    iota_p = lax.broadcasted_iota(jnp.int32, (lrows, TM), 0).astype(F32)
    gt = jnp.zeros((lrows, TM), F32)
    for k in range(TOP_K):
        gt = gt + jnp.where(iota_p == pos[k:k + 1], gate[k:k + 1], 0.0)
    row = lax.broadcasted_iota(jnp.int32, (lrows, 1), 0)
    ysl = _unpack_pairs(jnp.where(row < total, buf[slot], jnp.uint32(0)))
    f = _dot_tn(gt.astype(BF16), ysl)
    out_ref[...] = x1_ref[...] + mod_ref[0][5:6] * f


def _moe(x1, h2, route, cnt, mods, mrow, layer, w1, b1, w2, b2):
    rows, d = x1.shape
    n_tiles = rows // TM
    depth, n_e, _, f2 = w1.shape
    lrows = _loc_rows(n_e)
    nb_max = (rows * TOP_K + (SEG_ALIGN - 1) * n_tiles * n_e) // BLK + n_e
    cap = nb_max * BLK
    dp = d // 2

    i32 = lambda v: v.astype(jnp.int32)
    cnt = i32(cnt.reshape(n_tiles, n_e))
    c8 = (cnt + SEG_ALIGN - 1) // SEG_ALIGN * SEG_ALIGN
    loc = jnp.cumsum(c8, axis=1) - c8
    tot_tile = jnp.sum(c8, axis=1)
    tot = jnp.sum(c8, axis=0)
    nblk = (tot + BLK - 1) // BLK
    blk_end = jnp.cumsum(nblk)
    blk_start = blk_end - nblk
    dst = (blk_start * BLK)[None, :] + jnp.cumsum(c8, axis=0) - c8
    nused = blk_end[-1]
    bid = jnp.arange(nb_max, dtype=jnp.int32)
    bidc = jnp.minimum(bid, nused - 1)
    blk_e = jnp.minimum(jnp.sum(i32(bidc[:, None] >= blk_end[None, :]), axis=1), n_e - 1)
    valid = jnp.clip(tot[blk_e] - (bidc - blk_start[blk_e]) * BLK, 0, BLK)
    padoff = blk_start * BLK + tot
    padlen = nblk * BLK - tot
    loc, len8, dst = i32(loc.reshape(-1)), i32(c8.reshape(-1)), i32(dst.reshape(-1))
    nused = i32(nused.reshape(1))

    xs = pl.pallas_call(
        functools.partial(_dispatch_kernel, n_e=n_e, nb_max=nb_max),
        grid_spec=pltpu.PrefetchScalarGridSpec(
            num_scalar_prefetch=7,
            grid=(n_tiles,),
            in_specs=[
                pl.BlockSpec((TM, d), lambda i, *_: (i, 0)),
                pl.BlockSpec((1, 2 * TOP_K, TM), lambda i, *_: (i, 0, 0)),
            ],
            out_specs=pl.BlockSpec(memory_space=pl.ANY),
            scratch_shapes=[pltpu.VMEM((2, lrows, dp), jnp.uint32), pltpu.VMEM((BLK, dp), jnp.uint32),
                            pltpu.SemaphoreType.DMA((2,)), pltpu.SemaphoreType.DMA(())],
        ),
        out_shape=jax.ShapeDtypeStruct((cap, dp), jnp.uint32),
        compiler_params=_cparams(("arbitrary",)), name="moe_dispatch",
    )(loc, len8, dst, i32(tot_tile), i32(padoff), i32(padlen), nused, h2, route)

    ys = pl.pallas_call(
        _expert_kernel,
        grid_spec=pltpu.PrefetchScalarGridSpec(
            num_scalar_prefetch=4,
            grid=(nb_max,),
            in_specs=[
                pl.BlockSpec((BLK, dp), lambda i, blk, be, va, nu: (blk[i], 0)),
                pl.BlockSpec((1, 1, d, f2), lambda i, blk, be, va, nu: (layer, be[i], 0, 0)),
                pl.BlockSpec((1, 1, 1, f2), lambda i, blk, be, va, nu: (layer, be[i], 0, 0)),
                pl.BlockSpec((1, 1, f2 // 2, d), lambda i, blk, be, va, nu: (layer, be[i], 0, 0)),
                pl.BlockSpec((1, 1, 1, d), lambda i, blk, be, va, nu: (layer, be[i], 0, 0)),
            ],
            out_specs=pl.BlockSpec((BLK, dp), lambda i, blk, be, va, nu: (i, 0)),
            scratch_shapes=[pltpu.VMEM((d, f2), BF16), pltpu.VMEM((f2 // 2, d), BF16)],
        ),
        out_shape=jax.ShapeDtypeStruct((cap, dp), jnp.uint32),
        compiler_params=_cparams(("arbitrary",)), name="moe_experts",
    )(i32(bidc), i32(blk_e), i32(valid), nused, xs, w1, b1.reshape(depth, n_e, 1, f2), w2,
      b2.reshape(depth, n_e, 1, d))

    return pl.pallas_call(
        functools.partial(_combine_kernel, n_e=n_e),
        grid_spec=pltpu.PrefetchScalarGridSpec(
            num_scalar_prefetch=5,
            grid=(n_tiles,),
            in_specs=[
                pl.BlockSpec(memory_space=pl.ANY),
                pl.BlockSpec((1, 2 * TOP_K, TM), lambda i, *_: (i, 0, 0)),
                pl.BlockSpec((TM, d), lambda i, *_: (i, 0)),
                pl.BlockSpec((1, 6, d), lambda i, lo, le, ds, to, mr: (mr[i], 0, 0)),
            ],
            out_specs=pl.BlockSpec((TM, d), lambda i, *_: (i, 0)),
            scratch_shapes=[pltpu.VMEM((2, lrows, dp), jnp.uint32), pltpu.SemaphoreType.DMA((2,))],
        ),
        out_shape=jax.ShapeDtypeStruct((rows, d), F32),
        compiler_params=_cparams(("arbitrary",)), name="moe_combine",
    )(loc, len8, dst, i32(tot_tile), mrow, ys, route, x1, mods)


def _tile_info(nb, n_ctx, n_lat, latent_only):
    tpb = (n_ctx + n_lat) // TM
    ct = n_ctx // TM
    src, mrow, hp, hn, trow = [], [], [], [], []
    for b in range(nb):
        for j in range(ct if latent_only else 0, tpb):
            is_ctx = j < ct
            src.append(b * tpb + j)
            mrow.append(nb if is_ctx else b)
            hp.append(0 if j in (0, ct) else 1)
            hn.append(0 if j in (ct - 1, tpb - 1) else 1)
            trow.append(j)
    mk = lambda v: jnp.asarray(np.asarray(v, np.int32))
    return (mk(src), mk(mrow), mk(hp), mk(hn)), mk(trow)


def kernel(x, c, ctx, c_ctx, ada_w, ada_b, norm1_g, norm2_g, pool_w, pool_scale, ssm_lam_re, ssm_lam_im,
           ssm_log_dt, ssm_b_re, ssm_b_im, ssm_c_re, ssm_c_im, ssm_d, ssm_glu_w, ssm_glu_b, attn_wqkv,
           attn_q_g, attn_k_g, attn_wo, router_w, router_b, moe_w1, moe_b1, moe_w2, moe_b2):
    nb, n_lat, d = x.shape
    n_ctx = ctx.shape[1]
    depth = ada_w.shape[0]
    n_e = router_w.shape[-1]
    assert nb == 8 and n_ctx % TM == 0 and n_lat % TM == 0 and n_lat % GRID_W == 0
    s = n_ctx + n_lat

    c16 = jnp.concatenate([c, c_ctx[None, :], jnp.zeros((16 - nb - 1, d), F32)], axis=0)
    mods_all = _ada_mods(c16, ada_w, ada_b).reshape(depth, 16, 6, d)

    xu = jnp.concatenate([ctx, x], axis=1).reshape(nb * s, d)
    tinfo_u, trow_u = _tile_info(nb, n_ctx, n_lat, latent_only=False)
    tinfo_l, _ = _tile_info(nb, n_ctx, n_lat, latent_only=True)

    for i in range(depth):
        kind, j = i % 3, i // 3
        last = i == depth - 1
        mods = mods_all[i]
        g1 = norm1_g[i].reshape(1, d)
        g2 = norm2_g[i].reshape(1, d)
        rwt = router_w[i].T
        rwh = rwt.astype(BF16)
        rwl = (rwt - rwh.astype(F32)).astype(BF16)
        rb = router_b[i].reshape(n_e, 1)
        if kind == 0:
            tinfo = tinfo_l if last else tinfo_u
            x1, h2, route, cnt = _pool_layer(xu, tinfo, mods, g1, pool_w[j], pool_scale[j].reshape(1, d),
                                             g2, rwh, rwl, rb)
            mrow = tinfo[1]
        elif kind == 1:
            assert not last
            a_re, a_im, bblk, cblk = _ssm_params(ssm_lam_re[j], ssm_lam_im[j], ssm_log_dt[j], ssm_b_re[j],
                                                 ssm_b_im[j], ssm_c_re[j], ssm_c_im[j])
            y2 = _ssm_scan(xu.reshape(nb, s, d), mods, g1, a_re, a_im, bblk, cblk, n_ctx)
            x1, h2, route, cnt = _glu_layer(xu, y2.reshape(2, nb * s, d), tinfo_u, mods, g1,
                                            ssm_d[j].reshape(1, d), ssm_glu_w[j], ssm_glu_b[j].reshape(1, 2 * d),
                                            g2, rwh, rwl, rb)
            mrow = tinfo_u[1]
        else:
            assert not last
            x1, h2, route, cnt = _attn_layer(xu, tinfo_u, trow_u, mods, g1, attn_wqkv[j],
                                             attn_q_g[j].reshape(1, HEAD_DIM), attn_k_g[j].reshape(1, HEAD_DIM),
                                             attn_wo[j], g2, rwh, rwl, rb, nb, n_ctx)
            mrow = tinfo_u[1]
        xu = _moe(x1, h2, route, cnt, mods, mrow, i, moe_w1, moe_b1, moe_w2, moe_b2)
    if xu.shape[0] == nb * n_lat:
        return xu.reshape(nb, n_lat, d)
    return xu.reshape(nb, s, d)[:, n_ctx:, :]
```

```python
import functools
import math

import numpy as np
import jax
import jax.numpy as jnp
from jax import lax
from jax.experimental import pallas as pl
from jax.experimental.pallas import tpu as pltpu

F32 = jnp.float32
BF16 = jnp.bfloat16

GRID_W = 64
NORM_EPS = 1e-6
POOL_WINDOWS = (2, 4, 8, 16)
SSM_H = 16
SSM_P = 64
SSM_SET = 8
SSM_TT = 32
HEAD_DIM = 128
N_KV_HEADS = 2
ATTN_HEADS = 4
ROPE_F = HEAD_DIM // 4
ROPE_THETA = 10000.0
TOP_K = 4
SWIGLU_ALPHA = 1.702
SWIGLU_LIMIT = 7.0

TM = 256
BLK = 512
EXPERT_CHUNK = 512
SEG_ALIGN = 8
VMEM_LIMIT = 56 * 1024 * 1024


def _cparams(sem, vmem=VMEM_LIMIT):
    return pltpu.CompilerParams(dimension_semantics=sem, vmem_limit_bytes=vmem)


def _rms(x, g):
    return x * lax.rsqrt(jnp.mean(x * x, axis=-1, keepdims=True) + NORM_EPS) * g


def _sigmoid(x):
    return 1.0 / (1.0 + jnp.exp(-x))


def _dot(a, b):
    return jnp.dot(a, b, preferred_element_type=F32)


def _dot_nt(a, b):
    return lax.dot_general(a, b, (((1,), (1,)), ((), ())), preferred_element_type=F32)


def _dot_tn(a, b):
    return lax.dot_general(a, b, (((0,), (0,)), ((), ())), preferred_element_type=F32)


def _ada_kernel(c_ref, w_ref, b_ref, o_ref):
    c = c_ref[...]
    s = c * _sigmoid(c)
    o_ref[0] = jnp.dot(s, w_ref[0], preferred_element_type=F32,
                       precision=lax.Precision.HIGHEST) + b_ref[0]


def _ada_mods(c16, ada_w, ada_b):
    depth, d, six_d = ada_w.shape
    tn = d
    return pl.pallas_call(
        _ada_kernel,
        grid=(depth, six_d // tn),
        in_specs=[
            pl.BlockSpec((16, d), lambda l, j: (0, 0)),
            pl.BlockSpec((1, d, tn), lambda l, j: (l, 0, j)),
            pl.BlockSpec((1, 1, tn), lambda l, j: (l, 0, j)),
        ],
        out_specs=pl.BlockSpec((1, 16, tn), lambda l, j: (l, 0, j)),
        out_shape=jax.ShapeDtypeStruct((depth, 16, six_d), F32),
        compiler_params=_cparams(("arbitrary", "arbitrary")),
        name="ada_mods",
    )(c16, ada_w, ada_b.reshape(depth, 1, six_d))


def _post(x, y, mod, g2, rwh, rwl, rb, x1_ref, h2_ref, route_ref, cnt_ref):
    n_e = rwh.shape[0]
    x1 = x + mod[2:3] * y
    x1_ref[...] = x1
    h2 = _rms(x1, g2) * (1.0 + mod[4:5]) + mod[3:4]
    h2_ref[...] = h2.astype(BF16)
    hh = h2.astype(BF16)
    hl = (h2 - hh.astype(F32)).astype(BF16)
    logits = _dot_nt(rwh, hh) + _dot_nt(rwh, hl) + _dot_nt(rwl, hh) + rb
    iota_e = lax.broadcasted_iota(jnp.int32, (n_e, TM), 0)
    vals, onehots = [], []
    l = logits
    for _ in range(TOP_K):
        m = jnp.max(l, axis=0, keepdims=True)
        idx = jnp.min(jnp.where(l == m, iota_e, n_e), axis=0, keepdims=True)
        sel = iota_e == idx
        vals.append(m)
        onehots.append(sel)
        l = jnp.where(sel, -jnp.inf, l)
    ex = [jnp.exp(v - vals[0]) for v in vals]
    den = ex[0] + ex[1] + ex[2] + ex[3]
    gates = [e / den for e in ex]
    member = jnp.zeros((n_e, TM), F32)
    for sel in onehots:
        member = member + jnp.where(sel, 1.0, 0.0)
    r_i = lax.broadcasted_iota(jnp.int32, (TM, TM), 0)
    c_i = lax.broadcasted_iota(jnp.int32, (TM, TM), 1)
    upper = jnp.where(r_i < c_i, 1.0, 0.0).astype(BF16)
    cum = _dot(member.astype(BF16), upper)
    cnt = jnp.sum(member, axis=1, keepdims=True)
    cnt_ref[0] = cnt
    c8 = jnp.floor((cnt + (SEG_ALIGN - 1)) * (1.0 / SEG_ALIGN)) * SEG_ALIGN
    e_r = lax.broadcasted_iota(jnp.int32, (n_e, n_e), 0)
    e_c = lax.broadcasted_iota(jnp.int32, (n_e, n_e), 1)
    lower = jnp.where(e_c < e_r, 1.0, 0.0).astype(BF16)
    seg = _dot(lower, jnp.broadcast_to(c8, (n_e, TM)).astype(BF16))
    base = seg + cum
    rows = []
    for sel in onehots:
        rows.append(jnp.sum(jnp.where(sel, base, 0.0), axis=0, keepdims=True))
    route_ref[0] = jnp.concatenate(rows + gates, axis=0)


def _post_specs(n_steps, d, n_e):
    specs = [
        pl.BlockSpec((TM, d), lambda i, *_: (i, 0)),
        pl.BlockSpec((TM, d), lambda i, *_: (i, 0)),
        pl.BlockSpec((1, 2 * TOP_K, TM), lambda i, *_: (i, 0, 0)),
        pl.BlockSpec((1, n_e, 1), lambda i, *_: (i, 0, 0)),
    ]
    shapes = [
        jax.ShapeDtypeStruct((n_steps * TM, d), F32),
        jax.ShapeDtypeStruct((n_steps * TM, d), BF16),
        jax.ShapeDtypeStruct((n_steps, 2 * TOP_K, TM), F32),
        jax.ShapeDtypeStruct((n_steps, n_e, 1), F32),
    ]
    return specs, shapes


def _const_spec(shape):
    nd = len(shape)
    return pl.BlockSpec(shape, lambda i, *_: (0,) * nd)


def _pool_kernel(src, mrow, hp, hn, x_ref, xp_ref, xn_ref, mod_ref, g1_ref, pw_ref, ps_ref,
                 g2_ref, rwh_ref, rwl_ref, rb_ref, x1_ref, h2_ref, route_ref, cnt_ref, hh_scr):
    i = pl.program_id(0)
    d = x_ref.shape[1]
    gc = d // len(POOL_WINDOWS)
    halo = SEG_ALIGN
    mod = mod_ref[0]
    g1 = g1_ref[...]

    def pre(v):
        return _rms(v, g1) * (1.0 + mod[1:2]) + mod[0:1]

    has_prev = hp[i] > 0
    has_next = hn[i] > 0
    x = x_ref[...]
    h = pre(x)
    hh_scr[0:halo, :] = jnp.where(has_prev, pre(xp_ref[...]), 0.0)
    hh_scr[halo:halo + TM, :] = h
    hh_scr[halo + TM:2 * halo + TM, :] = jnp.where(has_next, pre(xn_ref[...]), 0.0)
    row = lax.broadcasted_iota(jnp.int32, (TM, 1), 0)
    ys = []
    for g, win in enumerate(POOL_WINDOWS):
        half = win // 2
        c0 = g * gc
        acc = hh_scr[pl.ds(halo - half, TM), c0:c0 + gc]
        for j in range(-half + 1, half):
            acc = acc + hh_scr[pl.ds(halo + j, TM), c0:c0 + gc]
        lo_clip = jnp.where(has_prev, 0, jnp.maximum(half - row, 0))
        hi_clip = jnp.where(has_next, 0, jnp.maximum(row + half - TM, 0))
        cnt = (win - lo_clip - hi_clip).astype(F32)
        diff = acc / cnt - h[:, c0:c0 + gc]
        ys.append(_dot(diff.astype(BF16), pw_ref[g]))
    y = jnp.concatenate(ys, axis=1) * ps_ref[...]
    _post(x, y, mod, g2_ref[...], rwh_ref[...], rwl_ref[...], rb_ref[...],
          x1_ref, h2_ref, route_ref, cnt_ref)


def _pool_layer(xu, tinfo, mods, g1, pool_w, pool_scale, g2, rwh, rwl, rb):
    rows, d = xu.shape
    n_steps = tinfo[0].shape[0]
    n_e = rwh.shape[0]
    gc = pool_w.shape[-1]
    rpb = TM // SEG_ALIGN
    out_specs, out_shapes = _post_specs(n_steps, d, n_e)
    grid_spec = pltpu.PrefetchScalarGridSpec(
        num_scalar_prefetch=4,
        grid=(n_steps,),
        in_specs=[
            pl.BlockSpec((TM, d), lambda i, src, mrow, hp, hn: (src[i], 0)),
            pl.BlockSpec((SEG_ALIGN, d), lambda i, src, mrow, hp, hn: (jnp.maximum(src[i] * rpb - 1, 0), 0)),
            pl.BlockSpec((SEG_ALIGN, d),
                         lambda i, src, mrow, hp, hn: (jnp.minimum((src[i] + 1) * rpb, rows // SEG_ALIGN - 1), 0)),
            pl.BlockSpec((1, 6, d), lambda i, src, mrow, hp, hn: (mrow[i], 0, 0)),
            _const_spec((1, d)),
            _const_spec((len(POOL_WINDOWS), gc, gc)),
            _const_spec((1, d)),
            _const_spec((1, d)),
            _const_spec((n_e, d)),
            _const_spec((n_e, d)),
            _const_spec((n_e, 1)),
        ],
        out_specs=out_specs,
        scratch_shapes=[pltpu.VMEM((TM + 2 * SEG_ALIGN, d), F32)],
    )
    return pl.pallas_call(
        _pool_kernel, grid_spec=grid_spec, out_shape=out_shapes,
        compiler_params=_cparams(("arbitrary",)), name="pool_mixer",
    )(*tinfo, xu, xu, xu, mods, g1, pool_w.astype(BF16), pool_scale, g2, rwh, rwl, rb)


def _ssm_kernel(tile_of, x_ref, mod_ref, g1_ref, are_ref, aim_ref, bb_ref, cb_ref, y_ref,
                u_scr, y_scr, h_scr, *xs_scrs, n_ctx_tiles):
    dr = pl.program_id(0)
    i = pl.program_id(1)
    nb, tt, d = x_ref.shape
    n_sets = bb_ref.shape[1]
    sw = bb_ref.shape[3]
    hw = sw // 2
    uw = bb_ref.shape[2]
    is_ctx = tile_of[dr * pl.num_programs(1) + i] < n_ctx_tiles

    @pl.when(i == 0)
    def _():
        h_scr[...] = jnp.zeros_like(h_scr)

    g1 = g1_ref[...]
    for b in range(nb):
        shift = jnp.where(is_ctx, mod_ref[nb, 0:1, :], mod_ref[b, 0:1, :])
        scale = jnp.where(is_ctx, mod_ref[nb, 1:2, :], mod_ref[b, 1:2, :])
        hb = _rms(x_ref[b], g1) * (1.0 + scale) + shift
        for j in range(n_sets):
            u_scr[j, pl.ds(b, tt, stride=nb), :] = hb[:, j * uw:(j + 1) * uw]
    for j in range(n_sets):
        xs_scrs[j][...] = _dot(u_scr[j].astype(BF16), bb_ref[0, j])
    for j in range(n_sets):
        xs = xs_scrs[j]
        ar = jnp.broadcast_to(are_ref[0, j:j + 1, :], (nb, hw))
        ai = jnp.broadcast_to(aim_ref[0, j:j + 1, :], (nb, hw))
        hr = h_scr[:, j * sw:j * sw + hw]
        hi = h_scr[:, j * sw + hw:(j + 1) * sw]
        for step in range(tt):
            t = step + dr * (tt - 1 - 2 * step)
            r0 = pl.multiple_of(t * nb, nb)
            nhr = ar * hr - ai * hi + xs[pl.ds(r0, nb), 0:hw]
            nhi = ar * hi + ai * hr + xs[pl.ds(r0, nb), hw:sw]
            xs[pl.ds(r0, nb), 0:hw] = nhr
            xs[pl.ds(r0, nb), hw:sw] = nhi
            hr, hi = nhr, nhi
        h_scr[:, j * sw:j * sw + hw] = hr
        h_scr[:, j * sw + hw:(j + 1) * sw] = hi
        y_scr[j] = _dot(xs[...].astype(BF16), cb_ref[0, j])
    for b in range(nb):
        for j in range(n_sets):
            y_ref[0, b, :, j * uw:(j + 1) * uw] = y_scr[j, pl.ds(b, tt, stride=nb), :]


def _ssm_scan(x3, mods, g1, a_re, a_im, bblk, cblk, n_ctx):
    nb, s, d = x3.shape
    tt = SSM_TT
    nt = s // tt
    nct = n_ctx // tt
    fwd = np.arange(nt)
    bwd = np.concatenate([np.arange(nct)[::-1], np.arange(nct, nt)[::-1]])
    tile_of = jnp.asarray(np.concatenate([fwd, bwd]), jnp.int32)
    n_sets, uw, sw = bblk.shape[1:]
    grid_spec = pltpu.PrefetchScalarGridSpec(
        num_scalar_prefetch=1,
        grid=(2, nt),
        in_specs=[
            pl.BlockSpec((nb, tt, d), lambda dr, i, to: (0, to[dr * nt + i], 0)),
            pl.BlockSpec(mods.shape, lambda dr, i, to: (0, 0, 0)),
            pl.BlockSpec((1, d), lambda dr, i, to: (0, 0)),
            pl.BlockSpec((1, n_sets, sw // 2), lambda dr, i, to: (dr, 0, 0)),
            pl.BlockSpec((1, n_sets, sw // 2), lambda dr, i, to: (dr, 0, 0)),
            pl.BlockSpec((1, n_sets, uw, sw), lambda dr, i, to: (dr, 0, 0, 0)),
            pl.BlockSpec((1, n_sets, sw, uw), lambda dr, i, to: (dr, 0, 0, 0)),
        ],
        out_specs=pl.BlockSpec((1, nb, tt, d), lambda dr, i, to: (dr, 0, to[dr * nt + i], 0)),
        scratch_shapes=[
            pltpu.VMEM((n_sets, tt * nb, uw), F32),
            pltpu.VMEM((n_sets, tt * nb, uw), F32),
            pltpu.VMEM((nb, n_sets * sw), F32),
        ] + [pltpu.VMEM((tt * nb, sw), F32) for _ in range(n_sets)],
    )
    return pl.pallas_call(
        functools.partial(_ssm_kernel, n_ctx_tiles=nct), grid_spec=grid_spec,
        out_shape=jax.ShapeDtypeStruct((2, nb, s, d), F32),
        compiler_params=_cparams(("arbitrary", "arbitrary")), name="s5_scan",
    )(tile_of, x3, mods, g1, a_re, a_im, bblk, cblk)


def _ssm_params(lam_re, lam_im, log_dt, b_re, b_im, c_re, c_im):
    g, p = lam_re.shape[1:]
    h = b_re.shape[-1]
    ns = g // SSM_SET
    eye = jnp.eye(SSM_SET, dtype=F32)
    outs = []
    for dr in range(2):
        lr, li = lam_re[dr].astype(F32), lam_im[dr].astype(F32)
        br, bi = b_re[dr].astype(F32), b_im[dr].astype(F32)
        dt = jnp.exp(log_dt[dr].astype(F32))[:, None]
        zr, zi = lr * dt, li * dt
        mag = jnp.exp(zr)
        ar, ai = mag * jnp.cos(zi), mag * jnp.sin(zi)
        den = lr * lr + li * li
        cr = ((ar - 1.0) * lr + ai * li) / den
        ci = (ai * lr - (ar - 1.0) * li) / den
        bbr = cr[..., None] * br - ci[..., None] * bi
        bbi = cr[..., None] * bi + ci[..., None] * br

        def blk_b(w):
            w = jnp.transpose(w, (0, 2, 1)).reshape(ns, SSM_SET, h, p)
            return jnp.einsum('ab,jahp->jahbp', eye, w).reshape(ns, SSM_SET * h, SSM_SET * p)

        def blk_c(w):
            w = jnp.transpose(w.reshape(ns, SSM_SET, h, p), (0, 1, 3, 2))
            return jnp.einsum('ab,japh->japbh', eye, w).reshape(ns, SSM_SET * p, SSM_SET * h)

        bblk = jnp.concatenate([blk_b(bbr), blk_b(bbi)], axis=2)
        cblk = jnp.concatenate([blk_c(c_re[dr].astype(F32)), -blk_c(c_im[dr].astype(F32))], axis=1)
        outs.append((ar.reshape(ns, SSM_SET * p), ai.reshape(ns, SSM_SET * p), bblk, cblk))
    a_re = jnp.stack([o[0] for o in outs])
    a_im = jnp.stack([o[1] for o in outs])
    bblk = jnp.stack([o[2] for o in outs]).astype(BF16)
    cblk = jnp.stack([o[3] for o in outs]).astype(BF16)
    return a_re, a_im, bblk, cblk


def _glu_kernel(src, mrow, x_ref, yf_ref, yb_ref, mod_ref, g1_ref, dsk_ref, gw_ref, gb_ref,
                g2_ref, rwh_ref, rwl_ref, rb_ref, x1_ref, h2_ref, route_ref, cnt_ref):
    d = x_ref.shape[1]
    mod = mod_ref[0]
    x = x_ref[...]
    h = _rms(x, g1_ref[...]) * (1.0 + mod[1:2]) + mod[0:1]
    y = yf_ref[0] + yb_ref[0] + dsk_ref[...] * h
    gl = 0.5 * y * (1.0 + jnp.tanh(math.sqrt(2.0 / math.pi) * (y + 0.044715 * (y * y * y))))
    z = _dot(gl.astype(BF16), gw_ref[...]) + gb_ref[...]
    out = z[:, :d] * _sigmoid(z[:, d:])
    _post(x, out, mod, g2_ref[...], rwh_ref[...], rwl_ref[...], rb_ref[...],
          x1_ref, h2_ref, route_ref, cnt_ref)


def _glu_layer(xu, y2, tinfo, mods, g1, d_skip, glu_w, glu_b, g2, rwh, rwl, rb):
    rows, d = xu.shape
    n_steps = rows // TM
    n_e = rwh.shape[0]
    out_specs, out_shapes = _post_specs(n_steps, d, n_e)
    grid_spec = pltpu.PrefetchScalarGridSpec(
        num_scalar_prefetch=2,
        grid=(n_steps,),
        in_specs=[
            pl.BlockSpec((TM, d), lambda i, src, mrow: (i, 0)),
            pl.BlockSpec((1, TM, d), lambda i, src, mrow: (0, i, 0)),
            pl.BlockSpec((1, TM, d), lambda i, src, mrow: (1, i, 0)),
            pl.BlockSpec((1, 6, d), lambda i, src, mrow: (mrow[i], 0, 0)),
            _const_spec((1, d)),
            _const_spec((1, d)),
            _const_spec((d, 2 * d)),
            _const_spec((1, 2 * d)),
            _const_spec((1, d)),
            _const_spec((n_e, d)),
            _const_spec((n_e, d)),
            _const_spec((n_e, 1)),
        ],
        out_specs=out_specs,
    )
    return pl.pallas_call(
        _glu_kernel, grid_spec=grid_spec, out_shape=out_shapes,
        compiler_params=_cparams(("arbitrary",)), name="s5_glu",
    )(tinfo[0], tinfo[1], xu, y2, y2, mods, g1, d_skip, glu_w.astype(BF16), glu_b, g2, rwh, rwl, rb)


def _rope(v, cos, sin_signed, first_half):
    partner = jnp.where(first_half, pltpu.roll(v, HEAD_DIM - HEAD_DIM // 4, axis=1),
                        pltpu.roll(v, HEAD_DIM // 4, axis=1))
    return v * cos + partner * sin_signed


def _qkv_kernel(src, mrow, trow, x_ref, mod_ref, g1_ref, w_ref, qg_ref, kg_ref, cos_ref, sin_ref,
                q_ref, k_ref, v_ref):
    d = x_ref.shape[1]
    kvw = k_ref.shape[1]
    mod = mod_ref[0]
    h = _rms(x_ref[...], g1_ref[...]) * (1.0 + mod[1:2]) + mod[0:1]
    z = _dot(h.astype(BF16), w_ref[...])
    cos = cos_ref[...]
    sin = sin_ref[...]
    lane = lax.broadcasted_iota(jnp.int32, (TM, HEAD_DIM), 1)
    first_half = (lane % (HEAD_DIM // 2)) < (HEAD_DIM // 4)
    q_scale = HEAD_DIM ** -0.5
    for hd in range(d // HEAD_DIM):
        zh = z[:, hd * HEAD_DIM:(hd + 1) * HEAD_DIM]
        zh = _rope(_rms(zh, qg_ref[...]), cos, sin, first_half) * q_scale
        q_ref[:, hd * HEAD_DIM:(hd + 1) * HEAD_DIM] = zh.astype(BF16)
    for hd in range(kvw // HEAD_DIM):
        zh = z[:, d + hd * HEAD_DIM:d + (hd + 1) * HEAD_DIM]
        zh = _rope(_rms(zh, kg_ref[...]), cos, sin, first_half)
        k_ref[:, hd * HEAD_DIM:(hd + 1) * HEAD_DIM] = zh.astype(BF16)
    v_ref[...] = z[:, d + kvw:].astype(BF16)


def _attn_kernel(q_ref, k_ref, v_ref, o_ref, *, n_ctx):
    qt = pl.program_id(2)

    def attend(n_keys):
        for hd in range(ATTN_HEADS):
            cols = slice(hd * HEAD_DIM, (hd + 1) * HEAD_DIM)
            s = _dot_nt(q_ref[:, cols], k_ref[0:n_keys, :])
            m = jnp.max(s, axis=-1, keepdims=True)
            p = jnp.exp(s - m)
            den = jnp.sum(p, axis=-1, keepdims=True)
            o_ref[:, cols] = (_dot(p.astype(BF16), v_ref[0:n_keys, :]) / den).astype(BF16)

    @pl.when(qt < n_ctx // TM)
    def _():
        attend(n_ctx)

    @pl.when(qt >= n_ctx // TM)
    def _():
        attend(k_ref.shape[0])


def _wo_kernel(src, mrow, x_ref, o_ref, mod_ref, wo_ref, g2_ref, rwh_ref, rwl_ref, rb_ref,
               x1_ref, h2_ref, route_ref, cnt_ref):
    y = _dot(o_ref[...], wo_ref[...])
    _post(x_ref[...], y, mod_ref[0], g2_ref[...], rwh_ref[...], rwl_ref[...], rb_ref[...],
          x1_ref, h2_ref, route_ref, cnt_ref)


def _rope_tables(n_ctx, n_lat):
    rows = n_lat // GRID_W
    row = jnp.repeat(jnp.arange(rows), GRID_W)
    col = jnp.tile(jnp.arange(GRID_W), rows)
    pos = jnp.stack([row, col], axis=-1).astype(F32)
    inv_freq = ROPE_THETA ** (-jnp.arange(ROPE_F, dtype=F32) / ROPE_F)
    ang = pos[:, :, None] * inv_freq
    cos, sin = jnp.cos(ang), jnp.sin(ang)
    cos_t = jnp.concatenate([cos, cos], axis=-1).reshape(n_lat, HEAD_DIM)
    sin_t = jnp.concatenate([-sin, sin], axis=-1).reshape(n_lat, HEAD_DIM)
    cos_t = jnp.concatenate([jnp.ones((n_ctx, HEAD_DIM), F32), cos_t], axis=0)
    sin_t = jnp.concatenate([jnp.zeros((n_ctx, HEAD_DIM), F32), sin_t], axis=0)
    return cos_t, sin_t


def _attn_layer(xu, tinfo, trow, mods, g1, wqkv, q_g, k_g, wo, g2, rwh, rwl, rb, nb, n_ctx):
    rows, d = xu.shape
    s = rows // nb
    n_steps = rows // TM
    n_e = rwh.shape[0]
    kvw = N_KV_HEADS * HEAD_DIM
    n_heads = d // HEAD_DIM
    rep = n_heads // N_KV_HEADS
    cos_t, sin_t = _rope_tables(n_ctx, s - n_ctx)
    grid_spec = pltpu.PrefetchScalarGridSpec(
        num_scalar_prefetch=3,
        grid=(n_steps,),
        in_specs=[
            pl.BlockSpec((TM, d), lambda i, src, mrow, tr: (i, 0)),
            pl.BlockSpec((1, 6, d), lambda i, src, mrow, tr: (mrow[i], 0, 0)),
            _const_spec((1, d)),
            _const_spec((d, d + 2 * kvw)),
            _const_spec((1, HEAD_DIM)),
            _const_spec((1, HEAD_DIM)),
            pl.BlockSpec((TM, HEAD_DIM), lambda i, src, mrow, tr: (tr[i], 0)),
            pl.BlockSpec((TM, HEAD_DIM), lambda i, src, mrow, tr: (tr[i], 0)),
        ],
        out_specs=[
            pl.BlockSpec((TM, d), lambda i, *_: (i, 0)),
            pl.BlockSpec((TM, kvw), lambda i, *_: (i, 0)),
            pl.BlockSpec((TM, kvw), lambda i, *_: (i, 0)),
        ],
    )
    q, k, v = pl.pallas_call(
        _qkv_kernel, grid_spec=grid_spec,
        out_shape=[jax.ShapeDtypeStruct((rows, d), BF16),
                   jax.ShapeDtypeStruct((rows, kvw), BF16),
                   jax.ShapeDtypeStruct((rows, kvw), BF16)],
        compiler_params=_cparams(("arbitrary",)), name="attn_qkv",
    )(tinfo[0], tinfo[1], trow, xu, mods, g1, wqkv.astype(BF16), q_g, k_g, cos_t, sin_t)

    tpb = s // TM
    o = pl.pallas_call(
        functools.partial(_attn_kernel, n_ctx=n_ctx),
        grid=(nb, n_heads // ATTN_HEADS, tpb),
        in_specs=[
            pl.BlockSpec((TM, ATTN_HEADS * HEAD_DIM), lambda b, hp, t: (b * tpb + t, hp)),
            pl.BlockSpec((s, HEAD_DIM), lambda b, hp, t: (b, hp * ATTN_HEADS // rep)),
            pl.BlockSpec((s, HEAD_DIM), lambda b, hp, t: (b, hp * ATTN_HEADS // rep)),
        ],
        out_specs=pl.BlockSpec((TM, ATTN_HEADS * HEAD_DIM), lambda b, hp, t: (b * tpb + t, hp)),
        out_shape=jax.ShapeDtypeStruct((rows, d), BF16),
        compiler_params=_cparams(("arbitrary", "arbitrary", "arbitrary")), name="attn_core",
    )(q, k, v)

    out_specs, out_shapes = _post_specs(n_steps, d, n_e)
    grid_spec = pltpu.PrefetchScalarGridSpec(
        num_scalar_prefetch=2,
        grid=(n_steps,),
        in_specs=[
            pl.BlockSpec((TM, d), lambda i, src, mrow: (i, 0)),
            pl.BlockSpec((TM, d), lambda i, src, mrow: (i, 0)),
            pl.BlockSpec((1, 6, d), lambda i, src, mrow: (mrow[i], 0, 0)),
            _const_spec((d, d)),
            _const_spec((1, d)),
            _const_spec((n_e, d)),
            _const_spec((n_e, d)),
            _const_spec((n_e, 1)),
        ],
        out_specs=out_specs,
    )
    return pl.pallas_call(
        _wo_kernel, grid_spec=grid_spec, out_shape=out_shapes,
        compiler_params=_cparams(("arbitrary",)), name="attn_out",
    )(tinfo[0], tinfo[1], xu, o, mods, wo.astype(BF16), g2, rwh, rwl, rb)


def _loc_rows(n_e):
    return -(-(TM * TOP_K + (SEG_ALIGN - 1) * n_e) // 256) * 256


def _pack_pairs(v, rounded):
    half = v.shape[1] // 2
    lo, hi = v[:, :half], v[:, half:]
    if not rounded:
        lo, hi = lo.astype(BF16).astype(F32), hi.astype(BF16).astype(F32)
    lo = lax.bitcast_convert_type(lo, jnp.uint32)
    hi = lax.bitcast_convert_type(hi, jnp.uint32)
    return (hi & jnp.uint32(0xFFFF0000)) | (lo >> 16)


def _unpack_pairs(p):
    lo = lax.bitcast_convert_type(p << 16, F32)
    hi = lax.bitcast_convert_type(p & jnp.uint32(0xFFFF0000), F32)
    return jnp.concatenate([lo, hi], axis=1).astype(BF16)


def _start_segments(t, n_e, loc, len8, dst, make_copy):
    def per_expert(e, carry):
        n = pl.multiple_of(len8[t * n_e + e], SEG_ALIGN)

        @pl.when(n > 0)
        def _():
            make_copy(pl.multiple_of(loc[t * n_e + e], SEG_ALIGN),
                      pl.multiple_of(dst[t * n_e + e], SEG_ALIGN), n).start()
        return carry

    lax.fori_loop(0, n_e, per_expert, 0)


def _dispatch_kernel(loc, len8, dst, tot, padoff, padlen, nused, h2_ref, route_ref, xs_ref,
                     buf, zbuf, sems, zsem, *, n_e, nb_max):
    i = pl.program_id(0)
    n_steps = pl.num_programs(0)
    slot = i % 2
    lrows = buf.shape[1]
    pos = route_ref[0, 0:TOP_K, :].astype(jnp.int32).astype(jnp.int16)
    iota_p = lax.broadcasted_iota(jnp.int16, (lrows, TM), 0)
    perm = jnp.zeros((lrows, TM), BF16)
    for k in range(TOP_K):
        perm = jnp.where(iota_p == pos[k:k + 1], jnp.ones((), BF16), perm)
    buf[slot] = _pack_pairs(_dot(perm, h2_ref[...]), rounded=True)

    def seg_copy(s):
        def make(a, g, n):
            return pltpu.make_async_copy(buf.at[s, pl.ds(a, n)], xs_ref.at[pl.ds(g, n)], sems.at[s])
        return make

    def wait_tile(t, s):
        seg_copy(s)(0, 0, pl.multiple_of(tot[t], SEG_ALIGN)).wait()

    _start_segments(i, n_e, loc, len8, dst, seg_copy(slot))

    @pl.when(i > 0)
    def _():
        wait_tile(i - 1, 1 - slot)

    @pl.when(i == n_steps - 1)
    def _():
        wait_tile(i, slot)

    @pl.when(i == n_steps - 1)
    def _():
        zbuf[...] = jnp.zeros_like(zbuf)

        def zero_rows(g, n):
            return pltpu.make_async_copy(zbuf.at[pl.ds(0, n)], xs_ref.at[pl.ds(g, n)], zsem)

        def pad(e, total):
            n = pl.multiple_of(padlen[e], SEG_ALIGN)

            @pl.when(n > 0)
            def _():
                zero_rows(pl.multiple_of(padoff[e], SEG_ALIGN), n).start()
            return total + n

        total = lax.fori_loop(0, n_e, pad, 0)

        def blank(b, carry):
            zero_rows(pl.multiple_of(b * BLK, BLK), BLK).start()
            return carry

        lax.fori_loop(nused[0], nb_max, blank, 0)
        total = pl.multiple_of(total + (nb_max - nused[0]) * BLK, SEG_ALIGN)

        @pl.when(total > 0)
        def _():
            pltpu.make_async_copy(xs_ref.at[pl.ds(0, total)], xs_ref.at[pl.ds(0, total)], zsem).wait()


def _expert_kernel(blk, blk_e, valid, nused, xs_ref, w1_ref, b1_ref, w2_ref, b2_ref, ys_ref, w1b, w2b):
    i = pl.program_id(0)
    f = w2b.shape[0]

    @pl.when(i >= nused[0])
    def _():
        ys_ref[...] = jnp.zeros_like(ys_ref)

    @pl.when(i < nused[0])
    def _():
        first = jnp.logical_or(i == 0, blk_e[i] != blk_e[jnp.maximum(i - 1, 0)])

        @pl.when(first)
        def _():
            w1b[...] = w1_ref[0, 0].astype(BF16)
            w2b[...] = w2_ref[0, 0].astype(BF16)

        for c in range(BLK // EXPERT_CHUNK):
            r0 = c * EXPERT_CHUNK
            row = r0 + lax.broadcasted_iota(jnp.int32, (EXPERT_CHUNK, 1), 0)
            x = _unpack_pairs(jnp.where(row < valid[i], xs_ref[r0:r0 + EXPERT_CHUNK, :], jnp.uint32(0)))
            z = _dot(x, w1b[...]) + b1_ref[0, 0]
            glu = jnp.minimum(z[:, :f], SWIGLU_LIMIT)
            lin = jnp.clip(z[:, f:], -SWIGLU_LIMIT, SWIGLU_LIMIT)
            act = glu * _sigmoid(SWIGLU_ALPHA * glu) * (lin + 1.0)
            y = _dot(act.astype(BF16), w2b[...]) + b2_ref[0, 0]
            ys_ref[r0:r0 + EXPERT_CHUNK, :] = _pack_pairs(y, rounded=False)


def _combine_kernel(loc, len8, dst, tot, mrow, ys_ref, route_ref, x1_ref, mod_ref, out_ref, buf, sems, *, n_e):
    i = pl.program_id(0)
    n_steps = pl.num_programs(0)
    slot = i % 2
    lrows = buf.shape[1]

    def seg_copy(s):
        def make(a, g, n):
            return pltpu.make_async_copy(ys_ref.at[pl.ds(g, n)], buf.at[s, pl.ds(a, n)], sems.at[s])
        return make

    @pl.when(i == 0)
    def _():
        _start_segments(0, n_e, loc, len8, dst, seg_copy(0))

    @pl.when(i + 1 < n_steps)
    def _():
        _start_segments(i + 1, n_e, loc, len8, dst, seg_copy(1 - slot))

    total = pl.multiple_of(tot[i], SEG_ALIGN)
    seg_copy(slot)(0, 0, total).wait()

    pos = route_ref[0, 0:TOP_K, :].astype(jnp.int32).astype(jnp.int16)
    gate = route_ref[0, TOP_K:2 * TOP_K, :].astype(BF16)
    iota_p = lax.broadcasted_iota(jnp.int16, (lrows, TM), 0)
    gt = jnp.zeros((lrows, TM), BF16)
    for k in range(TOP_K):
        gt = jnp.where(iota_p == pos[k:k + 1], gate[k:k + 1], gt)
    row = lax.broadcasted_iota(jnp.int32, (lrows, 1), 0)
    ysl = _unpack_pairs(jnp.where(row < total, buf[slot], jnp.uint32(0)))
    f = _dot_tn(gt, ysl)
    out_ref[...] = x1_ref[...] + mod_ref[0][5:6] * f


def _moe(x1, h2, route, cnt, mods, mrow, layer, w1, b1, w2, b2):
    rows, d = x1.shape
    n_tiles = rows // TM
    depth, n_e, _, f2 = w1.shape
    lrows = _loc_rows(n_e)
    nb_max = (rows * TOP_K + (SEG_ALIGN - 1) * n_tiles * n_e) // BLK + n_e
    cap = nb_max * BLK
    dp = d // 2

    i32 = lambda v: v.astype(jnp.int32)
    cnt = i32(cnt.reshape(n_tiles, n_e))
    c8 = (cnt + SEG_ALIGN - 1) // SEG_ALIGN * SEG_ALIGN
    loc = jnp.cumsum(c8, axis=1) - c8
    tot_tile = jnp.sum(c8, axis=1)
    tot = jnp.sum(c8, axis=0)
    nblk = (tot + BLK - 1) // BLK
    blk_end = jnp.cumsum(nblk)
    blk_start = blk_end - nblk
    dst = (blk_start * BLK)[None, :] + jnp.cumsum(c8, axis=0) - c8
    nused = blk_end[-1]
    bid = jnp.arange(nb_max, dtype=jnp.int32)
    bidc = jnp.minimum(bid, nused - 1)
    blk_e = jnp.minimum(jnp.sum(i32(bidc[:, None] >= blk_end[None, :]), axis=1), n_e - 1)
    valid = jnp.clip(tot[blk_e] - (bidc - blk_start[blk_e]) * BLK, 0, BLK)
    padoff = blk_start * BLK + tot
    padlen = nblk * BLK - tot
    loc, len8, dst = i32(loc.reshape(-1)), i32(c8.reshape(-1)), i32(dst.reshape(-1))
    nused = i32(nused.reshape(1))

    xs = pl.pallas_call(
        functools.partial(_dispatch_kernel, n_e=n_e, nb_max=nb_max),
        grid_spec=pltpu.PrefetchScalarGridSpec(
            num_scalar_prefetch=7,
            grid=(n_tiles,),
            in_specs=[
                pl.BlockSpec((TM, d), lambda i, *_: (i, 0)),
                pl.BlockSpec((1, 2 * TOP_K, TM), lambda i, *_: (i, 0, 0)),
            ],
            out_specs=pl.BlockSpec(memory_space=pl.ANY),
            scratch_shapes=[pltpu.VMEM((2, lrows, dp), jnp.uint32), pltpu.VMEM((BLK, dp), jnp.uint32),
                            pltpu.SemaphoreType.DMA((2,)), pltpu.SemaphoreType.DMA(())],
        ),
        out_shape=jax.ShapeDtypeStruct((cap, dp), jnp.uint32),
        compiler_params=_cparams(("arbitrary",)), name="moe_dispatch",
    )(loc, len8, dst, i32(tot_tile), i32(padoff), i32(padlen), nused, h2, route)

    ys = pl.pallas_call(
        _expert_kernel,
        grid_spec=pltpu.PrefetchScalarGridSpec(
            num_scalar_prefetch=4,
            grid=(nb_max,),
            in_specs=[
                pl.BlockSpec((BLK, dp), lambda i, blk, be, va, nu: (blk[i], 0)),
                pl.BlockSpec((1, 1, d, f2), lambda i, blk, be, va, nu: (layer, be[i], 0, 0)),
                pl.BlockSpec((1, 1, 1, f2), lambda i, blk, be, va, nu: (layer, be[i], 0, 0)),
                pl.BlockSpec((1, 1, f2 // 2, d), lambda i, blk, be, va, nu: (layer, be[i], 0, 0)),
                pl.BlockSpec((1, 1, 1, d), lambda i, blk, be, va, nu: (layer, be[i], 0, 0)),
            ],
            out_specs=pl.BlockSpec((BLK, dp), lambda i, blk, be, va, nu: (i, 0)),
            scratch_shapes=[pltpu.VMEM((d, f2), BF16), pltpu.VMEM((f2 // 2, d), BF16)],
        ),
        out_shape=jax.ShapeDtypeStruct((cap, dp), jnp.uint32),
        compiler_params=_cparams(("arbitrary",)), name="moe_experts",
    )(i32(bidc), i32(blk_e), i32(valid), nused, xs, w1, b1.reshape(depth, n_e, 1, f2), w2,
      b2.reshape(depth, n_e, 1, d))

    return pl.pallas_call(
        functools.partial(_combine_kernel, n_e=n_e),
        grid_spec=pltpu.PrefetchScalarGridSpec(
            num_scalar_prefetch=5,
            grid=(n_tiles,),
            in_specs=[
                pl.BlockSpec(memory_space=pl.ANY),
                pl.BlockSpec((1, 2 * TOP_K, TM), lambda i, *_: (i, 0, 0)),
                pl.BlockSpec((TM, d), lambda i, *_: (i, 0)),
                pl.BlockSpec((1, 6, d), lambda i, lo, le, ds, to, mr: (mr[i], 0, 0)),
            ],
            out_specs=pl.BlockSpec((TM, d), lambda i, *_: (i, 0)),
            scratch_shapes=[pltpu.VMEM((2, lrows, dp), jnp.uint32), pltpu.SemaphoreType.DMA((2,))],
        ),
        out_shape=jax.ShapeDtypeStruct((rows, d), F32),
        compiler_params=_cparams(("arbitrary",)), name="moe_combine",
    )(loc, len8, dst, i32(tot_tile), mrow, ys, route, x1, mods)


def _tile_info(nb, n_ctx, n_lat, latent_only):
    tpb = (n_ctx + n_lat) // TM
    ct = n_ctx // TM
    src, mrow, hp, hn, trow = [], [], [], [], []
    for b in range(nb):
        for j in range(ct if latent_only else 0, tpb):
            is_ctx = j < ct
            src.append(b * tpb + j)
            mrow.append(nb if is_ctx else b)
            hp.append(0 if j in (0, ct) else 1)
            hn.append(0 if j in (ct - 1, tpb - 1) else 1)
            trow.append(j)
    mk = lambda v: jnp.asarray(np.asarray(v, np.int32))
    return (mk(src), mk(mrow), mk(hp), mk(hn)), mk(trow)


def kernel(x, c, ctx, c_ctx, ada_w, ada_b, norm1_g, norm2_g, pool_w, pool_scale, ssm_lam_re, ssm_lam_im,
           ssm_log_dt, ssm_b_re, ssm_b_im, ssm_c_re, ssm_c_im, ssm_d, ssm_glu_w, ssm_glu_b, attn_wqkv,
           attn_q_g, attn_k_g, attn_wo, router_w, router_b, moe_w1, moe_b1, moe_w2, moe_b2):
    nb, n_lat, d = x.shape
    n_ctx = ctx.shape[1]
    depth = ada_w.shape[0]
    n_e = router_w.shape[-1]
    assert nb == 8 and n_ctx % TM == 0 and n_lat % TM == 0 and n_lat % GRID_W == 0
    s = n_ctx + n_lat

    c16 = jnp.concatenate([c, c_ctx[None, :], jnp.zeros((16 - nb - 1, d), F32)], axis=0)
    mods_all = _ada_mods(c16, ada_w, ada_b).reshape(depth, 16, 6, d)

    xu = jnp.concatenate([ctx, x], axis=1).reshape(nb * s, d)
    tinfo_u, trow_u = _tile_info(nb, n_ctx, n_lat, latent_only=False)
    tinfo_l, _ = _tile_info(nb, n_ctx, n_lat, latent_only=True)

    for i in range(depth):
        kind, j = i % 3, i // 3
        last = i == depth - 1
        mods = mods_all[i]
        g1 = norm1_g[i].reshape(1, d)
        g2 = norm2_g[i].reshape(1, d)
        rwt = router_w[i].T
        rwh = rwt.astype(BF16)
        rwl = (rwt - rwh.astype(F32)).astype(BF16)
        rb = router_b[i].reshape(n_e, 1)
        if kind == 0:
            tinfo = tinfo_l if last else tinfo_u
            x1, h2, route, cnt = _pool_layer(xu, tinfo, mods, g1, pool_w[j], pool_scale[j].reshape(1, d),
                                             g2, rwh, rwl, rb)
            mrow = tinfo[1]
        elif kind == 1:
            assert not last
            a_re, a_im, bblk, cblk = _ssm_params(ssm_lam_re[j], ssm_lam_im[j], ssm_log_dt[j], ssm_b_re[j],
                                                 ssm_b_im[j], ssm_c_re[j], ssm_c_im[j])
            y2 = _ssm_scan(xu.reshape(nb, s, d), mods, g1, a_re, a_im, bblk, cblk, n_ctx)
            x1, h2, route, cnt = _glu_layer(xu, y2.reshape(2, nb * s, d), tinfo_u, mods, g1,
                                            ssm_d[j].reshape(1, d), ssm_glu_w[j], ssm_glu_b[j].reshape(1, 2 * d),
                                            g2, rwh, rwl, rb)
            mrow = tinfo_u[1]
        else:
            assert not last
            x1, h2, route, cnt = _attn_layer(xu, tinfo_u, trow_u, mods, g1, attn_wqkv[j],
                                             attn_q_g[j].reshape(1, HEAD_DIM), attn_k_g[j].reshape(1, HEAD_DIM),
                                             attn_wo[j], g2, rwh, rwl, rb, nb, n_ctx)
            mrow = tinfo_u[1]
        xu = _moe(x1, h2, route, cnt, mods, mrow, i, moe_w1, moe_b1, moe_w2, moe_b2)
    if xu.shape[0] == nb * n_lat:
        return xu.reshape(nb, n_lat, d)
    return xu.reshape(nb, s, d)[:, n_ctx:, :]
```

```python
import functools
import math

import numpy as np
import jax
import jax.numpy as jnp
from jax import lax
from jax.experimental import pallas as pl
from jax.experimental.pallas import tpu as pltpu

F32 = jnp.float32
BF16 = jnp.bfloat16

GRID_W = 64
NORM_EPS = 1e-6
POOL_WINDOWS = (2, 4, 8, 16)
SSM_H = 16
SSM_P = 64
SSM_SET = 8
SSM_TT = 32
HEAD_DIM = 128
N_KV_HEADS = 2
ATTN_HEADS = 4
ROPE_F = HEAD_DIM // 4
ROPE_THETA = 10000.0
TOP_K = 4
SWIGLU_ALPHA = 1.702
SWIGLU_LIMIT = 7.0

TM = 256
SUB_TILES = 2
BLK = 512
EXPERT_CHUNK = 512
SEG_ALIGN = 8
VMEM_LIMIT = 56 * 1024 * 1024


def _cparams(sem, vmem=VMEM_LIMIT):
    return pltpu.CompilerParams(dimension_semantics=sem, vmem_limit_bytes=vmem)


def _rms(x, g):
    return x * lax.rsqrt(jnp.mean(x * x, axis=-1, keepdims=True) + NORM_EPS) * g


def _sigmoid(x):
    return 1.0 / (1.0 + jnp.exp(-x))


def _dot(a, b):
    return jnp.dot(a, b, preferred_element_type=F32)


def _dot_nt(a, b):
    return lax.dot_general(a, b, (((1,), (1,)), ((), ())), preferred_element_type=F32)


def _dot_tn(a, b):
    return lax.dot_general(a, b, (((0,), (0,)), ((), ())), preferred_element_type=F32)


def _ada_kernel(c_ref, w_ref, b_ref, o_ref):
    c = c_ref[...]
    s = c * _sigmoid(c)
    o_ref[0] = jnp.dot(s, w_ref[0], preferred_element_type=F32,
                       precision=lax.Precision.HIGHEST) + b_ref[0]


def _ada_mods(c16, ada_w, ada_b):
    depth, d, six_d = ada_w.shape
    tn = d
    return pl.pallas_call(
        _ada_kernel,
        grid=(depth, six_d // tn),
        in_specs=[
            pl.BlockSpec((16, d), lambda l, j: (0, 0)),
            pl.BlockSpec((1, d, tn), lambda l, j: (l, 0, j)),
            pl.BlockSpec((1, 1, tn), lambda l, j: (l, 0, j)),
        ],
        out_specs=pl.BlockSpec((1, 16, tn), lambda l, j: (l, 0, j)),
        out_shape=jax.ShapeDtypeStruct((depth, 16, six_d), F32),
        compiler_params=_cparams(("arbitrary", "arbitrary")),
        name="ada_mods",
    )(c16, ada_w, ada_b.reshape(depth, 1, six_d))


def _post_logits(x, y, mod, g2, rwh, rwl, rb, x1_ref, h2_ref):
    x1 = x + mod[2:3] * y
    x1_ref[...] = x1
    h2 = _rms(x1, g2) * (1.0 + mod[4:5]) + mod[3:4]
    h2_ref[...] = h2.astype(BF16)
    hh = h2.astype(BF16)
    hl = (h2 - hh.astype(F32)).astype(BF16)
    return _dot_nt(rwh, hh) + _dot_nt(rwh, hl) + _dot_nt(rwl, hh) + rb


def _post_route(logits, route_ref, cnt_ref):
    n_e = logits.shape[0]
    iota_e = lax.broadcasted_iota(jnp.int32, (n_e, TM), 0)
    vals, onehots = [], []
    l = logits
    for _ in range(TOP_K):
        m = jnp.max(l, axis=0, keepdims=True)
        idx = jnp.min(jnp.where(l == m, iota_e, n_e), axis=0, keepdims=True)
        sel = iota_e == idx
        vals.append(m)
        onehots.append(sel)
        l = jnp.where(sel, -jnp.inf, l)
    ex = [jnp.exp(v - vals[0]) for v in vals]
    den = ex[0] + ex[1] + ex[2] + ex[3]
    gates = [e / den for e in ex]
    member = jnp.zeros((n_e, TM), F32)
    for sel in onehots:
        member = member + jnp.where(sel, 1.0, 0.0)
    r_i = lax.broadcasted_iota(jnp.int32, (TM, TM), 0)
    c_i = lax.broadcasted_iota(jnp.int32, (TM, TM), 1)
    upper = jnp.where(r_i < c_i, 1.0, 0.0).astype(BF16)
    cum = _dot(member.astype(BF16), upper)
    cnt = jnp.sum(member, axis=1, keepdims=True)
    cnt_ref[0] = cnt
    c8 = jnp.floor((cnt + (SEG_ALIGN - 1)) * (1.0 / SEG_ALIGN)) * SEG_ALIGN
    e_r = lax.broadcasted_iota(jnp.int32, (n_e, n_e), 0)
    e_c = lax.broadcasted_iota(jnp.int32, (n_e, n_e), 1)
    lower = jnp.where(e_c < e_r, 1.0, 0.0).astype(BF16)
    seg = _dot(lower, jnp.broadcast_to(c8, (n_e, TM)).astype(BF16))
    base = seg + cum
    rows = []
    for sel in onehots:
        rows.append(jnp.sum(jnp.where(sel, base, 0.0), axis=0, keepdims=True))
    route_ref[0] = jnp.concatenate(rows + gates, axis=0)


def _post_specs(n_tiles, d, n_e, sub=1):
    specs = [
        pl.BlockSpec((sub * TM, d), lambda i, *_: (i, 0)),
        pl.BlockSpec((sub * TM, d), lambda i, *_: (i, 0)),
        pl.BlockSpec((sub, 2 * TOP_K, TM), lambda i, *_: (i, 0, 0)),
        pl.BlockSpec((sub, n_e, 1), lambda i, *_: (i, 0, 0)),
    ]
    shapes = [
        jax.ShapeDtypeStruct((n_tiles * TM, d), F32),
        jax.ShapeDtypeStruct((n_tiles * TM, d), BF16),
        jax.ShapeDtypeStruct((n_tiles, 2 * TOP_K, TM), F32),
        jax.ShapeDtypeStruct((n_tiles, n_e, 1), F32),
    ]
    return specs, shapes


def _post(xs, ys, mods, g2, rwh, rwl, rb, x1_ref, h2_ref, route_ref, cnt_ref):
    logits = []
    for u, (x, y, mod) in enumerate(zip(xs, ys, mods)):
        rows = pl.ds(u * TM, TM)
        logits.append(_post_logits(x, y, mod, g2, rwh, rwl, rb, x1_ref.at[rows], h2_ref.at[rows]))
    for u, l in enumerate(logits):
        _post_route(l, route_ref.at[pl.ds(u, 1)], cnt_ref.at[pl.ds(u, 1)])


def _mod_specs(sub, d):
    return [pl.BlockSpec((1, 6, d), lambda i, src, mrow, *_, u=u: (mrow[sub * i + u], 0, 0))
            for u in range(sub)]


def _const_spec(shape):
    nd = len(shape)
    return pl.BlockSpec(shape, lambda i, *_: (0,) * nd)


def _pool_kernel(src, mrow, hp, hn, *refs, sub):
    x_refs = refs[:sub]
    xp_ref, xn_ref = refs[sub:sub + 2]
    mod_refs = refs[sub + 2:2 * sub + 2]
    g1_ref, pw_ref, ps_ref, g2_ref, rwh_ref, rwl_ref, rb_ref = refs[2 * sub + 2:2 * sub + 9]
    outs = refs[2 * sub + 9:2 * sub + 13]
    hh_scr = refs[2 * sub + 13]
    i = pl.program_id(0)
    d = x_refs[0].shape[1]
    gc = d // len(POOL_WINDOWS)
    halo = SEG_ALIGN
    g1 = g1_ref[...]

    def pre(v, mod):
        return _rms(v, g1) * (1.0 + mod[1:2]) + mod[0:1]

    mods = [m[0] for m in mod_refs]
    xs = [r[...] for r in x_refs]
    hs = [pre(x, mod) for x, mod in zip(xs, mods)]
    row = lax.broadcasted_iota(jnp.int32, (TM, 1), 0)
    ys = []
    for u in range(sub):
        has_prev = hp[sub * i + u] > 0
        has_next = hn[sub * i + u] > 0
        before = pre(xp_ref[...], mods[0]) if u == 0 else hs[u - 1][TM - halo:, :]
        after = pre(xn_ref[...], mods[-1]) if u == sub - 1 else hs[u + 1][:halo, :]
        hh_scr[u, 0:halo, :] = jnp.where(has_prev, before, 0.0)
        hh_scr[u, halo:halo + TM, :] = hs[u]
        hh_scr[u, halo + TM:2 * halo + TM, :] = jnp.where(has_next, after, 0.0)
        parts = []
        for g, win in enumerate(POOL_WINDOWS):
            half = win // 2
            c0 = g * gc
            acc = hh_scr[u, pl.ds(halo - half, TM), c0:c0 + gc]
            for j in range(-half + 1, half):
                acc = acc + hh_scr[u, pl.ds(halo + j, TM), c0:c0 + gc]
            lo_clip = jnp.where(has_prev, 0, jnp.maximum(half - row, 0))
            hi_clip = jnp.where(has_next, 0, jnp.maximum(row + half - TM, 0))
            cnt = (win - lo_clip - hi_clip).astype(F32)
            diff = acc / cnt - hs[u][:, c0:c0 + gc]
            parts.append(_dot(diff.astype(BF16), pw_ref[g]))
        ys.append(jnp.concatenate(parts, axis=1) * ps_ref[...])
    _post(xs, ys, mods, g2_ref[...], rwh_ref[...], rwl_ref[...], rb_ref[...], *outs)


def _pool_layer(xu, tinfo, mods, g1, pool_w, pool_scale, g2, rwh, rwl, rb):
    rows, d = xu.shape
    n_steps = tinfo[0].shape[0]
    n_e = rwh.shape[0]
    gc = pool_w.shape[-1]
    rpb = TM // SEG_ALIGN
    sub = SUB_TILES
    out_specs, out_shapes = _post_specs(n_steps, d, n_e, sub)
    grid_spec = pltpu.PrefetchScalarGridSpec(
        num_scalar_prefetch=4,
        grid=(n_steps // sub,),
        in_specs=[
            pl.BlockSpec((TM, d), lambda i, src, mrow, hp, hn, u=u: (src[sub * i + u], 0)) for u in range(sub)
        ] + [
            pl.BlockSpec((SEG_ALIGN, d),
                         lambda i, src, mrow, hp, hn: (jnp.maximum(src[sub * i] * rpb - 1, 0), 0)),
            pl.BlockSpec((SEG_ALIGN, d),
                         lambda i, src, mrow, hp, hn: (jnp.minimum((src[sub * i + sub - 1] + 1) * rpb,
                                                                   rows // SEG_ALIGN - 1), 0)),
        ] + _mod_specs(sub, d) + [
            _const_spec((1, d)),
            _const_spec((len(POOL_WINDOWS), gc, gc)),
            _const_spec((1, d)),
            _const_spec((1, d)),
            _const_spec((n_e, d)),
            _const_spec((n_e, d)),
            _const_spec((n_e, 1)),
        ],
        out_specs=out_specs,
        scratch_shapes=[pltpu.VMEM((sub, TM + 2 * SEG_ALIGN, d), F32)],
    )
    return pl.pallas_call(
        functools.partial(_pool_kernel, sub=sub), grid_spec=grid_spec, out_shape=out_shapes,
        compiler_params=_cparams(("arbitrary",)), name="pool_mixer",
    )(*tinfo, *([xu] * (sub + 2)), *([mods] * sub), g1, pool_w.astype(BF16), pool_scale, g2, rwh, rwl, rb)


def _ssm_kernel(tile_of, x_ref, mod_ref, g1_ref, are_ref, aim_ref, bb_ref, cb_ref, y_ref,
                u_scr, y_scr, h_scr, *xs_scrs, n_ctx_tiles):
    dr = pl.program_id(0)
    i = pl.program_id(1)
    nb, tt, d = x_ref.shape
    n_sets = bb_ref.shape[1]
    sw = bb_ref.shape[3]
    hw = sw // 2
    uw = bb_ref.shape[2]
    is_ctx = tile_of[dr * pl.num_programs(1) + i] < n_ctx_tiles

    @pl.when(i == 0)
    def _():
        h_scr[...] = jnp.zeros_like(h_scr)

    g1 = g1_ref[...]
    for b in range(nb):
        shift = jnp.where(is_ctx, mod_ref[nb, 0:1, :], mod_ref[b, 0:1, :])
        scale = jnp.where(is_ctx, mod_ref[nb, 1:2, :], mod_ref[b, 1:2, :])
        hb = _rms(x_ref[b], g1) * (1.0 + scale) + shift
        for j in range(n_sets):
            u_scr[j, pl.ds(b, tt, stride=nb), :] = hb[:, j * uw:(j + 1) * uw]
    for j in range(n_sets):
        xs_scrs[j][...] = _dot(u_scr[j].astype(BF16), bb_ref[0, j])
    for j in range(n_sets):
        xs = xs_scrs[j]
        ar = jnp.broadcast_to(are_ref[0, j:j + 1, :], (nb, hw))
        ai = jnp.broadcast_to(aim_ref[0, j:j + 1, :], (nb, hw))
        hr = h_scr[:, j * sw:j * sw + hw]
        hi = h_scr[:, j * sw + hw:(j + 1) * sw]
        for step in range(tt):
            t = step + dr * (tt - 1 - 2 * step)
            r0 = pl.multiple_of(t * nb, nb)
            nhr = ar * hr - ai * hi + xs[pl.ds(r0, nb), 0:hw]
            nhi = ar * hi + ai * hr + xs[pl.ds(r0, nb), hw:sw]
            xs[pl.ds(r0, nb), 0:hw] = nhr
            xs[pl.ds(r0, nb), hw:sw] = nhi
            hr, hi = nhr, nhi
        h_scr[:, j * sw:j * sw + hw] = hr
        h_scr[:, j * sw + hw:(j + 1) * sw] = hi
        y_scr[j] = _dot(xs[...].astype(BF16), cb_ref[0, j])
    for b in range(nb):
        for j in range(n_sets):
            y_ref[0, b, :, j * uw:(j + 1) * uw] = y_scr[j, pl.ds(b, tt, stride=nb), :]


def _ssm_scan(x3, mods, g1, a_re, a_im, bblk, cblk, n_ctx):
    nb, s, d = x3.shape
    tt = SSM_TT
    nt = s // tt
    nct = n_ctx // tt
    fwd = np.arange(nt)
    bwd = np.concatenate([np.arange(nct)[::-1], np.arange(nct, nt)[::-1]])
    tile_of = jnp.asarray(np.concatenate([fwd, bwd]), jnp.int32)
    n_sets, uw, sw = bblk.shape[1:]
    grid_spec = pltpu.PrefetchScalarGridSpec(
        num_scalar_prefetch=1,
        grid=(2, nt),
        in_specs=[
            pl.BlockSpec((nb, tt, d), lambda dr, i, to: (0, to[dr * nt + i], 0)),
            pl.BlockSpec(mods.shape, lambda dr, i, to: (0, 0, 0)),
            pl.BlockSpec((1, d), lambda dr, i, to: (0, 0)),
            pl.BlockSpec((1, n_sets, sw // 2), lambda dr, i, to: (dr, 0, 0)),
            pl.BlockSpec((1, n_sets, sw // 2), lambda dr, i, to: (dr, 0, 0)),
            pl.BlockSpec((1, n_sets, uw, sw), lambda dr, i, to: (dr, 0, 0, 0)),
            pl.BlockSpec((1, n_sets, sw, uw), lambda dr, i, to: (dr, 0, 0, 0)),
        ],
        out_specs=pl.BlockSpec((1, nb, tt, d), lambda dr, i, to: (dr, 0, to[dr * nt + i], 0)),
        scratch_shapes=[
            pltpu.VMEM((n_sets, tt * nb, uw), F32),
            pltpu.VMEM((n_sets, tt * nb, uw), F32),
            pltpu.VMEM((nb, n_sets * sw), F32),
        ] + [pltpu.VMEM((tt * nb, sw), F32) for _ in range(n_sets)],
    )
    return pl.pallas_call(
        functools.partial(_ssm_kernel, n_ctx_tiles=nct), grid_spec=grid_spec,
        out_shape=jax.ShapeDtypeStruct((2, nb, s, d), F32),
        compiler_params=_cparams(("arbitrary", "arbitrary")), name="s5_scan",
    )(tile_of, x3, mods, g1, a_re, a_im, bblk, cblk)


def _ssm_params(lam_re, lam_im, log_dt, b_re, b_im, c_re, c_im):
    g, p = lam_re.shape[1:]
    h = b_re.shape[-1]
    ns = g // SSM_SET
    eye = jnp.eye(SSM_SET, dtype=F32)
    outs = []
    for dr in range(2):
        lr, li = lam_re[dr].astype(F32), lam_im[dr].astype(F32)
        br, bi = b_re[dr].astype(F32), b_im[dr].astype(F32)
        dt = jnp.exp(log_dt[dr].astype(F32))[:, None]
        zr, zi = lr * dt, li * dt
        mag = jnp.exp(zr)
        ar, ai = mag * jnp.cos(zi), mag * jnp.sin(zi)
        den = lr * lr + li * li
        cr = ((ar - 1.0) * lr + ai * li) / den
        ci = (ai * lr - (ar - 1.0) * li) / den
        bbr = cr[..., None] * br - ci[..., None] * bi
        bbi = cr[..., None] * bi + ci[..., None] * br

        def blk_b(w):
            w = jnp.transpose(w, (0, 2, 1)).reshape(ns, SSM_SET, h, p)
            return jnp.einsum('ab,jahp->jahbp', eye, w).reshape(ns, SSM_SET * h, SSM_SET * p)

        def blk_c(w):
            w = jnp.transpose(w.reshape(ns, SSM_SET, h, p), (0, 1, 3, 2))
            return jnp.einsum('ab,japh->japbh', eye, w).reshape(ns, SSM_SET * p, SSM_SET * h)

        bblk = jnp.concatenate([blk_b(bbr), blk_b(bbi)], axis=2)
        cblk = jnp.concatenate([blk_c(c_re[dr].astype(F32)), -blk_c(c_im[dr].astype(F32))], axis=1)
        outs.append((ar.reshape(ns, SSM_SET * p), ai.reshape(ns, SSM_SET * p), bblk, cblk))
    a_re = jnp.stack([o[0] for o in outs])
    a_im = jnp.stack([o[1] for o in outs])
    bblk = jnp.stack([o[2] for o in outs]).astype(BF16)
    cblk = jnp.stack([o[3] for o in outs]).astype(BF16)
    return a_re, a_im, bblk, cblk


def _glu_kernel(src, mrow, x_ref, yf_ref, yb_ref, *refs, sub):
    mod_refs = refs[:sub]
    g1_ref, dsk_ref, gw_ref, gb_ref, g2_ref, rwh_ref, rwl_ref, rb_ref = refs[sub:sub + 8]
    outs = refs[sub + 8:]
    d = x_ref.shape[1]
    xs, ys, mods = [], [], []
    for u in range(sub):
        rows = pl.ds(u * TM, TM)
        mod = mod_refs[u][0]
        x = x_ref[rows, :]
        h = _rms(x, g1_ref[...]) * (1.0 + mod[1:2]) + mod[0:1]
        y = yf_ref[0, rows, :] + yb_ref[0, rows, :] + dsk_ref[...] * h
        gl = 0.5 * y * (1.0 + jnp.tanh(math.sqrt(2.0 / math.pi) * (y + 0.044715 * (y * y * y))))
        z = _dot(gl.astype(BF16), gw_ref[...]) + gb_ref[...]
        xs.append(x)
        ys.append(z[:, :d] * _sigmoid(z[:, d:]))
        mods.append(mod)
    _post(xs, ys, mods, g2_ref[...], rwh_ref[...], rwl_ref[...], rb_ref[...], *outs)


def _glu_layer(xu, y2, tinfo, mods, g1, d_skip, glu_w, glu_b, g2, rwh, rwl, rb):
    rows, d = xu.shape
    n_steps = rows // TM
    n_e = rwh.shape[0]
    sub = SUB_TILES
    out_specs, out_shapes = _post_specs(n_steps, d, n_e, sub)
    grid_spec = pltpu.PrefetchScalarGridSpec(
        num_scalar_prefetch=2,
        grid=(n_steps // sub,),
        in_specs=[
            pl.BlockSpec((sub * TM, d), lambda i, src, mrow: (i, 0)),
            pl.BlockSpec((1, sub * TM, d), lambda i, src, mrow: (0, i, 0)),
            pl.BlockSpec((1, sub * TM, d), lambda i, src, mrow: (1, i, 0)),
        ] + _mod_specs(sub, d) + [
            _const_spec((1, d)),
            _const_spec((1, d)),
            _const_spec((d, 2 * d)),
            _const_spec((1, 2 * d)),
            _const_spec((1, d)),
            _const_spec((n_e, d)),
            _const_spec((n_e, d)),
            _const_spec((n_e, 1)),
        ],
        out_specs=out_specs,
    )
    return pl.pallas_call(
        functools.partial(_glu_kernel, sub=sub), grid_spec=grid_spec, out_shape=out_shapes,
        compiler_params=_cparams(("arbitrary",)), name="s5_glu",
    )(tinfo[0], tinfo[1], xu, y2, y2, *([mods] * sub), g1, d_skip, glu_w.astype(BF16), glu_b, g2, rwh, rwl, rb)


def _rope(v, cos, sin_signed, first_half):
    partner = jnp.where(first_half, pltpu.roll(v, HEAD_DIM - HEAD_DIM // 4, axis=1),
                        pltpu.roll(v, HEAD_DIM // 4, axis=1))
    return v * cos + partner * sin_signed


def _qkv_kernel(src, mrow, trow, x_ref, mod_ref, g1_ref, w_ref, qg_ref, kg_ref, cos_ref, sin_ref,
                q_ref, k_ref, v_ref):
    d = x_ref.shape[1]
    kvw = k_ref.shape[1]
    mod = mod_ref[0]
    h = _rms(x_ref[...], g1_ref[...]) * (1.0 + mod[1:2]) + mod[0:1]
    z = _dot(h.astype(BF16), w_ref[...])
    cos = cos_ref[...]
    sin = sin_ref[...]
    lane = lax.broadcasted_iota(jnp.int32, (TM, HEAD_DIM), 1)
    first_half = (lane % (HEAD_DIM // 2)) < (HEAD_DIM // 4)
    q_scale = HEAD_DIM ** -0.5
    for hd in range(d // HEAD_DIM):
        zh = z[:, hd * HEAD_DIM:(hd + 1) * HEAD_DIM]
        zh = _rope(_rms(zh, qg_ref[...]), cos, sin, first_half) * q_scale
        q_ref[:, hd * HEAD_DIM:(hd + 1) * HEAD_DIM] = zh.astype(BF16)
    for hd in range(kvw // HEAD_DIM):
        zh = z[:, d + hd * HEAD_DIM:d + (hd + 1) * HEAD_DIM]
        zh = _rope(_rms(zh, kg_ref[...]), cos, sin, first_half)
        k_ref[:, hd * HEAD_DIM:(hd + 1) * HEAD_DIM] = zh.astype(BF16)
    v_ref[...] = z[:, d + kvw:].astype(BF16)


def _attn_kernel(q_ref, k_ref, v_ref, o_ref, *, n_ctx):
    qt = pl.program_id(2)

    def attend(n_keys):
        for hd in range(ATTN_HEADS):
            cols = slice(hd * HEAD_DIM, (hd + 1) * HEAD_DIM)
            s = _dot_nt(q_ref[:, cols], k_ref[0:n_keys, :])
            m = jnp.max(s, axis=-1, keepdims=True)
            p = jnp.exp(s - m)
            den = jnp.sum(p, axis=-1, keepdims=True)
            o_ref[:, cols] = (_dot(p.astype(BF16), v_ref[0:n_keys, :]) / den).astype(BF16)

    @pl.when(qt < n_ctx // TM)
    def _():
        attend(n_ctx)

    @pl.when(qt >= n_ctx // TM)
    def _():
        attend(k_ref.shape[0])


def _wo_kernel(src, mrow, x_ref, o_ref, *refs, sub):
    mod_refs = refs[:sub]
    wo_ref, g2_ref, rwh_ref, rwl_ref, rb_ref = refs[sub:sub + 5]
    outs = refs[sub + 5:]
    tiles = [pl.ds(u * TM, TM) for u in range(sub)]
    ys = [_dot(o_ref[rows, :], wo_ref[...]) for rows in tiles]
    _post([x_ref[rows, :] for rows in tiles], ys, [m[0] for m in mod_refs],
          g2_ref[...], rwh_ref[...], rwl_ref[...], rb_ref[...], *outs)


def _rope_tables(n_ctx, n_lat):
    rows = n_lat // GRID_W
    row = jnp.repeat(jnp.arange(rows), GRID_W)
    col = jnp.tile(jnp.arange(GRID_W), rows)
    pos = jnp.stack([row, col], axis=-1).astype(F32)
    inv_freq = ROPE_THETA ** (-jnp.arange(ROPE_F, dtype=F32) / ROPE_F)
    ang = pos[:, :, None] * inv_freq
    cos, sin = jnp.cos(ang), jnp.sin(ang)
    cos_t = jnp.concatenate([cos, cos], axis=-1).reshape(n_lat, HEAD_DIM)
    sin_t = jnp.concatenate([-sin, sin], axis=-1).reshape(n_lat, HEAD_DIM)
    cos_t = jnp.concatenate([jnp.ones((n_ctx, HEAD_DIM), F32), cos_t], axis=0)
    sin_t = jnp.concatenate([jnp.zeros((n_ctx, HEAD_DIM), F32), sin_t], axis=0)
    return cos_t, sin_t


def _attn_layer(xu, tinfo, trow, mods, g1, wqkv, q_g, k_g, wo, g2, rwh, rwl, rb, nb, n_ctx):
    rows, d = xu.shape
    s = rows // nb
    n_steps = rows // TM
    n_e = rwh.shape[0]
    kvw = N_KV_HEADS * HEAD_DIM
    n_heads = d // HEAD_DIM
    rep = n_heads // N_KV_HEADS
    cos_t, sin_t = _rope_tables(n_ctx, s - n_ctx)
    grid_spec = pltpu.PrefetchScalarGridSpec(
        num_scalar_prefetch=3,
        grid=(n_steps,),
        in_specs=[
            pl.BlockSpec((TM, d), lambda i, src, mrow, tr: (i, 0)),
            pl.BlockSpec((1, 6, d), lambda i, src, mrow, tr: (mrow[i], 0, 0)),
            _const_spec((1, d)),
            _const_spec((d, d + 2 * kvw)),
            _const_spec((1, HEAD_DIM)),
            _const_spec((1, HEAD_DIM)),
            pl.BlockSpec((TM, HEAD_DIM), lambda i, src, mrow, tr: (tr[i], 0)),
            pl.BlockSpec((TM, HEAD_DIM), lambda i, src, mrow, tr: (tr[i], 0)),
        ],
        out_specs=[
            pl.BlockSpec((TM, d), lambda i, *_: (i, 0)),
            pl.BlockSpec((TM, kvw), lambda i, *_: (i, 0)),
            pl.BlockSpec((TM, kvw), lambda i, *_: (i, 0)),
        ],
    )
    q, k, v = pl.pallas_call(
        _qkv_kernel, grid_spec=grid_spec,
        out_shape=[jax.ShapeDtypeStruct((rows, d), BF16),
                   jax.ShapeDtypeStruct((rows, kvw), BF16),
                   jax.ShapeDtypeStruct((rows, kvw), BF16)],
        compiler_params=_cparams(("arbitrary",)), name="attn_qkv",
    )(tinfo[0], tinfo[1], trow, xu, mods, g1, wqkv.astype(BF16), q_g, k_g, cos_t, sin_t)

    tpb = s // TM
    o = pl.pallas_call(
        functools.partial(_attn_kernel, n_ctx=n_ctx),
        grid=(nb, n_heads // ATTN_HEADS, tpb),
        in_specs=[
            pl.BlockSpec((TM, ATTN_HEADS * HEAD_DIM), lambda b, hp, t: (b * tpb + t, hp)),
            pl.BlockSpec((s, HEAD_DIM), lambda b, hp, t: (b, hp * ATTN_HEADS // rep)),
            pl.BlockSpec((s, HEAD_DIM), lambda b, hp, t: (b, hp * ATTN_HEADS // rep)),
        ],
        out_specs=pl.BlockSpec((TM, ATTN_HEADS * HEAD_DIM), lambda b, hp, t: (b * tpb + t, hp)),
        out_shape=jax.ShapeDtypeStruct((rows, d), BF16),
        compiler_params=_cparams(("arbitrary", "arbitrary", "arbitrary")), name="attn_core",
    )(q, k, v)

    sub = SUB_TILES
    out_specs, out_shapes = _post_specs(n_steps, d, n_e, sub)
    grid_spec = pltpu.PrefetchScalarGridSpec(
        num_scalar_prefetch=2,
        grid=(n_steps // sub,),
        in_specs=[
            pl.BlockSpec((sub * TM, d), lambda i, src, mrow: (i, 0)),
            pl.BlockSpec((sub * TM, d), lambda i, src, mrow: (i, 0)),
        ] + _mod_specs(sub, d) + [
            _const_spec((d, d)),
            _const_spec((1, d)),
            _const_spec((n_e, d)),
            _const_spec((n_e, d)),
            _const_spec((n_e, 1)),
        ],
        out_specs=out_specs,
    )
    return pl.pallas_call(
        functools.partial(_wo_kernel, sub=sub), grid_spec=grid_spec, out_shape=out_shapes,
        compiler_params=_cparams(("arbitrary",)), name="attn_out",
    )(tinfo[0], tinfo[1], xu, o, *([mods] * sub), wo.astype(BF16), g2, rwh, rwl, rb)


def _loc_rows(n_e):
    return -(-(TM * TOP_K + (SEG_ALIGN - 1) * n_e) // 256) * 256


def _pack_pairs(v, rounded):
    half = v.shape[1] // 2
    lo, hi = v[:, :half], v[:, half:]
    if not rounded:
        lo, hi = lo.astype(BF16).astype(F32), hi.astype(BF16).astype(F32)
    lo = lax.bitcast_convert_type(lo, jnp.uint32)
    hi = lax.bitcast_convert_type(hi, jnp.uint32)
    return (hi & jnp.uint32(0xFFFF0000)) | (lo >> 16)


def _unpack_pairs(p):
    lo = lax.bitcast_convert_type(p << 16, F32)
    hi = lax.bitcast_convert_type(p & jnp.uint32(0xFFFF0000), F32)
    return jnp.concatenate([lo, hi], axis=1).astype(BF16)


def _start_segments(t, n_e, loc, len8, dst, make_copy):
    def per_expert(e, carry):
        n = pl.multiple_of(len8[t * n_e + e], SEG_ALIGN)

        @pl.when(n > 0)
        def _():
            make_copy(pl.multiple_of(loc[t * n_e + e], SEG_ALIGN),
                      pl.multiple_of(dst[t * n_e + e], SEG_ALIGN), n).start()
        return carry

    lax.fori_loop(0, n_e, per_expert, 0)


def _dispatch_kernel(loc, len8, dst, tot, padoff, padlen, nused, h2_ref, route_ref, xs_ref,
                     buf, zbuf, sems, zsem, *, n_e, nb_max):
    i = pl.program_id(0)
    n_steps = pl.num_programs(0)
    slot = i % 2
    lrows = buf.shape[1]
    pos = route_ref[0, 0:TOP_K, :].astype(jnp.int32).astype(jnp.int16)
    iota_p = lax.broadcasted_iota(jnp.int16, (lrows, TM), 0)
    perm = jnp.zeros((lrows, TM), BF16)
    for k in range(TOP_K):
        perm = jnp.where(iota_p == pos[k:k + 1], jnp.ones((), BF16), perm)
    buf[slot] = _pack_pairs(_dot(perm, h2_ref[...]), rounded=True)

    def seg_copy(s):
        def make(a, g, n):
            return pltpu.make_async_copy(buf.at[s, pl.ds(a, n)], xs_ref.at[pl.ds(g, n)], sems.at[s])
        return make

    def wait_tile(t, s):
        seg_copy(s)(0, 0, pl.multiple_of(tot[t], SEG_ALIGN)).wait()

    _start_segments(i, n_e, loc, len8, dst, seg_copy(slot))

    @pl.when(i > 0)
    def _():
        wait_tile(i - 1, 1 - slot)

    @pl.when(i == n_steps - 1)
    def _():
        wait_tile(i, slot)

    @pl.when(i == n_steps - 1)
    def _():
        zbuf[...] = jnp.zeros_like(zbuf)

        def zero_rows(g, n):
            return pltpu.make_async_copy(zbuf.at[pl.ds(0, n)], xs_ref.at[pl.ds(g, n)], zsem)

        def pad(e, total):
            n = pl.multiple_of(padlen[e], SEG_ALIGN)

            @pl.when(n > 0)
            def _():
                zero_rows(pl.multiple_of(padoff[e], SEG_ALIGN), n).start()
            return total + n

        total = lax.fori_loop(0, n_e, pad, 0)

        def blank(b, carry):
            zero_rows(pl.multiple_of(b * BLK, BLK), BLK).start()
            return carry

        lax.fori_loop(nused[0], nb_max, blank, 0)
        total = pl.multiple_of(total + (nb_max - nused[0]) * BLK, SEG_ALIGN)

        @pl.when(total > 0)
        def _():
            pltpu.make_async_copy(xs_ref.at[pl.ds(0, total)], xs_ref.at[pl.ds(0, total)], zsem).wait()


def _expert_kernel(blk, blk_e, valid, nused, xs_ref, w1_ref, b1_ref, w2_ref, b2_ref, ys_ref, w1b, w2b):
    i = pl.program_id(0)
    f = w2b.shape[0]

    @pl.when(i >= nused[0])
    def _():
        ys_ref[...] = jnp.zeros_like(ys_ref)

    @pl.when(i < nused[0])
    def _():
        first = jnp.logical_or(i == 0, blk_e[i] != blk_e[jnp.maximum(i - 1, 0)])

        @pl.when(first)
        def _():
            w1b[...] = w1_ref[0, 0].astype(BF16)
            w2b[...] = w2_ref[0, 0].astype(BF16)

        for c in range(BLK // EXPERT_CHUNK):
            r0 = c * EXPERT_CHUNK
            row = r0 + lax.broadcasted_iota(jnp.int32, (EXPERT_CHUNK, 1), 0)
            x = _unpack_pairs(jnp.where(row < valid[i], xs_ref[r0:r0 + EXPERT_CHUNK, :], jnp.uint32(0)))
            z = _dot(x, w1b[...]) + b1_ref[0, 0]
            glu = jnp.minimum(z[:, :f], SWIGLU_LIMIT)
            lin = jnp.clip(z[:, f:], -SWIGLU_LIMIT, SWIGLU_LIMIT)
            act = glu * _sigmoid(SWIGLU_ALPHA * glu) * (lin + 1.0)
            y = _dot(act.astype(BF16), w2b[...]) + b2_ref[0, 0]
            ys_ref[r0:r0 + EXPERT_CHUNK, :] = _pack_pairs(y, rounded=False)


def _combine_kernel(loc, len8, dst, tot, mrow, ys_ref, route_ref, x1_ref, mod_ref, out_ref, buf, sems, *, n_e):
    i = pl.program_id(0)
    n_steps = pl.num_programs(0)
    slot = i % 2
    lrows = buf.shape[1]

    def seg_copy(s):
        def make(a, g, n):
            return pltpu.make_async_copy(ys_ref.at[pl.ds(g, n)], buf.at[s, pl.ds(a, n)], sems.at[s])
        return make

    @pl.when(i == 0)
    def _():
        _start_segments(0, n_e, loc, len8, dst, seg_copy(0))

    @pl.when(i + 1 < n_steps)
    def _():
        _start_segments(i + 1, n_e, loc, len8, dst, seg_copy(1 - slot))

    total = pl.multiple_of(tot[i], SEG_ALIGN)
    seg_copy(slot)(0, 0, total).wait()

    pos = route_ref[0, 0:TOP_K, :].astype(jnp.int32).astype(jnp.int16)
    gate = route_ref[0, TOP_K:2 * TOP_K, :].astype(BF16)
    iota_p = lax.broadcasted_iota(jnp.int16, (lrows, TM), 0)
    gt = jnp.zeros((lrows, TM), BF16)
    for k in range(TOP_K):
        gt = jnp.where(iota_p == pos[k:k + 1], gate[k:k + 1], gt)
    row = lax.broadcasted_iota(jnp.int32, (lrows, 1), 0)
    ysl = _unpack_pairs(jnp.where(row < total, buf[slot], jnp.uint32(0)))
    f = _dot_tn(gt, ysl)
    out_ref[...] = x1_ref[...] + mod_ref[0][5:6] * f


def _moe(x1, h2, route, cnt, mods, mrow, layer, w1, b1, w2, b2):
    rows, d = x1.shape
    n_tiles = rows // TM
    depth, n_e, _, f2 = w1.shape
    lrows = _loc_rows(n_e)
    nb_max = (rows * TOP_K + (SEG_ALIGN - 1) * n_tiles * n_e) // BLK + n_e
    cap = nb_max * BLK
    dp = d // 2

    i32 = lambda v: v.astype(jnp.int32)
    cnt = i32(cnt.reshape(n_tiles, n_e))
    c8 = (cnt + SEG_ALIGN - 1) // SEG_ALIGN * SEG_ALIGN
    loc = jnp.cumsum(c8, axis=1) - c8
    tot_tile = jnp.sum(c8, axis=1)
    tot = jnp.sum(c8, axis=0)
    nblk = (tot + BLK - 1) // BLK
    blk_end = jnp.cumsum(nblk)
    blk_start = blk_end - nblk
    dst = (blk_start * BLK)[None, :] + jnp.cumsum(c8, axis=0) - c8
    nused = blk_end[-1]
    bid = jnp.arange(nb_max, dtype=jnp.int32)
    bidc = jnp.minimum(bid, nused - 1)
    blk_e = jnp.minimum(jnp.sum(i32(bidc[:, None] >= blk_end[None, :]), axis=1), n_e - 1)
    valid = jnp.clip(tot[blk_e] - (bidc - blk_start[blk_e]) * BLK, 0, BLK)
    padoff = blk_start * BLK + tot
    padlen = nblk * BLK - tot
    loc, len8, dst = i32(loc.reshape(-1)), i32(c8.reshape(-1)), i32(dst.reshape(-1))
    nused = i32(nused.reshape(1))

    xs = pl.pallas_call(
        functools.partial(_dispatch_kernel, n_e=n_e, nb_max=nb_max),
        grid_spec=pltpu.PrefetchScalarGridSpec(
            num_scalar_prefetch=7,
            grid=(n_tiles,),
            in_specs=[
                pl.BlockSpec((TM, d), lambda i, *_: (i, 0)),
                pl.BlockSpec((1, 2 * TOP_K, TM), lambda i, *_: (i, 0, 0)),
            ],
            out_specs=pl.BlockSpec(memory_space=pl.ANY),
            scratch_shapes=[pltpu.VMEM((2, lrows, dp), jnp.uint32), pltpu.VMEM((BLK, dp), jnp.uint32),
                            pltpu.SemaphoreType.DMA((2,)), pltpu.SemaphoreType.DMA(())],
        ),
        out_shape=jax.ShapeDtypeStruct((cap, dp), jnp.uint32),
        compiler_params=_cparams(("arbitrary",)), name="moe_dispatch",
    )(loc, len8, dst, i32(tot_tile), i32(padoff), i32(padlen), nused, h2, route)

    ys = pl.pallas_call(
        _expert_kernel,
        grid_spec=pltpu.PrefetchScalarGridSpec(
            num_scalar_prefetch=4,
            grid=(nb_max,),
            in_specs=[
                pl.BlockSpec((BLK, dp), lambda i, blk, be, va, nu: (blk[i], 0)),
                pl.BlockSpec((1, 1, d, f2), lambda i, blk, be, va, nu: (layer, be[i], 0, 0)),
                pl.BlockSpec((1, 1, 1, f2), lambda i, blk, be, va, nu: (layer, be[i], 0, 0)),
                pl.BlockSpec((1, 1, f2 // 2, d), lambda i, blk, be, va, nu: (layer, be[i], 0, 0)),
                pl.BlockSpec((1, 1, 1, d), lambda i, blk, be, va, nu: (layer, be[i], 0, 0)),
            ],
            out_specs=pl.BlockSpec((BLK, dp), lambda i, blk, be, va, nu: (i, 0)),
            scratch_shapes=[pltpu.VMEM((d, f2), BF16), pltpu.VMEM((f2 // 2, d), BF16)],
        ),
        out_shape=jax.ShapeDtypeStruct((cap, dp), jnp.uint32),
        compiler_params=_cparams(("arbitrary",)), name="moe_experts",
    )(i32(bidc), i32(blk_e), i32(valid), nused, xs, w1, b1.reshape(depth, n_e, 1, f2), w2,
      b2.reshape(depth, n_e, 1, d))

    return pl.pallas_call(
        functools.partial(_combine_kernel, n_e=n_e),
        grid_spec=pltpu.PrefetchScalarGridSpec(
            num_scalar_prefetch=5,
            grid=(n_tiles,),
            in_specs=[
                pl.BlockSpec(memory_space=pl.ANY),
                pl.BlockSpec((1, 2 * TOP_K, TM), lambda i, *_: (i, 0, 0)),
                pl.BlockSpec((TM, d), lambda i, *_: (i, 0)),
                pl.BlockSpec((1, 6, d), lambda i, lo, le, ds, to, mr: (mr[i], 0, 0)),
            ],
            out_specs=pl.BlockSpec((TM, d), lambda i, *_: (i, 0)),
            scratch_shapes=[pltpu.VMEM((2, lrows, dp), jnp.uint32), pltpu.SemaphoreType.DMA((2,))],
        ),
        out_shape=jax.ShapeDtypeStruct((rows, d), F32),
        compiler_params=_cparams(("arbitrary",)), name="moe_combine",
    )(loc, len8, dst, i32(tot_tile), mrow, ys, route, x1, mods)


def _tile_info(nb, n_ctx, n_lat, latent_only):
    tpb = (n_ctx + n_lat) // TM
    ct = n_ctx // TM
    src, mrow, hp, hn, trow = [], [], [], [], []
    for b in range(nb):
        for j in range(ct if latent_only else 0, tpb):
            is_ctx = j < ct
            src.append(b * tpb + j)
            mrow.append(nb if is_ctx else b)
            hp.append(0 if j in (0, ct) else 1)
            hn.append(0 if j in (ct - 1, tpb - 1) else 1)
            trow.append(j)
    mk = lambda v: jnp.asarray(np.asarray(v, np.int32))
    return (mk(src), mk(mrow), mk(hp), mk(hn)), mk(trow)


def kernel(x, c, ctx, c_ctx, ada_w, ada_b, norm1_g, norm2_g, pool_w, pool_scale, ssm_lam_re, ssm_lam_im,
           ssm_log_dt, ssm_b_re, ssm_b_im, ssm_c_re, ssm_c_im, ssm_d, ssm_glu_w, ssm_glu_b, attn_wqkv,
           attn_q_g, attn_k_g, attn_wo, router_w, router_b, moe_w1, moe_b1, moe_w2, moe_b2):
    nb, n_lat, d = x.shape
    n_ctx = ctx.shape[1]
    depth = ada_w.shape[0]
    n_e = router_w.shape[-1]
    assert nb == 8 and n_ctx % TM == 0 and n_lat % TM == 0 and n_lat % GRID_W == 0
    s = n_ctx + n_lat

    c16 = jnp.concatenate([c, c_ctx[None, :], jnp.zeros((16 - nb - 1, d), F32)], axis=0)
    mods_all = _ada_mods(c16, ada_w, ada_b).reshape(depth, 16, 6, d)

    xu = jnp.concatenate([ctx, x], axis=1).reshape(nb * s, d)
    tinfo_u, trow_u = _tile_info(nb, n_ctx, n_lat, latent_only=False)
    tinfo_l, _ = _tile_info(nb, n_ctx, n_lat, latent_only=True)

    for i in range(depth):
        kind, j = i % 3, i // 3
        last = i == depth - 1
        mods = mods_all[i]
        g1 = norm1_g[i].reshape(1, d)
        g2 = norm2_g[i].reshape(1, d)
        rwt = router_w[i].T
        rwh = rwt.astype(BF16)
        rwl = (rwt - rwh.astype(F32)).astype(BF16)
        rb = router_b[i].reshape(n_e, 1)
        if kind == 0:
            tinfo = tinfo_l if last else tinfo_u
            x1, h2, route, cnt = _pool_layer(xu, tinfo, mods, g1, pool_w[j], pool_scale[j].reshape(1, d),
                                             g2, rwh, rwl, rb)
            mrow = tinfo[1]
        elif kind == 1:
            assert not last
            a_re, a_im, bblk, cblk = _ssm_params(ssm_lam_re[j], ssm_lam_im[j], ssm_log_dt[j], ssm_b_re[j],
                                                 ssm_b_im[j], ssm_c_re[j], ssm_c_im[j])
            y2 = _ssm_scan(xu.reshape(nb, s, d), mods, g1, a_re, a_im, bblk, cblk, n_ctx)
            x1, h2, route, cnt = _glu_layer(xu, y2.reshape(2, nb * s, d), tinfo_u, mods, g1,
                                            ssm_d[j].reshape(1, d), ssm_glu_w[j], ssm_glu_b[j].reshape(1, 2 * d),
                                            g2, rwh, rwl, rb)
            mrow = tinfo_u[1]
        else:
            assert not last
            x1, h2, route, cnt = _attn_layer(xu, tinfo_u, trow_u, mods, g1, attn_wqkv[j],
                                             attn_q_g[j].reshape(1, HEAD_DIM), attn_k_g[j].reshape(1, HEAD_DIM),
                                             attn_wo[j], g2, rwh, rwl, rb, nb, n_ctx)
            mrow = tinfo_u[1]
        xu = _moe(x1, h2, route, cnt, mods, mrow, i, moe_w1, moe_b1, moe_w2, moe_b2)
    if xu.shape[0] == nb * n_lat:
        return xu.reshape(nb, n_lat, d)
    return xu.reshape(nb, s, d)[:, n_ctx:, :]
```

```python
import functools
import math

import numpy as np
import jax
import jax.numpy as jnp
from jax import lax
from jax.experimental import pallas as pl
from jax.experimental.pallas import tpu as pltpu

F32 = jnp.float32
BF16 = jnp.bfloat16

GRID_W = 64
NORM_EPS = 1e-6
POOL_WINDOWS = (2, 4, 8, 16)
SSM_H = 16
SSM_P = 64
SSM_SET = 8
SSM_TT = 32
HEAD_DIM = 128
N_KV_HEADS = 2
ATTN_HEADS = 4
ROPE_F = HEAD_DIM // 4
ROPE_THETA = 10000.0
TOP_K = 4
SWIGLU_ALPHA = 1.702
SWIGLU_LIMIT = 7.0

TM = 256
SUB_TILES = 2
BLK = 512
EXPERT_CHUNK = 512
SEG_ALIGN = 8
VMEM_LIMIT = 56 * 1024 * 1024


def _cparams(sem, vmem=VMEM_LIMIT):
    return pltpu.CompilerParams(dimension_semantics=sem, vmem_limit_bytes=vmem)


def _rms(x, g):
    return x * lax.rsqrt(jnp.mean(x * x, axis=-1, keepdims=True) + NORM_EPS) * g


def _sigmoid(x):
    return 1.0 / (1.0 + jnp.exp(-x))


def _dot(a, b):
    return jnp.dot(a, b, preferred_element_type=F32)


def _dot_nt(a, b):
    return lax.dot_general(a, b, (((1,), (1,)), ((), ())), preferred_element_type=F32)


def _dot_tn(a, b):
    return lax.dot_general(a, b, (((0,), (0,)), ((), ())), preferred_element_type=F32)


def _ada_kernel(c_ref, w_ref, b_ref, o_ref):
    c = c_ref[...]
    s = c * _sigmoid(c)
    o_ref[0] = jnp.dot(s, w_ref[0], preferred_element_type=F32,
                       precision=lax.Precision.HIGHEST) + b_ref[0]


def _ada_mods(c16, ada_w, ada_b):
    depth, d, six_d = ada_w.shape
    tn = d
    return pl.pallas_call(
        _ada_kernel,
        grid=(depth, six_d // tn),
        in_specs=[
            pl.BlockSpec((16, d), lambda l, j: (0, 0)),
            pl.BlockSpec((1, d, tn), lambda l, j: (l, 0, j)),
            pl.BlockSpec((1, 1, tn), lambda l, j: (l, 0, j)),
        ],
        out_specs=pl.BlockSpec((1, 16, tn), lambda l, j: (l, 0, j)),
        out_shape=jax.ShapeDtypeStruct((depth, 16, six_d), F32),
        compiler_params=_cparams(("arbitrary", "arbitrary")),
        name="ada_mods",
    )(c16, ada_w, ada_b.reshape(depth, 1, six_d))


def _post_logits(x, y, mod, g2, rwh, rwl, rb, x1_ref, h2_ref):
    x1 = x + mod[2:3] * y
    x1_ref[...] = x1
    h2 = _rms(x1, g2) * (1.0 + mod[4:5]) + mod[3:4]
    h2_ref[...] = h2.astype(BF16)
    hh = h2.astype(BF16)
    hl = (h2 - hh.astype(F32)).astype(BF16)
    return _dot_nt(rwh, hh) + _dot_nt(rwh, hl) + _dot_nt(rwl, hh) + rb


def _post_route(logits, route_ref, cnt_ref):
    n_e = logits.shape[0]
    iota_e = lax.broadcasted_iota(jnp.int32, (n_e, TM), 0)
    vals, onehots = [], []
    l = logits
    for _ in range(TOP_K):
        m = jnp.max(l, axis=0, keepdims=True)
        idx = jnp.min(jnp.where(l == m, iota_e, n_e), axis=0, keepdims=True)
        sel = iota_e == idx
        vals.append(m)
        onehots.append(sel)
        l = jnp.where(sel, -jnp.inf, l)
    ex = [jnp.exp(v - vals[0]) for v in vals]
    den = ex[0] + ex[1] + ex[2] + ex[3]
    gates = [e / den for e in ex]
    member = jnp.zeros((n_e, TM), F32)
    for sel in onehots:
        member = member + jnp.where(sel, 1.0, 0.0)
    r_i = lax.broadcasted_iota(jnp.int32, (TM, TM), 0)
    c_i = lax.broadcasted_iota(jnp.int32, (TM, TM), 1)
    upper = jnp.where(r_i < c_i, 1.0, 0.0).astype(BF16)
    cum = _dot(member.astype(BF16), upper)
    cnt = jnp.sum(member, axis=1, keepdims=True)
    cnt_ref[0] = cnt
    c8 = jnp.floor((cnt + (SEG_ALIGN - 1)) * (1.0 / SEG_ALIGN)) * SEG_ALIGN
    e_r = lax.broadcasted_iota(jnp.int32, (n_e, n_e), 0)
    e_c = lax.broadcasted_iota(jnp.int32, (n_e, n_e), 1)
    lower = jnp.where(e_c < e_r, 1.0, 0.0).astype(BF16)
    seg = _dot(lower, jnp.broadcast_to(c8, (n_e, TM)).astype(BF16))
    base = seg + cum
    rows = []
    for sel in onehots:
        rows.append(jnp.sum(jnp.where(sel, base, 0.0), axis=0, keepdims=True))
    route_ref[0] = jnp.concatenate(rows + gates, axis=0)


def _post_specs(n_tiles, d, n_e, sub=1):
    specs = [
        pl.BlockSpec((sub * TM, d), lambda i, *_: (i, 0)),
        pl.BlockSpec((sub * TM, d), lambda i, *_: (i, 0)),
        pl.BlockSpec((sub, 2 * TOP_K, TM), lambda i, *_: (i, 0, 0)),
        pl.BlockSpec((sub, n_e, 1), lambda i, *_: (i, 0, 0)),
    ]
    shapes = [
        jax.ShapeDtypeStruct((n_tiles * TM, d), F32),
        jax.ShapeDtypeStruct((n_tiles * TM, d), BF16),
        jax.ShapeDtypeStruct((n_tiles, 2 * TOP_K, TM), F32),
        jax.ShapeDtypeStruct((n_tiles, n_e, 1), F32),
    ]
    return specs, shapes


def _post(xs, ys, mods, g2, rwh, rwl, rb, x1_ref, h2_ref, route_ref, cnt_ref):
    logits = []
    for u, (x, y, mod) in enumerate(zip(xs, ys, mods)):
        rows = pl.ds(u * TM, TM)
        logits.append(_post_logits(x, y, mod, g2, rwh, rwl, rb, x1_ref.at[rows], h2_ref.at[rows]))
    for u, l in enumerate(logits):
        _post_route(l, route_ref.at[pl.ds(u, 1)], cnt_ref.at[pl.ds(u, 1)])


def _mod_specs(sub, d):
    return [pl.BlockSpec((1, 6, d), lambda i, src, mrow, *_, u=u: (mrow[sub * i + u], 0, 0))
            for u in range(sub)]


def _const_spec(shape):
    nd = len(shape)
    return pl.BlockSpec(shape, lambda i, *_: (0,) * nd)


def _pool_kernel(src, mrow, hp, hn, *refs, sub):
    x_refs = refs[:sub]
    xp_ref, xn_ref = refs[sub:sub + 2]
    mod_refs = refs[sub + 2:2 * sub + 2]
    g1_ref, pw_ref, ps_ref, g2_ref, rwh_ref, rwl_ref, rb_ref = refs[2 * sub + 2:2 * sub + 9]
    outs = refs[2 * sub + 9:2 * sub + 13]
    hh_scr = refs[2 * sub + 13]
    i = pl.program_id(0)
    d = x_refs[0].shape[1]
    gc = d // len(POOL_WINDOWS)
    halo = SEG_ALIGN
    g1 = g1_ref[...]

    def pre(v, mod):
        return _rms(v, g1) * (1.0 + mod[1:2]) + mod[0:1]

    mods = [m[0] for m in mod_refs]
    xs = [r[...] for r in x_refs]
    hs = [pre(x, mod) for x, mod in zip(xs, mods)]
    row = lax.broadcasted_iota(jnp.int32, (TM, 1), 0)
    ys = []
    for u in range(sub):
        has_prev = hp[sub * i + u] > 0
        has_next = hn[sub * i + u] > 0
        before = pre(xp_ref[...], mods[0]) if u == 0 else hs[u - 1][TM - halo:, :]
        after = pre(xn_ref[...], mods[-1]) if u == sub - 1 else hs[u + 1][:halo, :]
        hh_scr[u, 0:halo, :] = jnp.where(has_prev, before, 0.0)
        hh_scr[u, halo:halo + TM, :] = hs[u]
        hh_scr[u, halo + TM:2 * halo + TM, :] = jnp.where(has_next, after, 0.0)
        parts = []
        for g, win in enumerate(POOL_WINDOWS):
            half = win // 2
            c0 = g * gc
            acc = hh_scr[u, pl.ds(halo - half, TM), c0:c0 + gc]
            for j in range(-half + 1, half):
                acc = acc + hh_scr[u, pl.ds(halo + j, TM), c0:c0 + gc]
            lo_clip = jnp.where(has_prev, 0, jnp.maximum(half - row, 0))
            hi_clip = jnp.where(has_next, 0, jnp.maximum(row + half - TM, 0))
            cnt = (win - lo_clip - hi_clip).astype(F32)
            diff = acc / cnt - hs[u][:, c0:c0 + gc]
            parts.append(_dot(diff.astype(BF16), pw_ref[g]))
        ys.append(jnp.concatenate(parts, axis=1) * ps_ref[...])
    _post(xs, ys, mods, g2_ref[...], rwh_ref[...], rwl_ref[...], rb_ref[...], *outs)


def _pool_layer(xu, tinfo, mods, g1, pool_w, pool_scale, g2, rwh, rwl, rb):
    rows, d = xu.shape
    n_steps = tinfo[0].shape[0]
    n_e = rwh.shape[0]
    gc = pool_w.shape[-1]
    rpb = TM // SEG_ALIGN
    sub = SUB_TILES
    out_specs, out_shapes = _post_specs(n_steps, d, n_e, sub)
    grid_spec = pltpu.PrefetchScalarGridSpec(
        num_scalar_prefetch=4,
        grid=(n_steps // sub,),
        in_specs=[
            pl.BlockSpec((TM, d), lambda i, src, mrow, hp, hn, u=u: (src[sub * i + u], 0)) for u in range(sub)
        ] + [
            pl.BlockSpec((SEG_ALIGN, d),
                         lambda i, src, mrow, hp, hn: (jnp.maximum(src[sub * i] * rpb - 1, 0), 0)),
            pl.BlockSpec((SEG_ALIGN, d),
                         lambda i, src, mrow, hp, hn: (jnp.minimum((src[sub * i + sub - 1] + 1) * rpb,
                                                                   rows // SEG_ALIGN - 1), 0)),
        ] + _mod_specs(sub, d) + [
            _const_spec((1, d)),
            _const_spec((len(POOL_WINDOWS), gc, gc)),
            _const_spec((1, d)),
            _const_spec((1, d)),
            _const_spec((n_e, d)),
            _const_spec((n_e, d)),
            _const_spec((n_e, 1)),
        ],
        out_specs=out_specs,
        scratch_shapes=[pltpu.VMEM((sub, TM + 2 * SEG_ALIGN, d), F32)],
    )
    return pl.pallas_call(
        functools.partial(_pool_kernel, sub=sub), grid_spec=grid_spec, out_shape=out_shapes,
        compiler_params=_cparams(("arbitrary",)), name="pool_mixer",
    )(*tinfo, *([xu] * (sub + 2)), *([mods] * sub), g1, pool_w.astype(BF16), pool_scale, g2, rwh, rwl, rb)


def _ssm_kernel(tile_of, x_ref, mod_ref, g1_ref, are_ref, aim_ref, bb_ref, cb_ref, y_ref,
                u_scr, y_scr, h_scr, *xs_scrs, n_ctx_tiles):
    dr = pl.program_id(0)
    i = pl.program_id(1)
    nb, tt, d = x_ref.shape
    n_sets = bb_ref.shape[1]
    sw = bb_ref.shape[3]
    hw = sw // 2
    uw = bb_ref.shape[2]
    is_ctx = tile_of[dr * pl.num_programs(1) + i] < n_ctx_tiles

    @pl.when(i == 0)
    def _():
        h_scr[...] = jnp.zeros_like(h_scr)

    g1 = g1_ref[...]
    for b in range(nb):
        shift = jnp.where(is_ctx, mod_ref[nb, 0:1, :], mod_ref[b, 0:1, :])
        scale = jnp.where(is_ctx, mod_ref[nb, 1:2, :], mod_ref[b, 1:2, :])
        hb = _rms(x_ref[b], g1) * (1.0 + scale) + shift
        for j in range(n_sets):
            u_scr[j, pl.ds(b, tt, stride=nb), :] = hb[:, j * uw:(j + 1) * uw]
    for j in range(n_sets):
        xs_scrs[j][...] = _dot(u_scr[j].astype(BF16), bb_ref[0, j])
    for j in range(n_sets):
        xs = xs_scrs[j]
        ar = jnp.broadcast_to(are_ref[0, j:j + 1, :], (nb, hw))
        ai = jnp.broadcast_to(aim_ref[0, j:j + 1, :], (nb, hw))
        hr = h_scr[:, j * sw:j * sw + hw]
        hi = h_scr[:, j * sw + hw:(j + 1) * sw]
        for step in range(tt):
            t = step + dr * (tt - 1 - 2 * step)
            r0 = pl.multiple_of(t * nb, nb)
            nhr = ar * hr - ai * hi + xs[pl.ds(r0, nb), 0:hw]
            nhi = ar * hi + ai * hr + xs[pl.ds(r0, nb), hw:sw]
            xs[pl.ds(r0, nb), 0:hw] = nhr
            xs[pl.ds(r0, nb), hw:sw] = nhi
            hr, hi = nhr, nhi
        h_scr[:, j * sw:j * sw + hw] = hr
        h_scr[:, j * sw + hw:(j + 1) * sw] = hi
        y_scr[j] = _dot(xs[...].astype(BF16), cb_ref[0, j])
    for b in range(nb):
        for j in range(n_sets):
            y_ref[0, b, :, j * uw:(j + 1) * uw] = y_scr[j, pl.ds(b, tt, stride=nb), :]


def _ssm_scan(x3, mods, g1, a_re, a_im, bblk, cblk, n_ctx):
    nb, s, d = x3.shape
    tt = SSM_TT
    nt = s // tt
    nct = n_ctx // tt
    fwd = np.arange(nt)
    bwd = np.concatenate([np.arange(nct)[::-1], np.arange(nct, nt)[::-1]])
    tile_of = jnp.asarray(np.concatenate([fwd, bwd]), jnp.int32)
    n_sets, uw, sw = bblk.shape[1:]
    grid_spec = pltpu.PrefetchScalarGridSpec(
        num_scalar_prefetch=1,
        grid=(2, nt),
        in_specs=[
            pl.BlockSpec((nb, tt, d), lambda dr, i, to: (0, to[dr * nt + i], 0)),
            pl.BlockSpec(mods.shape, lambda dr, i, to: (0, 0, 0)),
            pl.BlockSpec((1, d), lambda dr, i, to: (0, 0)),
            pl.BlockSpec((1, n_sets, sw // 2), lambda dr, i, to: (dr, 0, 0)),
            pl.BlockSpec((1, n_sets, sw // 2), lambda dr, i, to: (dr, 0, 0)),
            pl.BlockSpec((1, n_sets, uw, sw), lambda dr, i, to: (dr, 0, 0, 0)),
            pl.BlockSpec((1, n_sets, sw, uw), lambda dr, i, to: (dr, 0, 0, 0)),
        ],
        out_specs=pl.BlockSpec((1, nb, tt, d), lambda dr, i, to: (dr, 0, to[dr * nt + i], 0)),
        scratch_shapes=[
            pltpu.VMEM((n_sets, tt * nb, uw), F32),
            pltpu.VMEM((n_sets, tt * nb, uw), F32),
            pltpu.VMEM((nb, n_sets * sw), F32),
        ] + [pltpu.VMEM((tt * nb, sw), F32) for _ in range(n_sets)],
    )
    return pl.pallas_call(
        functools.partial(_ssm_kernel, n_ctx_tiles=nct), grid_spec=grid_spec,
        out_shape=jax.ShapeDtypeStruct((2, nb, s, d), F32),
        compiler_params=_cparams(("arbitrary", "arbitrary")), name="s5_scan",
    )(tile_of, x3, mods, g1, a_re, a_im, bblk, cblk)


def _ssm_params(lam_re, lam_im, log_dt, b_re, b_im, c_re, c_im):
    g, p = lam_re.shape[1:]
    h = b_re.shape[-1]
    ns = g // SSM_SET
    eye = jnp.eye(SSM_SET, dtype=F32)
    outs = []
    for dr in range(2):
        lr, li = lam_re[dr].astype(F32), lam_im[dr].astype(F32)
        br, bi = b_re[dr].astype(F32), b_im[dr].astype(F32)
        dt = jnp.exp(log_dt[dr].astype(F32))[:, None]
        zr, zi = lr * dt, li * dt
        mag = jnp.exp(zr)
        ar, ai = mag * jnp.cos(zi), mag * jnp.sin(zi)
        den = lr * lr + li * li
        cr = ((ar - 1.0) * lr + ai * li) / den
        ci = (ai * lr - (ar - 1.0) * li) / den
        bbr = cr[..., None] * br - ci[..., None] * bi
        bbi = cr[..., None] * bi + ci[..., None] * br

        def blk_b(w):
            w = jnp.transpose(w, (0, 2, 1)).reshape(ns, SSM_SET, h, p)
            return jnp.einsum('ab,jahp->jahbp', eye, w).reshape(ns, SSM_SET * h, SSM_SET * p)

        def blk_c(w):
            w = jnp.transpose(w.reshape(ns, SSM_SET, h, p), (0, 1, 3, 2))
            return jnp.einsum('ab,japh->japbh', eye, w).reshape(ns, SSM_SET * p, SSM_SET * h)

        bblk = jnp.concatenate([blk_b(bbr), blk_b(bbi)], axis=2)
        cblk = jnp.concatenate([blk_c(c_re[dr].astype(F32)), -blk_c(c_im[dr].astype(F32))], axis=1)
        outs.append((ar.reshape(ns, SSM_SET * p), ai.reshape(ns, SSM_SET * p), bblk, cblk))
    a_re = jnp.stack([o[0] for o in outs])
    a_im = jnp.stack([o[1] for o in outs])
    bblk = jnp.stack([o[2] for o in outs]).astype(BF16)
    cblk = jnp.stack([o[3] for o in outs]).astype(BF16)
    return a_re, a_im, bblk, cblk


def _glu_kernel(src, mrow, x_ref, yf_ref, yb_ref, *refs, sub):
    mod_refs = refs[:sub]
    g1_ref, dsk_ref, gw_ref, gb_ref, g2_ref, rwh_ref, rwl_ref, rb_ref = refs[sub:sub + 8]
    outs = refs[sub + 8:]
    d = x_ref.shape[1]
    xs, ys, mods = [], [], []
    for u in range(sub):
        rows = pl.ds(u * TM, TM)
        mod = mod_refs[u][0]
        x = x_ref[rows, :]
        h = _rms(x, g1_ref[...]) * (1.0 + mod[1:2]) + mod[0:1]
        y = yf_ref[0, rows, :] + yb_ref[0, rows, :] + dsk_ref[...] * h
        gl = 0.5 * y * (1.0 + jnp.tanh(math.sqrt(2.0 / math.pi) * (y + 0.044715 * (y * y * y))))
        z = _dot(gl.astype(BF16), gw_ref[...]) + gb_ref[...]
        xs.append(x)
        ys.append(z[:, :d] * _sigmoid(z[:, d:]))
        mods.append(mod)
    _post(xs, ys, mods, g2_ref[...], rwh_ref[...], rwl_ref[...], rb_ref[...], *outs)


def _glu_layer(xu, y2, tinfo, mods, g1, d_skip, glu_w, glu_b, g2, rwh, rwl, rb):
    rows, d = xu.shape
    n_steps = rows // TM
    n_e = rwh.shape[0]
    sub = SUB_TILES
    out_specs, out_shapes = _post_specs(n_steps, d, n_e, sub)
    grid_spec = pltpu.PrefetchScalarGridSpec(
        num_scalar_prefetch=2,
        grid=(n_steps // sub,),
        in_specs=[
            pl.BlockSpec((sub * TM, d), lambda i, src, mrow: (i, 0)),
            pl.BlockSpec((1, sub * TM, d), lambda i, src, mrow: (0, i, 0)),
            pl.BlockSpec((1, sub * TM, d), lambda i, src, mrow: (1, i, 0)),
        ] + _mod_specs(sub, d) + [
            _const_spec((1, d)),
            _const_spec((1, d)),
            _const_spec((d, 2 * d)),
            _const_spec((1, 2 * d)),
            _const_spec((1, d)),
            _const_spec((n_e, d)),
            _const_spec((n_e, d)),
            _const_spec((n_e, 1)),
        ],
        out_specs=out_specs,
    )
    return pl.pallas_call(
        functools.partial(_glu_kernel, sub=sub), grid_spec=grid_spec, out_shape=out_shapes,
        compiler_params=_cparams(("arbitrary",)), name="s5_glu",
    )(tinfo[0], tinfo[1], xu, y2, y2, *([mods] * sub), g1, d_skip, glu_w.astype(BF16), glu_b, g2, rwh, rwl, rb)


def _rope(v, cos, sin_signed, first_half):
    partner = jnp.where(first_half, pltpu.roll(v, HEAD_DIM - HEAD_DIM // 4, axis=1),
                        pltpu.roll(v, HEAD_DIM // 4, axis=1))
    return v * cos + partner * sin_signed


def _qkv_kernel(src, mrow, trow, x_ref, mod_ref, g1_ref, w_ref, qg_ref, kg_ref, cos_ref, sin_ref,
                q_ref, k_ref, v_ref):
    d = x_ref.shape[1]
    kvw = k_ref.shape[1]
    mod = mod_ref[0]
    h = _rms(x_ref[...], g1_ref[...]) * (1.0 + mod[1:2]) + mod[0:1]
    z = _dot(h.astype(BF16), w_ref[...])
    cos = cos_ref[...]
    sin = sin_ref[...]
    lane = lax.broadcasted_iota(jnp.int32, (TM, HEAD_DIM), 1)
    first_half = (lane % (HEAD_DIM // 2)) < (HEAD_DIM // 4)
    q_scale = HEAD_DIM ** -0.5
    for hd in range(d // HEAD_DIM):
        zh = z[:, hd * HEAD_DIM:(hd + 1) * HEAD_DIM]
        zh = _rope(_rms(zh, qg_ref[...]), cos, sin, first_half) * q_scale
        q_ref[:, hd * HEAD_DIM:(hd + 1) * HEAD_DIM] = zh.astype(BF16)
    for hd in range(kvw // HEAD_DIM):
        zh = z[:, d + hd * HEAD_DIM:d + (hd + 1) * HEAD_DIM]
        zh = _rope(_rms(zh, kg_ref[...]), cos, sin, first_half)
        k_ref[:, hd * HEAD_DIM:(hd + 1) * HEAD_DIM] = zh.astype(BF16)
    v_ref[...] = z[:, d + kvw:].astype(BF16)


def _attn_kernel(q_ref, k_ref, v_ref, o_ref, *, n_ctx):
    qt = pl.program_id(2)

    def attend(n_keys):
        for hd in range(ATTN_HEADS):
            cols = slice(hd * HEAD_DIM, (hd + 1) * HEAD_DIM)
            s = _dot_nt(q_ref[:, cols], k_ref[0:n_keys, :])
            m = jnp.max(s, axis=-1, keepdims=True)
            p = jnp.exp(s - m)
            den = jnp.sum(p, axis=-1, keepdims=True)
            o_ref[:, cols] = (_dot(p.astype(BF16), v_ref[0:n_keys, :]) / den).astype(BF16)

    @pl.when(qt < n_ctx // TM)
    def _():
        attend(n_ctx)

    @pl.when(qt >= n_ctx // TM)
    def _():
        attend(k_ref.shape[0])


def _wo_kernel(src, mrow, x_ref, o_ref, *refs, sub):
    mod_refs = refs[:sub]
    wo_ref, g2_ref, rwh_ref, rwl_ref, rb_ref = refs[sub:sub + 5]
    outs = refs[sub + 5:]
    tiles = [pl.ds(u * TM, TM) for u in range(sub)]
    ys = [_dot(o_ref[rows, :], wo_ref[...]) for rows in tiles]
    _post([x_ref[rows, :] for rows in tiles], ys, [m[0] for m in mod_refs],
          g2_ref[...], rwh_ref[...], rwl_ref[...], rb_ref[...], *outs)


def _rope_tables(n_ctx, n_lat):
    rows = n_lat // GRID_W
    row = jnp.repeat(jnp.arange(rows), GRID_W)
    col = jnp.tile(jnp.arange(GRID_W), rows)
    pos = jnp.stack([row, col], axis=-1).astype(F32)
    inv_freq = ROPE_THETA ** (-jnp.arange(ROPE_F, dtype=F32) / ROPE_F)
    ang = pos[:, :, None] * inv_freq
    cos, sin = jnp.cos(ang), jnp.sin(ang)
    cos_t = jnp.concatenate([cos, cos], axis=-1).reshape(n_lat, HEAD_DIM)
    sin_t = jnp.concatenate([-sin, sin], axis=-1).reshape(n_lat, HEAD_DIM)
    cos_t = jnp.concatenate([jnp.ones((n_ctx, HEAD_DIM), F32), cos_t], axis=0)
    sin_t = jnp.concatenate([jnp.zeros((n_ctx, HEAD_DIM), F32), sin_t], axis=0)
    return cos_t, sin_t


def _attn_layer(xu, tinfo, trow, mods, g1, wqkv, q_g, k_g, wo, g2, rwh, rwl, rb, nb, n_ctx):
    rows, d = xu.shape
    s = rows // nb
    n_steps = rows // TM
    n_e = rwh.shape[0]
    kvw = N_KV_HEADS * HEAD_DIM
    n_heads = d // HEAD_DIM
    rep = n_heads // N_KV_HEADS
    cos_t, sin_t = _rope_tables(n_ctx, s - n_ctx)
    grid_spec = pltpu.PrefetchScalarGridSpec(
        num_scalar_prefetch=3,
        grid=(n_steps,),
        in_specs=[
            pl.BlockSpec((TM, d), lambda i, src, mrow, tr: (i, 0)),
            pl.BlockSpec((1, 6, d), lambda i, src, mrow, tr: (mrow[i], 0, 0)),
            _const_spec((1, d)),
            _const_spec((d, d + 2 * kvw)),
            _const_spec((1, HEAD_DIM)),
            _const_spec((1, HEAD_DIM)),
            pl.BlockSpec((TM, HEAD_DIM), lambda i, src, mrow, tr: (tr[i], 0)),
            pl.BlockSpec((TM, HEAD_DIM), lambda i, src, mrow, tr: (tr[i], 0)),
        ],
        out_specs=[
            pl.BlockSpec((TM, d), lambda i, *_: (i, 0)),
            pl.BlockSpec((TM, kvw), lambda i, *_: (i, 0)),
            pl.BlockSpec((TM, kvw), lambda i, *_: (i, 0)),
        ],
    )
    q, k, v = pl.pallas_call(
        _qkv_kernel, grid_spec=grid_spec,
        out_shape=[jax.ShapeDtypeStruct((rows, d), BF16),
                   jax.ShapeDtypeStruct((rows, kvw), BF16),
                   jax.ShapeDtypeStruct((rows, kvw), BF16)],
        compiler_params=_cparams(("arbitrary",)), name="attn_qkv",
    )(tinfo[0], tinfo[1], trow, xu, mods, g1, wqkv.astype(BF16), q_g, k_g, cos_t, sin_t)

    tpb = s // TM
    o = pl.pallas_call(
        functools.partial(_attn_kernel, n_ctx=n_ctx),
        grid=(nb, n_heads // ATTN_HEADS, tpb),
        in_specs=[
            pl.BlockSpec((TM, ATTN_HEADS * HEAD_DIM), lambda b, hp, t: (b * tpb + t, hp)),
            pl.BlockSpec((s, HEAD_DIM), lambda b, hp, t: (b, hp * ATTN_HEADS // rep)),
            pl.BlockSpec((s, HEAD_DIM), lambda b, hp, t: (b, hp * ATTN_HEADS // rep)),
        ],
        out_specs=pl.BlockSpec((TM, ATTN_HEADS * HEAD_DIM), lambda b, hp, t: (b * tpb + t, hp)),
        out_shape=jax.ShapeDtypeStruct((rows, d), BF16),
        compiler_params=_cparams(("arbitrary", "arbitrary", "arbitrary")), name="attn_core",
    )(q, k, v)

    sub = SUB_TILES
    out_specs, out_shapes = _post_specs(n_steps, d, n_e, sub)
    grid_spec = pltpu.PrefetchScalarGridSpec(
        num_scalar_prefetch=2,
        grid=(n_steps // sub,),
        in_specs=[
            pl.BlockSpec((sub * TM, d), lambda i, src, mrow: (i, 0)),
            pl.BlockSpec((sub * TM, d), lambda i, src, mrow: (i, 0)),
        ] + _mod_specs(sub, d) + [
            _const_spec((d, d)),
            _const_spec((1, d)),
            _const_spec((n_e, d)),
            _const_spec((n_e, d)),
            _const_spec((n_e, 1)),
        ],
        out_specs=out_specs,
    )
    return pl.pallas_call(
        functools.partial(_wo_kernel, sub=sub), grid_spec=grid_spec, out_shape=out_shapes,
        compiler_params=_cparams(("arbitrary",)), name="attn_out",
    )(tinfo[0], tinfo[1], xu, o, *([mods] * sub), wo.astype(BF16), g2, rwh, rwl, rb)


def _loc_rows(n_e):
    return -(-(TM * TOP_K + (SEG_ALIGN - 1) * n_e) // 256) * 256


def _pack_pairs(v, rounded):
    half = v.shape[1] // 2
    lo, hi = v[:, :half], v[:, half:]
    if not rounded:
        lo, hi = lo.astype(BF16).astype(F32), hi.astype(BF16).astype(F32)
    lo = lax.bitcast_convert_type(lo, jnp.uint32)
    hi = lax.bitcast_convert_type(hi, jnp.uint32)
    return (hi & jnp.uint32(0xFFFF0000)) | (lo >> 16)


def _unpack_pairs(p):
    lo = lax.bitcast_convert_type(p << 16, F32)
    hi = lax.bitcast_convert_type(p & jnp.uint32(0xFFFF0000), F32)
    return jnp.concatenate([lo, hi], axis=1).astype(BF16)


def _start_segments(t, n_e, loc, len8, dst, make_copy):
    for e in range(n_e):
        n = pl.multiple_of(len8[t * n_e + e], SEG_ALIGN)

        @pl.when(n > 0)
        def _(e=e, n=n):
            make_copy(pl.multiple_of(loc[t * n_e + e], SEG_ALIGN),
                      pl.multiple_of(dst[t * n_e + e], SEG_ALIGN), n).start()


def _dispatch_kernel(loc, len8, dst, tot, padoff, padlen, nused, h2_ref, route_ref, xs_ref,
                     buf, zbuf, sems, zsem, *, n_e, nb_max):
    i = pl.program_id(0)
    n_steps = pl.num_programs(0)
    slot = i % 2
    sub, lrows = buf.shape[1:3]
    iota_p = lax.broadcasted_iota(jnp.int16, (lrows, TM), 0)
    for u in range(sub):
        pos = route_ref[u, 0:TOP_K, :].astype(jnp.int32).astype(jnp.int16)
        perm = jnp.zeros((lrows, TM), BF16)
        for k in range(TOP_K):
            perm = jnp.where(iota_p == pos[k:k + 1], jnp.ones((), BF16), perm)
        buf[slot, u] = _pack_pairs(_dot(perm, h2_ref[pl.ds(u * TM, TM), :]), rounded=True)

    def seg_copy(s, u):
        def make(a, g, n):
            return pltpu.make_async_copy(buf.at[s, u, pl.ds(a, n)], xs_ref.at[pl.ds(g, n)], sems.at[s, u])
        return make

    def wait_tiles(step, s):
        for u in range(sub):
            seg_copy(s, u)(0, 0, pl.multiple_of(tot[sub * step + u], SEG_ALIGN)).wait()

    for u in range(sub):
        _start_segments(sub * i + u, n_e, loc, len8, dst, seg_copy(slot, u))

    @pl.when(i > 0)
    def _():
        wait_tiles(i - 1, 1 - slot)

    @pl.when(i == n_steps - 1)
    def _():
        wait_tiles(i, slot)

    @pl.when(i == n_steps - 1)
    def _():
        zbuf[...] = jnp.zeros_like(zbuf)

        def zero_rows(g, n):
            return pltpu.make_async_copy(zbuf.at[pl.ds(0, n)], xs_ref.at[pl.ds(g, n)], zsem)

        def pad(e, total):
            n = pl.multiple_of(padlen[e], SEG_ALIGN)

            @pl.when(n > 0)
            def _():
                zero_rows(pl.multiple_of(padoff[e], SEG_ALIGN), n).start()
            return total + n

        total = lax.fori_loop(0, n_e, pad, 0)

        def blank(b, carry):
            zero_rows(pl.multiple_of(b * BLK, BLK), BLK).start()
            return carry

        lax.fori_loop(nused[0], nb_max, blank, 0)
        total = pl.multiple_of(total + (nb_max - nused[0]) * BLK, SEG_ALIGN)

        @pl.when(total > 0)
        def _():
            pltpu.make_async_copy(xs_ref.at[pl.ds(0, total)], xs_ref.at[pl.ds(0, total)], zsem).wait()


def _expert_kernel(blk, blk_e, valid, nused, xs_ref, w1_ref, b1_ref, w2_ref, b2_ref, ys_ref, w1b, w2b):
    i = pl.program_id(0)
    f = w2b.shape[0]

    @pl.when(i >= nused[0])
    def _():
        ys_ref[...] = jnp.zeros_like(ys_ref)

    @pl.when(i < nused[0])
    def _():
        first = jnp.logical_or(i == 0, blk_e[i] != blk_e[jnp.maximum(i - 1, 0)])

        @pl.when(first)
        def _():
            w1b[...] = w1_ref[0, 0].astype(BF16)
            w2b[...] = w2_ref[0, 0].astype(BF16)

        for c in range(BLK // EXPERT_CHUNK):
            r0 = c * EXPERT_CHUNK
            row = r0 + lax.broadcasted_iota(jnp.int32, (EXPERT_CHUNK, 1), 0)
            x = _unpack_pairs(jnp.where(row < valid[i], xs_ref[r0:r0 + EXPERT_CHUNK, :], jnp.uint32(0)))
            z = _dot(x, w1b[...]) + b1_ref[0, 0]
            glu = jnp.minimum(z[:, :f], SWIGLU_LIMIT)
            lin = jnp.clip(z[:, f:], -SWIGLU_LIMIT, SWIGLU_LIMIT)
            act = glu * _sigmoid(SWIGLU_ALPHA * glu) * (lin + 1.0)
            y = _dot(act.astype(BF16), w2b[...]) + b2_ref[0, 0]
            ys_ref[r0:r0 + EXPERT_CHUNK, :] = _pack_pairs(y, rounded=False)


def _combine_kernel(loc, len8, dst, tot, mrow, ys_ref, route_ref, x1_ref, *refs, n_e):
    buf, sems = refs[-2:]
    out_ref = refs[-3]
    mod_refs = refs[:-3]
    i = pl.program_id(0)
    n_steps = pl.num_programs(0)
    slot = i % 2
    sub, lrows = buf.shape[1:3]

    def seg_copy(s, u):
        def make(a, g, n):
            return pltpu.make_async_copy(ys_ref.at[pl.ds(g, n)], buf.at[s, u, pl.ds(a, n)], sems.at[s, u])
        return make

    @pl.when(i == 0)
    def _():
        for u in range(sub):
            _start_segments(u, n_e, loc, len8, dst, seg_copy(0, u))

    @pl.when(i + 1 < n_steps)
    def _():
        for u in range(sub):
            _start_segments(sub * (i + 1) + u, n_e, loc, len8, dst, seg_copy(1 - slot, u))

    iota_p = lax.broadcasted_iota(jnp.int16, (lrows, TM), 0)
    row = lax.broadcasted_iota(jnp.int32, (lrows, 1), 0)
    totals = [pl.multiple_of(tot[sub * i + u], SEG_ALIGN) for u in range(sub)]
    for u in range(sub):
        seg_copy(slot, u)(0, 0, totals[u]).wait()
    for u in range(sub):
        total = totals[u]
        pos = route_ref[u, 0:TOP_K, :].astype(jnp.int32).astype(jnp.int16)
        gate = route_ref[u, TOP_K:2 * TOP_K, :].astype(BF16)
        gt = jnp.zeros((lrows, TM), BF16)
        for k in range(TOP_K):
            gt = jnp.where(iota_p == pos[k:k + 1], gate[k:k + 1], gt)
        ysl = _unpack_pairs(jnp.where(row < total, buf[slot, u], jnp.uint32(0)))
        rows = pl.ds(u * TM, TM)
        out_ref[rows, :] = x1_ref[rows, :] + mod_refs[u][0][5:6] * _dot_tn(gt, ysl)


def _moe(x1, h2, route, cnt, mods, mrow, layer, w1, b1, w2, b2):
    rows, d = x1.shape
    n_tiles = rows // TM
    depth, n_e, _, f2 = w1.shape
    lrows = _loc_rows(n_e)
    nb_max = (rows * TOP_K + (SEG_ALIGN - 1) * n_tiles * n_e) // BLK + n_e
    cap = nb_max * BLK
    dp = d // 2

    i32 = lambda v: v.astype(jnp.int32)
    cnt = i32(cnt.reshape(n_tiles, n_e))
    c8 = (cnt + SEG_ALIGN - 1) // SEG_ALIGN * SEG_ALIGN
    loc = jnp.cumsum(c8, axis=1) - c8
    tot_tile = jnp.sum(c8, axis=1)
    tot = jnp.sum(c8, axis=0)
    nblk = (tot + BLK - 1) // BLK
    blk_end = jnp.cumsum(nblk)
    blk_start = blk_end - nblk
    dst = (blk_start * BLK)[None, :] + jnp.cumsum(c8, axis=0) - c8
    nused = blk_end[-1]
    bid = jnp.arange(nb_max, dtype=jnp.int32)
    bidc = jnp.minimum(bid, nused - 1)
    blk_e = jnp.minimum(jnp.sum(i32(bidc[:, None] >= blk_end[None, :]), axis=1), n_e - 1)
    valid = jnp.clip(tot[blk_e] - (bidc - blk_start[blk_e]) * BLK, 0, BLK)
    padoff = blk_start * BLK + tot
    padlen = nblk * BLK - tot
    loc, len8, dst = i32(loc.reshape(-1)), i32(c8.reshape(-1)), i32(dst.reshape(-1))
    nused = i32(nused.reshape(1))
    sub = SUB_TILES
    row_buf = pltpu.VMEM((2, sub, lrows, dp), jnp.uint32)

    xs = pl.pallas_call(
        functools.partial(_dispatch_kernel, n_e=n_e, nb_max=nb_max),
        grid_spec=pltpu.PrefetchScalarGridSpec(
            num_scalar_prefetch=7,
            grid=(n_tiles // sub,),
            in_specs=[
                pl.BlockSpec((sub * TM, d), lambda i, *_: (i, 0)),
                pl.BlockSpec((sub, 2 * TOP_K, TM), lambda i, *_: (i, 0, 0)),
            ],
            out_specs=pl.BlockSpec(memory_space=pl.ANY),
            scratch_shapes=[row_buf, pltpu.VMEM((BLK, dp), jnp.uint32),
                            pltpu.SemaphoreType.DMA((2, sub)), pltpu.SemaphoreType.DMA(())],
        ),
        out_shape=jax.ShapeDtypeStruct((cap, dp), jnp.uint32),
        compiler_params=_cparams(("arbitrary",)), name="moe_dispatch",
    )(loc, len8, dst, i32(tot_tile), i32(padoff), i32(padlen), nused, h2, route)

    ys = pl.pallas_call(
        _expert_kernel,
        grid_spec=pltpu.PrefetchScalarGridSpec(
            num_scalar_prefetch=4,
            grid=(nb_max,),
            in_specs=[
                pl.BlockSpec((BLK, dp), lambda i, blk, be, va, nu: (blk[i], 0)),
                pl.BlockSpec((1, 1, d, f2), lambda i, blk, be, va, nu: (layer, be[i], 0, 0)),
                pl.BlockSpec((1, 1, 1, f2), lambda i, blk, be, va, nu: (layer, be[i], 0, 0)),
                pl.BlockSpec((1, 1, f2 // 2, d), lambda i, blk, be, va, nu: (layer, be[i], 0, 0)),
                pl.BlockSpec((1, 1, 1, d), lambda i, blk, be, va, nu: (layer, be[i], 0, 0)),
            ],
            out_specs=pl.BlockSpec((BLK, dp), lambda i, blk, be, va, nu: (i, 0)),
            scratch_shapes=[pltpu.VMEM((d, f2), BF16), pltpu.VMEM((f2 // 2, d), BF16)],
        ),
        out_shape=jax.ShapeDtypeStruct((cap, dp), jnp.uint32),
        compiler_params=_cparams(("arbitrary",)), name="moe_experts",
    )(i32(bidc), i32(blk_e), i32(valid), nused, xs, w1, b1.reshape(depth, n_e, 1, f2), w2,
      b2.reshape(depth, n_e, 1, d))

    return pl.pallas_call(
        functools.partial(_combine_kernel, n_e=n_e),
        grid_spec=pltpu.PrefetchScalarGridSpec(
            num_scalar_prefetch=5,
            grid=(n_tiles // sub,),
            in_specs=[
                pl.BlockSpec(memory_space=pl.ANY),
                pl.BlockSpec((sub, 2 * TOP_K, TM), lambda i, *_: (i, 0, 0)),
                pl.BlockSpec((sub * TM, d), lambda i, *_: (i, 0)),
            ] + [pl.BlockSpec((1, 6, d), lambda i, lo, le, ds, to, mr, u=u: (mr[sub * i + u], 0, 0))
                 for u in range(sub)],
            out_specs=pl.BlockSpec((sub * TM, d), lambda i, *_: (i, 0)),
            scratch_shapes=[row_buf, pltpu.SemaphoreType.DMA((2, sub))],
        ),
        out_shape=jax.ShapeDtypeStruct((rows, d), F32),
        compiler_params=_cparams(("arbitrary",)), name="moe_combine",
    )(loc, len8, dst, i32(tot_tile), mrow, ys, route, x1, *([mods] * sub))


def _tile_info(nb, n_ctx, n_lat, latent_only):
    tpb = (n_ctx + n_lat) // TM
    ct = n_ctx // TM
    src, mrow, hp, hn, trow = [], [], [], [], []
    for b in range(nb):
        for j in range(ct if latent_only else 0, tpb):
            is_ctx = j < ct
            src.append(b * tpb + j)
            mrow.append(nb if is_ctx else b)
            hp.append(0 if j in (0, ct) else 1)
            hn.append(0 if j in (ct - 1, tpb - 1) else 1)
            trow.append(j)
    mk = lambda v: jnp.asarray(np.asarray(v, np.int32))
    return (mk(src), mk(mrow), mk(hp), mk(hn)), mk(trow)


def kernel(x, c, ctx, c_ctx, ada_w, ada_b, norm1_g, norm2_g, pool_w, pool_scale, ssm_lam_re, ssm_lam_im,
           ssm_log_dt, ssm_b_re, ssm_b_im, ssm_c_re, ssm_c_im, ssm_d, ssm_glu_w, ssm_glu_b, attn_wqkv,
           attn_q_g, attn_k_g, attn_wo, router_w, router_b, moe_w1, moe_b1, moe_w2, moe_b2):
    nb, n_lat, d = x.shape
    n_ctx = ctx.shape[1]
    depth = ada_w.shape[0]
    n_e = router_w.shape[-1]
    assert nb == 8 and n_ctx % TM == 0 and n_lat % TM == 0 and n_lat % GRID_W == 0
    s = n_ctx + n_lat

    c16 = jnp.concatenate([c, c_ctx[None, :], jnp.zeros((16 - nb - 1, d), F32)], axis=0)
    mods_all = _ada_mods(c16, ada_w, ada_b).reshape(depth, 16, 6, d)

    xu = jnp.concatenate([ctx, x], axis=1).reshape(nb * s, d)
    tinfo_u, trow_u = _tile_info(nb, n_ctx, n_lat, latent_only=False)
    tinfo_l, _ = _tile_info(nb, n_ctx, n_lat, latent_only=True)

    for i in range(depth):
        kind, j = i % 3, i // 3
        last = i == depth - 1
        mods = mods_all[i]
        g1 = norm1_g[i].reshape(1, d)
        g2 = norm2_g[i].reshape(1, d)
        rwt = router_w[i].T
        rwh = rwt.astype(BF16)
        rwl = (rwt - rwh.astype(F32)).astype(BF16)
        rb = router_b[i].reshape(n_e, 1)
        if kind == 0:
            tinfo = tinfo_l if last else tinfo_u
            x1, h2, route, cnt = _pool_layer(xu, tinfo, mods, g1, pool_w[j], pool_scale[j].reshape(1, d),
                                             g2, rwh, rwl, rb)
            mrow = tinfo[1]
        elif kind == 1:
            assert not last
            a_re, a_im, bblk, cblk = _ssm_params(ssm_lam_re[j], ssm_lam_im[j], ssm_log_dt[j], ssm_b_re[j],
                                                 ssm_b_im[j], ssm_c_re[j], ssm_c_im[j])
            y2 = _ssm_scan(xu.reshape(nb, s, d), mods, g1, a_re, a_im, bblk, cblk, n_ctx)
            x1, h2, route, cnt = _glu_layer(xu, y2.reshape(2, nb * s, d), tinfo_u, mods, g1,
                                            ssm_d[j].reshape(1, d), ssm_glu_w[j], ssm_glu_b[j].reshape(1, 2 * d),
                                            g2, rwh, rwl, rb)
            mrow = tinfo_u[1]
        else:
            assert not last
            x1, h2, route, cnt = _attn_layer(xu, tinfo_u, trow_u, mods, g1, attn_wqkv[j],
                                             attn_q_g[j].reshape(1, HEAD_DIM), attn_k_g[j].reshape(1, HEAD_DIM),
                                             attn_wo[j], g2, rwh, rwl, rb, nb, n_ctx)
            mrow = tinfo_u[1]
        xu = _moe(x1, h2, route, cnt, mods, mrow, i, moe_w1, moe_b1, moe_w2, moe_b2)
    if xu.shape[0] == nb * n_lat:
        return xu.reshape(nb, n_lat, d)
    return xu.reshape(nb, s, d)[:, n_ctx:, :]
```

```python
import functools
import math

import numpy as np
import jax
import jax.numpy as jnp
from jax import lax
from jax.experimental import pallas as pl
from jax.experimental.pallas import tpu as pltpu

F32 = jnp.float32
BF16 = jnp.bfloat16

GRID_W = 64
NORM_EPS = 1e-6
POOL_WINDOWS = (2, 4, 8, 16)
SSM_H = 16
SSM_P = 64
SSM_SET = 8
SSM_TT = 32
HEAD_DIM = 128
N_KV_HEADS = 2
ATTN_HEADS = 4
ROPE_F = HEAD_DIM // 4
ROPE_THETA = 10000.0
TOP_K = 4
SWIGLU_ALPHA = 1.702
SWIGLU_LIMIT = 7.0

TM = 256
SUB_TILES = 2
BLK = 512
SUBLANES = 8
LANES = 128
SEG_ALIGN = 2
VMEM_LIMIT = 56 * 1024 * 1024


def _cparams(sem, vmem=VMEM_LIMIT):
    return pltpu.CompilerParams(dimension_semantics=sem, vmem_limit_bytes=vmem)


def _rms(x, g):
    return x * lax.rsqrt(jnp.mean(x * x, axis=-1, keepdims=True) + NORM_EPS) * g


def _sigmoid(x):
    return 1.0 / (1.0 + jnp.exp(-x))


def _dot(a, b):
    return jnp.dot(a, b, preferred_element_type=F32)


def _dot_nt(a, b):
    return lax.dot_general(a, b, (((1,), (1,)), ((), ())), preferred_element_type=F32)


def _dot_tn(a, b):
    return lax.dot_general(a, b, (((0,), (0,)), ((), ())), preferred_element_type=F32)


def _ada_kernel(c_ref, w_ref, b_ref, o_ref):
    c = c_ref[...]
    s = c * _sigmoid(c)
    o_ref[0] = jnp.dot(s, w_ref[0], preferred_element_type=F32,
                       precision=lax.Precision.HIGHEST) + b_ref[0]


def _ada_mods(c16, ada_w, ada_b):
    depth, d, six_d = ada_w.shape
    tn = d
    return pl.pallas_call(
        _ada_kernel,
        grid=(depth, six_d // tn),
        in_specs=[
            pl.BlockSpec((16, d), lambda l, j: (0, 0)),
            pl.BlockSpec((1, d, tn), lambda l, j: (l, 0, j)),
            pl.BlockSpec((1, 1, tn), lambda l, j: (l, 0, j)),
        ],
        out_specs=pl.BlockSpec((1, 16, tn), lambda l, j: (l, 0, j)),
        out_shape=jax.ShapeDtypeStruct((depth, 16, six_d), F32),
        compiler_params=_cparams(("arbitrary", "arbitrary")),
        name="ada_mods",
    )(c16, ada_w, ada_b.reshape(depth, 1, six_d))


def _post_logits(x, y, mod, g2, rwh, rwl, rb, x1_ref, h2_ref):
    x1 = x + mod[2:3] * y
    x1_ref[...] = x1
    h2 = _rms(x1, g2) * (1.0 + mod[4:5]) + mod[3:4]
    h2_ref[...] = h2.astype(BF16)
    hh = h2.astype(BF16)
    hl = (h2 - hh.astype(F32)).astype(BF16)
    return _dot_nt(rwh, hh) + _dot_nt(rwh, hl) + _dot_nt(rwl, hh) + rb


def _post_route(logits, route_ref, cnt_ref):
    n_e = logits.shape[0]
    iota_e = lax.broadcasted_iota(jnp.int32, (n_e, TM), 0)
    vals, onehots = [], []
    l = logits
    for _ in range(TOP_K):
        m = jnp.max(l, axis=0, keepdims=True)
        idx = jnp.min(jnp.where(l == m, iota_e, n_e), axis=0, keepdims=True)
        sel = iota_e == idx
        vals.append(m)
        onehots.append(sel)
        l = jnp.where(sel, -jnp.inf, l)
    ex = [jnp.exp(v - vals[0]) for v in vals]
    den = ex[0] + ex[1] + ex[2] + ex[3]
    gates = [e / den for e in ex]
    member = jnp.zeros((n_e, TM), F32)
    for sel in onehots:
        member = member + jnp.where(sel, 1.0, 0.0)
    r_i = lax.broadcasted_iota(jnp.int32, (TM, TM), 0)
    c_i = lax.broadcasted_iota(jnp.int32, (TM, TM), 1)
    upper = jnp.where(r_i < c_i, 1.0, 0.0).astype(BF16)
    cum = _dot(member.astype(BF16), upper)
    cnt = jnp.sum(member, axis=1, keepdims=True)
    cnt_ref[0] = cnt
    c8 = jnp.floor((cnt + (SEG_ALIGN - 1)) * (1.0 / SEG_ALIGN)) * SEG_ALIGN
    e_r = lax.broadcasted_iota(jnp.int32, (n_e, n_e), 0)
    e_c = lax.broadcasted_iota(jnp.int32, (n_e, n_e), 1)
    lower = jnp.where(e_c < e_r, 1.0, 0.0).astype(BF16)
    seg = _dot(lower, jnp.broadcast_to(c8, (n_e, TM)).astype(BF16))
    base = seg + cum
    rows = []
    for sel in onehots:
        rows.append(jnp.sum(jnp.where(sel, base, 0.0), axis=0, keepdims=True))
    route_ref[0] = jnp.concatenate(rows + gates, axis=0)


def _post_specs(n_tiles, d, n_e, sub=1):
    specs = [
        pl.BlockSpec((sub * TM, d), lambda i, *_: (i, 0)),
        pl.BlockSpec((sub * TM, d), lambda i, *_: (i, 0)),
        pl.BlockSpec((sub, 2 * TOP_K, TM), lambda i, *_: (i, 0, 0)),
        pl.BlockSpec((sub, n_e, 1), lambda i, *_: (i, 0, 0)),
    ]
    shapes = [
        jax.ShapeDtypeStruct((n_tiles * TM, d), F32),
        jax.ShapeDtypeStruct((n_tiles * TM, d), BF16),
        jax.ShapeDtypeStruct((n_tiles, 2 * TOP_K, TM), F32),
        jax.ShapeDtypeStruct((n_tiles, n_e, 1), F32),
    ]
    return specs, shapes


def _post(xs, ys, mods, g2, rwh, rwl, rb, x1_ref, h2_ref, route_ref, cnt_ref):
    logits = []
    for u, (x, y, mod) in enumerate(zip(xs, ys, mods)):
        rows = pl.ds(u * TM, TM)
        logits.append(_post_logits(x, y, mod, g2, rwh, rwl, rb, x1_ref.at[rows], h2_ref.at[rows]))
    for u, l in enumerate(logits):
        _post_route(l, route_ref.at[pl.ds(u, 1)], cnt_ref.at[pl.ds(u, 1)])


def _mod_specs(sub, d):
    return [pl.BlockSpec((1, 6, d), lambda i, src, mrow, *_, u=u: (mrow[sub * i + u], 0, 0))
            for u in range(sub)]


def _const_spec(shape):
    nd = len(shape)
    return pl.BlockSpec(shape, lambda i, *_: (0,) * nd)


def _pool_kernel(src, mrow, hp, hn, *refs, sub):
    x_refs = refs[:sub]
    xp_ref, xn_ref = refs[sub:sub + 2]
    mod_refs = refs[sub + 2:2 * sub + 2]
    g1_ref, pw_ref, ps_ref, g2_ref, rwh_ref, rwl_ref, rb_ref = refs[2 * sub + 2:2 * sub + 9]
    outs = refs[2 * sub + 9:2 * sub + 13]
    hh_scr = refs[2 * sub + 13]
    i = pl.program_id(0)
    d = x_refs[0].shape[1]
    gc = d // len(POOL_WINDOWS)
    halo = SUBLANES
    g1 = g1_ref[...]

    def pre(v, mod):
        return _rms(v, g1) * (1.0 + mod[1:2]) + mod[0:1]

    mods = [m[0] for m in mod_refs]
    xs = [r[...] for r in x_refs]
    hs = [pre(x, mod) for x, mod in zip(xs, mods)]
    row = lax.broadcasted_iota(jnp.int32, (TM, 1), 0)
    ys = []
    for u in range(sub):
        has_prev = hp[sub * i + u] > 0
        has_next = hn[sub * i + u] > 0
        before = pre(xp_ref[...], mods[0]) if u == 0 else hs[u - 1][TM - halo:, :]
        after = pre(xn_ref[...], mods[-1]) if u == sub - 1 else hs[u + 1][:halo, :]
        hh_scr[u, 0:halo, :] = jnp.where(has_prev, before, 0.0)
        hh_scr[u, halo:halo + TM, :] = hs[u]
        hh_scr[u, halo + TM:2 * halo + TM, :] = jnp.where(has_next, after, 0.0)
        parts = []
        for g, win in enumerate(POOL_WINDOWS):
            half = win // 2
            c0 = g * gc
            acc = hh_scr[u, pl.ds(halo - half, TM), c0:c0 + gc]
            for j in range(-half + 1, half):
                acc = acc + hh_scr[u, pl.ds(halo + j, TM), c0:c0 + gc]
            lo_clip = jnp.where(has_prev, 0, jnp.maximum(half - row, 0))
            hi_clip = jnp.where(has_next, 0, jnp.maximum(row + half - TM, 0))
            cnt = (win - lo_clip - hi_clip).astype(F32)
            diff = acc / cnt - hs[u][:, c0:c0 + gc]
            parts.append(_dot(diff.astype(BF16), pw_ref[g]))
        ys.append(jnp.concatenate(parts, axis=1) * ps_ref[...])
    _post(xs, ys, mods, g2_ref[...], rwh_ref[...], rwl_ref[...], rb_ref[...], *outs)


def _pool_layer(xu, tinfo, mods, g1, pool_w, pool_scale, g2, rwh, rwl, rb):
    rows, d = xu.shape
    n_steps = tinfo[0].shape[0]
    n_e = rwh.shape[0]
    gc = pool_w.shape[-1]
    rpb = TM // SUBLANES
    sub = SUB_TILES
    out_specs, out_shapes = _post_specs(n_steps, d, n_e, sub)
    grid_spec = pltpu.PrefetchScalarGridSpec(
        num_scalar_prefetch=4,
        grid=(n_steps // sub,),
        in_specs=[
            pl.BlockSpec((TM, d), lambda i, src, mrow, hp, hn, u=u: (src[sub * i + u], 0)) for u in range(sub)
        ] + [
            pl.BlockSpec((SUBLANES, d),
                         lambda i, src, mrow, hp, hn: (jnp.maximum(src[sub * i] * rpb - 1, 0), 0)),
            pl.BlockSpec((SUBLANES, d),
                         lambda i, src, mrow, hp, hn: (jnp.minimum((src[sub * i + sub - 1] + 1) * rpb,
                                                                   rows // SUBLANES - 1), 0)),
        ] + _mod_specs(sub, d) + [
            _const_spec((1, d)),
            _const_spec((len(POOL_WINDOWS), gc, gc)),
            _const_spec((1, d)),
            _const_spec((1, d)),
            _const_spec((n_e, d)),
            _const_spec((n_e, d)),
            _const_spec((n_e, 1)),
        ],
        out_specs=out_specs,
        scratch_shapes=[pltpu.VMEM((sub, TM + 2 * SUBLANES, d), F32)],
    )
    return pl.pallas_call(
        functools.partial(_pool_kernel, sub=sub), grid_spec=grid_spec, out_shape=out_shapes,
        compiler_params=_cparams(("arbitrary",)), name="pool_mixer",
    )(*tinfo, *([xu] * (sub + 2)), *([mods] * sub), g1, pool_w.astype(BF16), pool_scale, g2, rwh, rwl, rb)


def _ssm_kernel(tile_of, x_ref, mod_ref, g1_ref, are_ref, aim_ref, bb_ref, cb_ref, y_ref,
                u_scr, y_scr, h_scr, *xs_scrs, n_ctx_tiles):
    dr = pl.program_id(0)
    i = pl.program_id(1)
    nb, tt, d = x_ref.shape
    n_sets = bb_ref.shape[1]
    sw = bb_ref.shape[3]
    hw = sw // 2
    uw = bb_ref.shape[2]
    is_ctx = tile_of[dr * pl.num_programs(1) + i] < n_ctx_tiles

    @pl.when(i == 0)
    def _():
        h_scr[...] = jnp.zeros_like(h_scr)

    g1 = g1_ref[...]
    for b in range(nb):
        shift = jnp.where(is_ctx, mod_ref[nb, 0:1, :], mod_ref[b, 0:1, :])
        scale = jnp.where(is_ctx, mod_ref[nb, 1:2, :], mod_ref[b, 1:2, :])
        hb = _rms(x_ref[b], g1) * (1.0 + scale) + shift
        for j in range(n_sets):
            u_scr[j, pl.ds(b, tt, stride=nb), :] = hb[:, j * uw:(j + 1) * uw]
    for j in range(n_sets):
        xs_scrs[j][...] = _dot(u_scr[j].astype(BF16), bb_ref[0, j])
    for j in range(n_sets):
        xs = xs_scrs[j]
        ar = jnp.broadcast_to(are_ref[0, j:j + 1, :], (nb, hw))
        ai = jnp.broadcast_to(aim_ref[0, j:j + 1, :], (nb, hw))
        hr = h_scr[:, j * sw:j * sw + hw]
        hi = h_scr[:, j * sw + hw:(j + 1) * sw]
        for step in range(tt):
            t = step + dr * (tt - 1 - 2 * step)
            r0 = pl.multiple_of(t * nb, nb)
            nhr = ar * hr - ai * hi + xs[pl.ds(r0, nb), 0:hw]
            nhi = ar * hi + ai * hr + xs[pl.ds(r0, nb), hw:sw]
            xs[pl.ds(r0, nb), 0:hw] = nhr
            xs[pl.ds(r0, nb), hw:sw] = nhi
            hr, hi = nhr, nhi
        h_scr[:, j * sw:j * sw + hw] = hr
        h_scr[:, j * sw + hw:(j + 1) * sw] = hi
        y_scr[j] = _dot(xs[...].astype(BF16), cb_ref[0, j])
    for b in range(nb):
        for j in range(n_sets):
            y_ref[0, b, :, j * uw:(j + 1) * uw] = y_scr[j, pl.ds(b, tt, stride=nb), :]


def _ssm_scan(x3, mods, g1, a_re, a_im, bblk, cblk, n_ctx):
    nb, s, d = x3.shape
    tt = SSM_TT
    nt = s // tt
    nct = n_ctx // tt
    fwd = np.arange(nt)
    bwd = np.concatenate([np.arange(nct)[::-1], np.arange(nct, nt)[::-1]])
    tile_of = jnp.asarray(np.concatenate([fwd, bwd]), jnp.int32)
    n_sets, uw, sw = bblk.shape[1:]
    grid_spec = pltpu.PrefetchScalarGridSpec(
        num_scalar_prefetch=1,
        grid=(2, nt),
        in_specs=[
            pl.BlockSpec((nb, tt, d), lambda dr, i, to: (0, to[dr * nt + i], 0)),
            pl.BlockSpec(mods.shape, lambda dr, i, to: (0, 0, 0)),
            pl.BlockSpec((1, d), lambda dr, i, to: (0, 0)),
            pl.BlockSpec((1, n_sets, sw // 2), lambda dr, i, to: (dr, 0, 0)),
            pl.BlockSpec((1, n_sets, sw // 2), lambda dr, i, to: (dr, 0, 0)),
            pl.BlockSpec((1, n_sets, uw, sw), lambda dr, i, to: (dr, 0, 0, 0)),
            pl.BlockSpec((1, n_sets, sw, uw), lambda dr, i, to: (dr, 0, 0, 0)),
        ],
        out_specs=pl.BlockSpec((1, nb, tt, d), lambda dr, i, to: (dr, 0, to[dr * nt + i], 0)),
        scratch_shapes=[
            pltpu.VMEM((n_sets, tt * nb, uw), F32),
            pltpu.VMEM((n_sets, tt * nb, uw), F32),
            pltpu.VMEM((nb, n_sets * sw), F32),
        ] + [pltpu.VMEM((tt * nb, sw), F32) for _ in range(n_sets)],
    )
    return pl.pallas_call(
        functools.partial(_ssm_kernel, n_ctx_tiles=nct), grid_spec=grid_spec,
        out_shape=jax.ShapeDtypeStruct((2, nb, s, d), F32),
        compiler_params=_cparams(("arbitrary", "arbitrary")), name="s5_scan",
    )(tile_of, x3, mods, g1, a_re, a_im, bblk, cblk)


def _ssm_params(lam_re, lam_im, log_dt, b_re, b_im, c_re, c_im):
    g, p = lam_re.shape[1:]
    h = b_re.shape[-1]
    ns = g // SSM_SET
    eye = jnp.eye(SSM_SET, dtype=F32)
    outs = []
    for dr in range(2):
        lr, li = lam_re[dr].astype(F32), lam_im[dr].astype(F32)
        br, bi = b_re[dr].astype(F32), b_im[dr].astype(F32)
        dt = jnp.exp(log_dt[dr].astype(F32))[:, None]
        zr, zi = lr * dt, li * dt
        mag = jnp.exp(zr)
        ar, ai = mag * jnp.cos(zi), mag * jnp.sin(zi)
        den = lr * lr + li * li
        cr = ((ar - 1.0) * lr + ai * li) / den
        ci = (ai * lr - (ar - 1.0) * li) / den
        bbr = cr[..., None] * br - ci[..., None] * bi
        bbi = cr[..., None] * bi + ci[..., None] * br

        def blk_b(w):
            w = jnp.transpose(w, (0, 2, 1)).reshape(ns, SSM_SET, h, p)
            return jnp.einsum('ab,jahp->jahbp', eye, w).reshape(ns, SSM_SET * h, SSM_SET * p)

        def blk_c(w):
            w = jnp.transpose(w.reshape(ns, SSM_SET, h, p), (0, 1, 3, 2))
            return jnp.einsum('ab,japh->japbh', eye, w).reshape(ns, SSM_SET * p, SSM_SET * h)

        bblk = jnp.concatenate([blk_b(bbr), blk_b(bbi)], axis=2)
        cblk = jnp.concatenate([blk_c(c_re[dr].astype(F32)), -blk_c(c_im[dr].astype(F32))], axis=1)
        outs.append((ar.reshape(ns, SSM_SET * p), ai.reshape(ns, SSM_SET * p), bblk, cblk))
    a_re = jnp.stack([o[0] for o in outs])
    a_im = jnp.stack([o[1] for o in outs])
    bblk = jnp.stack([o[2] for o in outs]).astype(BF16)
    cblk = jnp.stack([o[3] for o in outs]).astype(BF16)
    return a_re, a_im, bblk, cblk


def _glu_kernel(src, mrow, x_ref, yf_ref, yb_ref, *refs, sub):
    mod_refs = refs[:sub]
    g1_ref, dsk_ref, gw_ref, gb_ref, g2_ref, rwh_ref, rwl_ref, rb_ref = refs[sub:sub + 8]
    outs = refs[sub + 8:]
    d = x_ref.shape[1]
    xs, ys, mods = [], [], []
    for u in range(sub):
        rows = pl.ds(u * TM, TM)
        mod = mod_refs[u][0]
        x = x_ref[rows, :]
        h = _rms(x, g1_ref[...]) * (1.0 + mod[1:2]) + mod[0:1]
        y = yf_ref[0, rows, :] + yb_ref[0, rows, :] + dsk_ref[...] * h
        gl = 0.5 * y * (1.0 + jnp.tanh(math.sqrt(2.0 / math.pi) * (y + 0.044715 * (y * y * y))))
        z = _dot(gl.astype(BF16), gw_ref[...]) + gb_ref[...]
        xs.append(x)
        ys.append(z[:, :d] * _sigmoid(z[:, d:]))
        mods.append(mod)
    _post(xs, ys, mods, g2_ref[...], rwh_ref[...], rwl_ref[...], rb_ref[...], *outs)


def _glu_layer(xu, y2, tinfo, mods, g1, d_skip, glu_w, glu_b, g2, rwh, rwl, rb):
    rows, d = xu.shape
    n_steps = rows // TM
    n_e = rwh.shape[0]
    sub = SUB_TILES
    out_specs, out_shapes = _post_specs(n_steps, d, n_e, sub)
    grid_spec = pltpu.PrefetchScalarGridSpec(
        num_scalar_prefetch=2,
        grid=(n_steps // sub,),
        in_specs=[
            pl.BlockSpec((sub * TM, d), lambda i, src, mrow: (i, 0)),
            pl.BlockSpec((1, sub * TM, d), lambda i, src, mrow: (0, i, 0)),
            pl.BlockSpec((1, sub * TM, d), lambda i, src, mrow: (1, i, 0)),
        ] + _mod_specs(sub, d) + [
            _const_spec((1, d)),
            _const_spec((1, d)),
            _const_spec((d, 2 * d)),
            _const_spec((1, 2 * d)),
            _const_spec((1, d)),
            _const_spec((n_e, d)),
            _const_spec((n_e, d)),
            _const_spec((n_e, 1)),
        ],
        out_specs=out_specs,
    )
    return pl.pallas_call(
        functools.partial(_glu_kernel, sub=sub), grid_spec=grid_spec, out_shape=out_shapes,
        compiler_params=_cparams(("arbitrary",)), name="s5_glu",
    )(tinfo[0], tinfo[1], xu, y2, y2, *([mods] * sub), g1, d_skip, glu_w.astype(BF16), glu_b, g2, rwh, rwl, rb)


def _rope(v, cos, sin_signed, first_half):
    partner = jnp.where(first_half, pltpu.roll(v, HEAD_DIM - HEAD_DIM // 4, axis=1),
                        pltpu.roll(v, HEAD_DIM // 4, axis=1))
    return v * cos + partner * sin_signed


def _qkv_kernel(src, mrow, trow, x_ref, mod_ref, g1_ref, w_ref, qg_ref, kg_ref, cos_ref, sin_ref,
                q_ref, k_ref, v_ref):
    d = x_ref.shape[1]
    kvw = k_ref.shape[1]
    mod = mod_ref[0]
    h = _rms(x_ref[...], g1_ref[...]) * (1.0 + mod[1:2]) + mod[0:1]
    z = _dot(h.astype(BF16), w_ref[...])
    cos = cos_ref[...]
    sin = sin_ref[...]
    lane = lax.broadcasted_iota(jnp.int32, (TM, HEAD_DIM), 1)
    first_half = (lane % (HEAD_DIM // 2)) < (HEAD_DIM // 4)
    q_scale = HEAD_DIM ** -0.5
    for hd in range(d // HEAD_DIM):
        zh = z[:, hd * HEAD_DIM:(hd + 1) * HEAD_DIM]
        zh = _rope(_rms(zh, qg_ref[...]), cos, sin, first_half) * q_scale
        q_ref[:, hd * HEAD_DIM:(hd + 1) * HEAD_DIM] = zh.astype(BF16)
    for hd in range(kvw // HEAD_DIM):
        zh = z[:, d + hd * HEAD_DIM:d + (hd + 1) * HEAD_DIM]
        zh = _rope(_rms(zh, kg_ref[...]), cos, sin, first_half)
        k_ref[:, hd * HEAD_DIM:(hd + 1) * HEAD_DIM] = zh.astype(BF16)
    v_ref[...] = z[:, d + kvw:].astype(BF16)


def _attn_kernel(q_ref, k_ref, v_ref, o_ref, *, n_ctx):
    qt = pl.program_id(2)

    def attend(n_keys):
        for hd in range(ATTN_HEADS):
            cols = slice(hd * HEAD_DIM, (hd + 1) * HEAD_DIM)
            s = _dot_nt(q_ref[:, cols], k_ref[0:n_keys, :])
            m = jnp.max(s, axis=-1, keepdims=True)
            p = jnp.exp(s - m)
            den = jnp.sum(p, axis=-1, keepdims=True)
            o_ref[:, cols] = (_dot(p.astype(BF16), v_ref[0:n_keys, :]) / den).astype(BF16)

    @pl.when(qt < n_ctx // TM)
    def _():
        attend(n_ctx)

    @pl.when(qt >= n_ctx // TM)
    def _():
        attend(k_ref.shape[0])


def _wo_kernel(src, mrow, x_ref, o_ref, *refs, sub):
    mod_refs = refs[:sub]
    wo_ref, g2_ref, rwh_ref, rwl_ref, rb_ref = refs[sub:sub + 5]
    outs = refs[sub + 5:]
    tiles = [pl.ds(u * TM, TM) for u in range(sub)]
    ys = [_dot(o_ref[rows, :], wo_ref[...]) for rows in tiles]
    _post([x_ref[rows, :] for rows in tiles], ys, [m[0] for m in mod_refs],
          g2_ref[...], rwh_ref[...], rwl_ref[...], rb_ref[...], *outs)


def _rope_tables(n_ctx, n_lat):
    rows = n_lat // GRID_W
    row = jnp.repeat(jnp.arange(rows), GRID_W)
    col = jnp.tile(jnp.arange(GRID_W), rows)
    pos = jnp.stack([row, col], axis=-1).astype(F32)
    inv_freq = ROPE_THETA ** (-jnp.arange(ROPE_F, dtype=F32) / ROPE_F)
    ang = pos[:, :, None] * inv_freq
    cos, sin = jnp.cos(ang), jnp.sin(ang)
    cos_t = jnp.concatenate([cos, cos], axis=-1).reshape(n_lat, HEAD_DIM)
    sin_t = jnp.concatenate([-sin, sin], axis=-1).reshape(n_lat, HEAD_DIM)
    cos_t = jnp.concatenate([jnp.ones((n_ctx, HEAD_DIM), F32), cos_t], axis=0)
    sin_t = jnp.concatenate([jnp.zeros((n_ctx, HEAD_DIM), F32), sin_t], axis=0)
    return cos_t, sin_t


def _attn_layer(xu, tinfo, trow, mods, g1, wqkv, q_g, k_g, wo, g2, rwh, rwl, rb, nb, n_ctx):
    rows, d = xu.shape
    s = rows // nb
    n_steps = rows // TM
    n_e = rwh.shape[0]
    kvw = N_KV_HEADS * HEAD_DIM
    n_heads = d // HEAD_DIM
    rep = n_heads // N_KV_HEADS
    cos_t, sin_t = _rope_tables(n_ctx, s - n_ctx)
    grid_spec = pltpu.PrefetchScalarGridSpec(
        num_scalar_prefetch=3,
        grid=(n_steps,),
        in_specs=[
            pl.BlockSpec((TM, d), lambda i, src, mrow, tr: (i, 0)),
            pl.BlockSpec((1, 6, d), lambda i, src, mrow, tr: (mrow[i], 0, 0)),
            _const_spec((1, d)),
            _const_spec((d, d + 2 * kvw)),
            _const_spec((1, HEAD_DIM)),
            _const_spec((1, HEAD_DIM)),
            pl.BlockSpec((TM, HEAD_DIM), lambda i, src, mrow, tr: (tr[i], 0)),
            pl.BlockSpec((TM, HEAD_DIM), lambda i, src, mrow, tr: (tr[i], 0)),
        ],
        out_specs=[
            pl.BlockSpec((TM, d), lambda i, *_: (i, 0)),
            pl.BlockSpec((TM, kvw), lambda i, *_: (i, 0)),
            pl.BlockSpec((TM, kvw), lambda i, *_: (i, 0)),
        ],
    )
    q, k, v = pl.pallas_call(
        _qkv_kernel, grid_spec=grid_spec,
        out_shape=[jax.ShapeDtypeStruct((rows, d), BF16),
                   jax.ShapeDtypeStruct((rows, kvw), BF16),
                   jax.ShapeDtypeStruct((rows, kvw), BF16)],
        compiler_params=_cparams(("arbitrary",)), name="attn_qkv",
    )(tinfo[0], tinfo[1], trow, xu, mods, g1, wqkv.astype(BF16), q_g, k_g, cos_t, sin_t)

    tpb = s // TM
    o = pl.pallas_call(
        functools.partial(_attn_kernel, n_ctx=n_ctx),
        grid=(nb, n_heads // ATTN_HEADS, tpb),
        in_specs=[
            pl.BlockSpec((TM, ATTN_HEADS * HEAD_DIM), lambda b, hp, t: (b * tpb + t, hp)),
            pl.BlockSpec((s, HEAD_DIM), lambda b, hp, t: (b, hp * ATTN_HEADS // rep)),
            pl.BlockSpec((s, HEAD_DIM), lambda b, hp, t: (b, hp * ATTN_HEADS // rep)),
        ],
        out_specs=pl.BlockSpec((TM, ATTN_HEADS * HEAD_DIM), lambda b, hp, t: (b * tpb + t, hp)),
        out_shape=jax.ShapeDtypeStruct((rows, d), BF16),
        compiler_params=_cparams(("arbitrary", "arbitrary", "arbitrary")), name="attn_core",
    )(q, k, v)

    sub = SUB_TILES
    out_specs, out_shapes = _post_specs(n_steps, d, n_e, sub)
    grid_spec = pltpu.PrefetchScalarGridSpec(
        num_scalar_prefetch=2,
        grid=(n_steps // sub,),
        in_specs=[
            pl.BlockSpec((sub * TM, d), lambda i, src, mrow: (i, 0)),
            pl.BlockSpec((sub * TM, d), lambda i, src, mrow: (i, 0)),
        ] + _mod_specs(sub, d) + [
            _const_spec((d, d)),
            _const_spec((1, d)),
            _const_spec((n_e, d)),
            _const_spec((n_e, d)),
            _const_spec((n_e, 1)),
        ],
        out_specs=out_specs,
    )
    return pl.pallas_call(
        functools.partial(_wo_kernel, sub=sub), grid_spec=grid_spec, out_shape=out_shapes,
        compiler_params=_cparams(("arbitrary",)), name="attn_out",
    )(tinfo[0], tinfo[1], xu, o, *([mods] * sub), wo.astype(BF16), g2, rwh, rwl, rb)


def _loc_rows(n_e):
    return -(-(TM * TOP_K + (SEG_ALIGN - 1) * n_e) // 32) * 32


def _store_rows(ref, v):
    r = v.shape[0]
    slabs = ref.shape[0] // r
    for j in range(slabs):
        ref[pl.ds(j, r, stride=slabs), :] = v[:, j * LANES:(j + 1) * LANES]


def _load_rows(ref, r):
    slabs = ref.shape[0] // r
    return jnp.concatenate([ref[pl.ds(j, r, stride=slabs), :] for j in range(slabs)], axis=1)


def _pack_pairs(v, rounded):
    half = v.shape[1] // 2
    lo, hi = v[:, :half], v[:, half:]
    if not rounded:
        lo, hi = lo.astype(BF16).astype(F32), hi.astype(BF16).astype(F32)
    lo = lax.bitcast_convert_type(lo, jnp.uint32)
    hi = lax.bitcast_convert_type(hi, jnp.uint32)
    return (hi & jnp.uint32(0xFFFF0000)) | (lo >> 16)


def _unpack_pairs(p):
    lo = lax.bitcast_convert_type(p << 16, F32)
    hi = lax.bitcast_convert_type(p & jnp.uint32(0xFFFF0000), F32)
    return jnp.concatenate([lo, hi], axis=1).astype(BF16)


def _tile_rows(v):
    return pl.multiple_of(v * (SUBLANES // SEG_ALIGN), SUBLANES)


def _start_segments(t, n_e, loc, len8, dst, make_copy):
    for e in range(n_e):
        n = len8[t * n_e + e]

        @pl.when(n > 0)
        def _(e=e, n=n):
            make_copy(_tile_rows(loc[t * n_e + e]), _tile_rows(dst[t * n_e + e]), _tile_rows(n)).start()


def _dispatch_kernel(loc, len8, dst, tot, padoff, padlen, nused, h2_ref, route_ref, xs_ref,
                     buf, zbuf, sems, zsem, *, n_e, nb_max):
    i = pl.program_id(0)
    n_steps = pl.num_programs(0)
    slot = i % 2
    sub = buf.shape[1]
    lrows = buf.shape[2] // (SUBLANES // SEG_ALIGN)
    iota_p = lax.broadcasted_iota(jnp.int16, (lrows, TM), 0)
    for u in range(sub):
        pos = route_ref[u, 0:TOP_K, :].astype(jnp.int32).astype(jnp.int16)
        perm = jnp.zeros((lrows, TM), BF16)
        for k in range(TOP_K):
            perm = jnp.where(iota_p == pos[k:k + 1], jnp.ones((), BF16), perm)
        _store_rows(buf.at[slot, u], _pack_pairs(_dot(perm, h2_ref[pl.ds(u * TM, TM), :]), rounded=True))

    def seg_copy(s, u):
        def make(a, g, n):
            return pltpu.make_async_copy(buf.at[s, u, pl.ds(a, n)], xs_ref.at[pl.ds(g, n)], sems.at[s, u])
        return make

    def wait_tiles(step, s):
        for u in range(sub):
            seg_copy(s, u)(0, 0, _tile_rows(tot[sub * step + u])).wait()

    for u in range(sub):
        _start_segments(sub * i + u, n_e, loc, len8, dst, seg_copy(slot, u))

    @pl.when(i > 0)
    def _():
        wait_tiles(i - 1, 1 - slot)

    @pl.when(i == n_steps - 1)
    def _():
        wait_tiles(i, slot)

    @pl.when(i == n_steps - 1)
    def _():
        zbuf[...] = jnp.zeros_like(zbuf)

        def zero_rows(g, n):
            return pltpu.make_async_copy(zbuf.at[pl.ds(0, n)], xs_ref.at[pl.ds(g, n)], zsem)

        def pad(e, total):
            n = padlen[e]

            @pl.when(n > 0)
            def _():
                zero_rows(_tile_rows(padoff[e]), _tile_rows(n)).start()
            return total + n

        total = lax.fori_loop(0, n_e, pad, 0)

        def blank(b, carry):
            zero_rows(_tile_rows(b * BLK), zbuf.shape[0]).start()
            return carry

        lax.fori_loop(nused[0], nb_max, blank, 0)
        total = _tile_rows(total + (nb_max - nused[0]) * BLK)

        @pl.when(total > 0)
        def _():
            pltpu.make_async_copy(xs_ref.at[pl.ds(0, total)], xs_ref.at[pl.ds(0, total)], zsem).wait()


def _expert_kernel(blk, blk_e, valid, nused, xs_ref, w1_ref, b1_ref, w2_ref, b2_ref, ys_ref, w1b, w2b):
    i = pl.program_id(0)
    f = w2b.shape[0]

    @pl.when(i >= nused[0])
    def _():
        ys_ref[...] = jnp.zeros_like(ys_ref)

    @pl.when(i < nused[0])
    def _():
        first = jnp.logical_or(i == 0, blk_e[i] != blk_e[jnp.maximum(i - 1, 0)])

        @pl.when(first)
        def _():
            w1b[...] = w1_ref[0, 0].astype(BF16)
            w2b[...] = w2_ref[0, 0].astype(BF16)

        row = lax.broadcasted_iota(jnp.int32, (BLK, 1), 0)
        x = _unpack_pairs(jnp.where(row < valid[i], _load_rows(xs_ref, BLK), jnp.uint32(0)))
        z = _dot(x, w1b[...]) + b1_ref[0, 0]
        glu = jnp.minimum(z[:, :f], SWIGLU_LIMIT)
        lin = jnp.clip(z[:, f:], -SWIGLU_LIMIT, SWIGLU_LIMIT)
        act = glu * _sigmoid(SWIGLU_ALPHA * glu) * (lin + 1.0)
        y = _dot(act.astype(BF16), w2b[...]) + b2_ref[0, 0]
        _store_rows(ys_ref, _pack_pairs(y, rounded=False))


def _combine_kernel(loc, len8, dst, tot, mrow, ys_ref, route_ref, x1_ref, *refs, n_e):
    buf, sems = refs[-2:]
    out_ref = refs[-3]
    mod_refs = refs[:-3]
    i = pl.program_id(0)
    n_steps = pl.num_programs(0)
    slot = i % 2
    sub = buf.shape[1]
    lrows = buf.shape[2] // (SUBLANES // SEG_ALIGN)

    def seg_copy(s, u):
        def make(a, g, n):
            return pltpu.make_async_copy(ys_ref.at[pl.ds(g, n)], buf.at[s, u, pl.ds(a, n)], sems.at[s, u])
        return make

    @pl.when(i == 0)
    def _():
        for u in range(sub):
            _start_segments(u, n_e, loc, len8, dst, seg_copy(0, u))

    @pl.when(i + 1 < n_steps)
    def _():
        for u in range(sub):
            _start_segments(sub * (i + 1) + u, n_e, loc, len8, dst, seg_copy(1 - slot, u))

    iota_p = lax.broadcasted_iota(jnp.int16, (lrows, TM), 0)
    row = lax.broadcasted_iota(jnp.int32, (lrows, 1), 0)
    totals = [tot[sub * i + u] for u in range(sub)]
    for u in range(sub):
        seg_copy(slot, u)(0, 0, _tile_rows(totals[u])).wait()
    for u in range(sub):
        total = totals[u]
        pos = route_ref[u, 0:TOP_K, :].astype(jnp.int32).astype(jnp.int16)
        gate = route_ref[u, TOP_K:2 * TOP_K, :].astype(BF16)
        gt = jnp.zeros((lrows, TM), BF16)
        for k in range(TOP_K):
            gt = jnp.where(iota_p == pos[k:k + 1], gate[k:k + 1], gt)
        ysl = _unpack_pairs(jnp.where(row < total, _load_rows(buf.at[slot, u], lrows), jnp.uint32(0)))
        rows = pl.ds(u * TM, TM)
        out_ref[rows, :] = x1_ref[rows, :] + mod_refs[u][0][5:6] * _dot_tn(gt, ysl)


def _moe(x1, h2, route, cnt, mods, mrow, layer, w1, b1, w2, b2):
    rows, d = x1.shape
    n_tiles = rows // TM
    depth, n_e, _, f2 = w1.shape
    lrows = _loc_rows(n_e)
    nb_max = (rows * TOP_K + (SEG_ALIGN - 1) * n_tiles * n_e) // BLK + n_e
    cap = nb_max * BLK
    slabs = d // 2 // LANES
    assert slabs * SEG_ALIGN == SUBLANES

    i32 = lambda v: v.astype(jnp.int32)
    cnt = i32(cnt.reshape(n_tiles, n_e))
    c8 = (cnt + SEG_ALIGN - 1) // SEG_ALIGN * SEG_ALIGN
    loc = jnp.cumsum(c8, axis=1) - c8
    tot_tile = jnp.sum(c8, axis=1)
    tot = jnp.sum(c8, axis=0)
    nblk = (tot + BLK - 1) // BLK
    blk_end = jnp.cumsum(nblk)
    blk_start = blk_end - nblk
    dst = (blk_start * BLK)[None, :] + jnp.cumsum(c8, axis=0) - c8
    nused = blk_end[-1]
    bid = jnp.arange(nb_max, dtype=jnp.int32)
    bidc = jnp.minimum(bid, nused - 1)
    blk_e = jnp.minimum(jnp.sum(i32(bidc[:, None] >= blk_end[None, :]), axis=1), n_e - 1)
    valid = jnp.clip(tot[blk_e] - (bidc - blk_start[blk_e]) * BLK, 0, BLK)
    padoff = blk_start * BLK + tot
    padlen = nblk * BLK - tot
    loc, len8, dst = i32(loc.reshape(-1)), i32(c8.reshape(-1)), i32(dst.reshape(-1))
    nused = i32(nused.reshape(1))
    sub = SUB_TILES
    row_buf = pltpu.VMEM((2, sub, lrows * slabs, LANES), jnp.uint32)

    xs = pl.pallas_call(
        functools.partial(_dispatch_kernel, n_e=n_e, nb_max=nb_max),
        grid_spec=pltpu.PrefetchScalarGridSpec(
            num_scalar_prefetch=7,
            grid=(n_tiles // sub,),
            in_specs=[
                pl.BlockSpec((sub * TM, d), lambda i, *_: (i, 0)),
                pl.BlockSpec((sub, 2 * TOP_K, TM), lambda i, *_: (i, 0, 0)),
            ],
            out_specs=pl.BlockSpec(memory_space=pl.ANY),
            scratch_shapes=[row_buf, pltpu.VMEM((BLK * slabs, LANES), jnp.uint32),
                            pltpu.SemaphoreType.DMA((2, sub)), pltpu.SemaphoreType.DMA(())],
        ),
        out_shape=jax.ShapeDtypeStruct((cap * slabs, LANES), jnp.uint32),
        compiler_params=_cparams(("arbitrary",)), name="moe_dispatch",
    )(loc, len8, dst, i32(tot_tile), i32(padoff), i32(padlen), nused, h2, route)

    ys = pl.pallas_call(
        _expert_kernel,
        grid_spec=pltpu.PrefetchScalarGridSpec(
            num_scalar_prefetch=4,
            grid=(nb_max,),
            in_specs=[
                pl.BlockSpec((BLK * slabs, LANES), lambda i, blk, be, va, nu: (blk[i], 0)),
                pl.BlockSpec((1, 1, d, f2), lambda i, blk, be, va, nu: (layer, be[i], 0, 0)),
                pl.BlockSpec((1, 1, 1, f2), lambda i, blk, be, va, nu: (layer, be[i], 0, 0)),
                pl.BlockSpec((1, 1, f2 // 2, d), lambda i, blk, be, va, nu: (layer, be[i], 0, 0)),
                pl.BlockSpec((1, 1, 1, d), lambda i, blk, be, va, nu: (layer, be[i], 0, 0)),
            ],
            out_specs=pl.BlockSpec((BLK * slabs, LANES), lambda i, blk, be, va, nu: (i, 0)),
            scratch_shapes=[pltpu.VMEM((d, f2), BF16), pltpu.VMEM((f2 // 2, d), BF16)],
        ),
        out_shape=jax.ShapeDtypeStruct((cap * slabs, LANES), jnp.uint32),
        compiler_params=_cparams(("arbitrary",)), name="moe_experts",
    )(i32(bidc), i32(blk_e), i32(valid), nused, xs, w1, b1.reshape(depth, n_e, 1, f2), w2,
      b2.reshape(depth, n_e, 1, d))

    return pl.pallas_call(
        functools.partial(_combine_kernel, n_e=n_e),
        grid_spec=pltpu.PrefetchScalarGridSpec(
            num_scalar_prefetch=5,
            grid=(n_tiles // sub,),
            in_specs=[
                pl.BlockSpec(memory_space=pl.ANY),
                pl.BlockSpec((sub, 2 * TOP_K, TM), lambda i, *_: (i, 0, 0)),
                pl.BlockSpec((sub * TM, d), lambda i, *_: (i, 0)),
            ] + [pl.BlockSpec((1, 6, d), lambda i, lo, le, ds, to, mr, u=u: (mr[sub * i + u], 0, 0))
                 for u in range(sub)],
            out_specs=pl.BlockSpec((sub * TM, d), lambda i, *_: (i, 0)),
            scratch_shapes=[row_buf, pltpu.SemaphoreType.DMA((2, sub))],
        ),
        out_shape=jax.ShapeDtypeStruct((rows, d), F32),
        compiler_params=_cparams(("arbitrary",)), name="moe_combine",
    )(loc, len8, dst, i32(tot_tile), mrow, ys, route, x1, *([mods] * sub))


def _tile_info(nb, n_ctx, n_lat, latent_only):
    tpb = (n_ctx + n_lat) // TM
    ct = n_ctx // TM
    src, mrow, hp, hn, trow = [], [], [], [], []
    for b in range(nb):
        for j in range(ct if latent_only else 0, tpb):
            is_ctx = j < ct
            src.append(b * tpb + j)
            mrow.append(nb if is_ctx else b)
            hp.append(0 if j in (0, ct) else 1)
            hn.append(0 if j in (ct - 1, tpb - 1) else 1)
            trow.append(j)
    mk = lambda v: jnp.asarray(np.asarray(v, np.int32))
    return (mk(src), mk(mrow), mk(hp), mk(hn)), mk(trow)


def kernel(x, c, ctx, c_ctx, ada_w, ada_b, norm1_g, norm2_g, pool_w, pool_scale, ssm_lam_re, ssm_lam_im,
           ssm_log_dt, ssm_b_re, ssm_b_im, ssm_c_re, ssm_c_im, ssm_d, ssm_glu_w, ssm_glu_b, attn_wqkv,
           attn_q_g, attn_k_g, attn_wo, router_w, router_b, moe_w1, moe_b1, moe_w2, moe_b2):
    nb, n_lat, d = x.shape
    n_ctx = ctx.shape[1]
    depth = ada_w.shape[0]
    n_e = router_w.shape[-1]
    assert nb == 8 and n_ctx % TM == 0 and n_lat % TM == 0 and n_lat % GRID_W == 0
    s = n_ctx + n_lat

    c16 = jnp.concatenate([c, c_ctx[None, :], jnp.zeros((16 - nb - 1, d), F32)], axis=0)
    mods_all = _ada_mods(c16, ada_w, ada_b).reshape(depth, 16, 6, d)

    xu = jnp.concatenate([ctx, x], axis=1).reshape(nb * s, d)
    tinfo_u, trow_u = _tile_info(nb, n_ctx, n_lat, latent_only=False)
    tinfo_l, _ = _tile_info(nb, n_ctx, n_lat, latent_only=True)

    for i in range(depth):
        kind, j = i % 3, i // 3
        last = i == depth - 1
        mods = mods_all[i]
        g1 = norm1_g[i].reshape(1, d)
        g2 = norm2_g[i].reshape(1, d)
        rwt = router_w[i].T
        rwh = rwt.astype(BF16)
        rwl = (rwt - rwh.astype(F32)).astype(BF16)
        rb = router_b[i].reshape(n_e, 1)
        if kind == 0:
            tinfo = tinfo_l if last else tinfo_u
            x1, h2, route, cnt = _pool_layer(xu, tinfo, mods, g1, pool_w[j], pool_scale[j].reshape(1, d),
                                             g2, rwh, rwl, rb)
            mrow = tinfo[1]
        elif kind == 1:
            assert not last
            a_re, a_im, bblk, cblk = _ssm_params(ssm_lam_re[j], ssm_lam_im[j], ssm_log_dt[j], ssm_b_re[j],
                                                 ssm_b_im[j], ssm_c_re[j], ssm_c_im[j])
            y2 = _ssm_scan(xu.reshape(nb, s, d), mods, g1, a_re, a_im, bblk, cblk, n_ctx)
            x1, h2, route, cnt = _glu_layer(xu, y2.reshape(2, nb * s, d), tinfo_u, mods, g1,
                                            ssm_d[j].reshape(1, d), ssm_glu_w[j], ssm_glu_b[j].reshape(1, 2 * d),
                                            g2, rwh, rwl, rb)
            mrow = tinfo_u[1]
        else:
            assert not last
            x1, h2, route, cnt = _attn_layer(xu, tinfo_u, trow_u, mods, g1, attn_wqkv[j],
                                             attn_q_g[j].reshape(1, HEAD_DIM), attn_k_g[j].reshape(1, HEAD_DIM),
                                             attn_wo[j], g2, rwh, rwl, rb, nb, n_ctx)
            mrow = tinfo_u[1]
        xu = _moe(x1, h2, route, cnt, mods, mrow, i, moe_w1, moe_b1, moe_w2, moe_b2)
    if xu.shape[0] == nb * n_lat:
        return xu.reshape(nb, n_lat, d)
    return xu.reshape(nb, s, d)[:, n_ctx:, :]
```

```python
import functools
import math

import numpy as np
import jax
import jax.numpy as jnp
from jax import lax
from jax.experimental import pallas as pl
from jax.experimental.pallas import tpu as pltpu

F32 = jnp.float32
BF16 = jnp.bfloat16

GRID_W = 64
NORM_EPS = 1e-6
POOL_WINDOWS = (2, 4, 8, 16)
SSM_H = 16
SSM_P = 64
SSM_SET = 8
SSM_TT = 32
HEAD_DIM = 128
N_KV_HEADS = 2
ATTN_HEADS = 4
ROPE_F = HEAD_DIM // 4
ROPE_THETA = 10000.0
TOP_K = 4
SWIGLU_ALPHA = 1.702
SWIGLU_LIMIT = 7.0

TM = 256
SUB_TILES = 4
BLK = 512
SUBLANES = 8
LANES = 128
SEG_ALIGN = 2
VMEM_LIMIT = 56 * 1024 * 1024


def _cparams(sem, vmem=VMEM_LIMIT):
    return pltpu.CompilerParams(dimension_semantics=sem, vmem_limit_bytes=vmem)


def _rms(x, g):
    return x * lax.rsqrt(jnp.mean(x * x, axis=-1, keepdims=True) + NORM_EPS) * g


def _sigmoid(x):
    return 1.0 / (1.0 + jnp.exp(-x))


def _dot(a, b):
    return jnp.dot(a, b, preferred_element_type=F32)


def _dot_nt(a, b):
    return lax.dot_general(a, b, (((1,), (1,)), ((), ())), preferred_element_type=F32)


def _dot_tn(a, b):
    return lax.dot_general(a, b, (((0,), (0,)), ((), ())), preferred_element_type=F32)


def _ada_kernel(c_ref, w_ref, b_ref, o_ref):
    c = c_ref[...]
    s = c * _sigmoid(c)
    o_ref[0] = jnp.dot(s, w_ref[0], preferred_element_type=F32,
                       precision=lax.Precision.HIGHEST) + b_ref[0]


def _ada_mods(c16, ada_w, ada_b):
    depth, d, six_d = ada_w.shape
    tn = d
    return pl.pallas_call(
        _ada_kernel,
        grid=(depth, six_d // tn),
        in_specs=[
            pl.BlockSpec((16, d), lambda l, j: (0, 0)),
            pl.BlockSpec((1, d, tn), lambda l, j: (l, 0, j)),
            pl.BlockSpec((1, 1, tn), lambda l, j: (l, 0, j)),
        ],
        out_specs=pl.BlockSpec((1, 16, tn), lambda l, j: (l, 0, j)),
        out_shape=jax.ShapeDtypeStruct((depth, 16, six_d), F32),
        compiler_params=_cparams(("arbitrary", "arbitrary")),
        name="ada_mods",
    )(c16, ada_w, ada_b.reshape(depth, 1, six_d))


def _post_logits(x, y, mod, g2, rwh, rwl, rb, x1_ref, h2_ref):
    x1 = x + mod[2:3] * y
    x1_ref[...] = x1
    h2 = _rms(x1, g2) * (1.0 + mod[4:5]) + mod[3:4]
    h2_ref[...] = h2.astype(BF16)
    hh = h2.astype(BF16)
    hl = (h2 - hh.astype(F32)).astype(BF16)
    return _dot_nt(rwh, hh) + _dot_nt(rwh, hl) + _dot_nt(rwl, hh) + rb


def _post_route(logits, route_ref, cnt_ref):
    n_e = logits.shape[0]
    iota_e = lax.broadcasted_iota(jnp.int32, (n_e, TM), 0)
    vals, onehots = [], []
    l = logits
    for _ in range(TOP_K):
        m = jnp.max(l, axis=0, keepdims=True)
        idx = jnp.min(jnp.where(l == m, iota_e, n_e), axis=0, keepdims=True)
        sel = iota_e == idx
        vals.append(m)
        onehots.append(sel)
        l = jnp.where(sel, -jnp.inf, l)
    ex = [jnp.exp(v - vals[0]) for v in vals]
    den = ex[0] + ex[1] + ex[2] + ex[3]
    gates = [e / den for e in ex]
    member = jnp.zeros((n_e, TM), F32)
    for sel in onehots:
        member = member + jnp.where(sel, 1.0, 0.0)
    r_i = lax.broadcasted_iota(jnp.int32, (TM, TM), 0)
    c_i = lax.broadcasted_iota(jnp.int32, (TM, TM), 1)
    upper = jnp.where(r_i < c_i, 1.0, 0.0).astype(BF16)
    cum = _dot(member.astype(BF16), upper)
    cnt = jnp.sum(member, axis=1, keepdims=True)
    cnt_ref[0] = cnt
    c8 = jnp.floor((cnt + (SEG_ALIGN - 1)) * (1.0 / SEG_ALIGN)) * SEG_ALIGN
    e_r = lax.broadcasted_iota(jnp.int32, (n_e, n_e), 0)
    e_c = lax.broadcasted_iota(jnp.int32, (n_e, n_e), 1)
    lower = jnp.where(e_c < e_r, 1.0, 0.0).astype(BF16)
    seg = _dot(lower, jnp.broadcast_to(c8, (n_e, TM)).astype(BF16))
    base = seg + cum
    rows = []
    for sel in onehots:
        rows.append(jnp.sum(jnp.where(sel, base, 0.0), axis=0, keepdims=True))
    route_ref[0] = jnp.concatenate(rows + gates, axis=0)


def _post_specs(n_tiles, d, n_e, sub=1):
    specs = [
        pl.BlockSpec((sub * TM, d), lambda i, *_: (i, 0)),
        pl.BlockSpec((sub * TM, d), lambda i, *_: (i, 0)),
        pl.BlockSpec((sub, 2 * TOP_K, TM), lambda i, *_: (i, 0, 0)),
        pl.BlockSpec((sub, n_e, 1), lambda i, *_: (i, 0, 0)),
    ]
    shapes = [
        jax.ShapeDtypeStruct((n_tiles * TM, d), F32),
        jax.ShapeDtypeStruct((n_tiles * TM, d), BF16),
        jax.ShapeDtypeStruct((n_tiles, 2 * TOP_K, TM), F32),
        jax.ShapeDtypeStruct((n_tiles, n_e, 1), F32),
    ]
    return specs, shapes


def _post(xs, ys, mods, g2, rwh, rwl, rb, x1_ref, h2_ref, route_ref, cnt_ref):
    logits = []
    for u, (x, y, mod) in enumerate(zip(xs, ys, mods)):
        rows = pl.ds(u * TM, TM)
        logits.append(_post_logits(x, y, mod, g2, rwh, rwl, rb, x1_ref.at[rows], h2_ref.at[rows]))
    for u, l in enumerate(logits):
        _post_route(l, route_ref.at[pl.ds(u, 1)], cnt_ref.at[pl.ds(u, 1)])


def _mod_specs(sub, d):
    return [pl.BlockSpec((1, 6, d), lambda i, src, mrow, *_, u=u: (mrow[sub * i + u], 0, 0))
            for u in range(sub)]


def _const_spec(shape):
    nd = len(shape)
    return pl.BlockSpec(shape, lambda i, *_: (0,) * nd)


def _pool_kernel(src, mrow, hp, hn, *refs, sub):
    x_refs = refs[:sub]
    xp_ref, xn_ref = refs[sub:sub + 2]
    mod_refs = refs[sub + 2:2 * sub + 2]
    g1_ref, pw_ref, ps_ref, g2_ref, rwh_ref, rwl_ref, rb_ref = refs[2 * sub + 2:2 * sub + 9]
    outs = refs[2 * sub + 9:2 * sub + 13]
    hh_scr = refs[2 * sub + 13]
    i = pl.program_id(0)
    d = x_refs[0].shape[1]
    gc = d // len(POOL_WINDOWS)
    halo = SUBLANES
    g1 = g1_ref[...]

    def pre(v, mod):
        return _rms(v, g1) * (1.0 + mod[1:2]) + mod[0:1]

    mods = [m[0] for m in mod_refs]
    xs = [r[...] for r in x_refs]
    hs = [pre(x, mod) for x, mod in zip(xs, mods)]
    row = lax.broadcasted_iota(jnp.int32, (TM, 1), 0)
    ys = []
    for u in range(sub):
        has_prev = hp[sub * i + u] > 0
        has_next = hn[sub * i + u] > 0
        before = pre(xp_ref[...], mods[0]) if u == 0 else hs[u - 1][TM - halo:, :]
        after = pre(xn_ref[...], mods[-1]) if u == sub - 1 else hs[u + 1][:halo, :]
        hh_scr[u, 0:halo, :] = jnp.where(has_prev, before, 0.0)
        hh_scr[u, halo:halo + TM, :] = hs[u]
        hh_scr[u, halo + TM:2 * halo + TM, :] = jnp.where(has_next, after, 0.0)
        parts = []
        for g, win in enumerate(POOL_WINDOWS):
            half = win // 2
            c0 = g * gc
            acc = hh_scr[u, pl.ds(halo - half, TM), c0:c0 + gc]
            for j in range(-half + 1, half):
                acc = acc + hh_scr[u, pl.ds(halo + j, TM), c0:c0 + gc]
            lo_clip = jnp.where(has_prev, 0, jnp.maximum(half - row, 0))
            hi_clip = jnp.where(has_next, 0, jnp.maximum(row + half - TM, 0))
            cnt = (win - lo_clip - hi_clip).astype(F32)
            diff = acc / cnt - hs[u][:, c0:c0 + gc]
            parts.append(_dot(diff.astype(BF16), pw_ref[g]))
        ys.append(jnp.concatenate(parts, axis=1) * ps_ref[...])
    _post(xs, ys, mods, g2_ref[...], rwh_ref[...], rwl_ref[...], rb_ref[...], *outs)


def _pool_layer(xu, tinfo, mods, g1, pool_w, pool_scale, g2, rwh, rwl, rb):
    rows, d = xu.shape
    n_steps = tinfo[0].shape[0]
    n_e = rwh.shape[0]
    gc = pool_w.shape[-1]
    rpb = TM // SUBLANES
    sub = SUB_TILES
    out_specs, out_shapes = _post_specs(n_steps, d, n_e, sub)
    grid_spec = pltpu.PrefetchScalarGridSpec(
        num_scalar_prefetch=4,
        grid=(n_steps // sub,),
        in_specs=[
            pl.BlockSpec((TM, d), lambda i, src, mrow, hp, hn, u=u: (src[sub * i + u], 0)) for u in range(sub)
        ] + [
            pl.BlockSpec((SUBLANES, d),
                         lambda i, src, mrow, hp, hn: (jnp.maximum(src[sub * i] * rpb - 1, 0), 0)),
            pl.BlockSpec((SUBLANES, d),
                         lambda i, src, mrow, hp, hn: (jnp.minimum((src[sub * i + sub - 1] + 1) * rpb,
                                                                   rows // SUBLANES - 1), 0)),
        ] + _mod_specs(sub, d) + [
            _const_spec((1, d)),
            _const_spec((len(POOL_WINDOWS), gc, gc)),
            _const_spec((1, d)),
            _const_spec((1, d)),
            _const_spec((n_e, d)),
            _const_spec((n_e, d)),
            _const_spec((n_e, 1)),
        ],
        out_specs=out_specs,
        scratch_shapes=[pltpu.VMEM((sub, TM + 2 * SUBLANES, d), F32)],
    )
    return pl.pallas_call(
        functools.partial(_pool_kernel, sub=sub), grid_spec=grid_spec, out_shape=out_shapes,
        compiler_params=_cparams(("arbitrary",)), name="pool_mixer",
    )(*tinfo, *([xu] * (sub + 2)), *([mods] * sub), g1, pool_w.astype(BF16), pool_scale, g2, rwh, rwl, rb)


def _ssm_kernel(tile_of, x_ref, mod_ref, g1_ref, are_ref, aim_ref, bb_ref, cb_ref, y_ref,
                u_scr, y_scr, h_scr, *xs_scrs, n_ctx_tiles):
    dr = pl.program_id(0)
    i = pl.program_id(1)
    nb, tt, d = x_ref.shape
    n_sets = bb_ref.shape[1]
    sw = bb_ref.shape[3]
    hw = sw // 2
    uw = bb_ref.shape[2]
    is_ctx = tile_of[dr * pl.num_programs(1) + i] < n_ctx_tiles

    @pl.when(i == 0)
    def _():
        h_scr[...] = jnp.zeros_like(h_scr)

    g1 = g1_ref[...]
    for b in range(nb):
        shift = jnp.where(is_ctx, mod_ref[nb, 0:1, :], mod_ref[b, 0:1, :])
        scale = jnp.where(is_ctx, mod_ref[nb, 1:2, :], mod_ref[b, 1:2, :])
        hb = _rms(x_ref[b], g1) * (1.0 + scale) + shift
        for j in range(n_sets):
            u_scr[j, pl.ds(b, tt, stride=nb), :] = hb[:, j * uw:(j + 1) * uw]
    for j in range(n_sets):
        xs_scrs[j][...] = _dot(u_scr[j].astype(BF16), bb_ref[0, j])
    for j in range(n_sets):
        xs = xs_scrs[j]
        ar = jnp.broadcast_to(are_ref[0, j:j + 1, :], (nb, hw))
        ai = jnp.broadcast_to(aim_ref[0, j:j + 1, :], (nb, hw))
        hr = h_scr[:, j * sw:j * sw + hw]
        hi = h_scr[:, j * sw + hw:(j + 1) * sw]
        for step in range(tt):
            t = step + dr * (tt - 1 - 2 * step)
            r0 = pl.multiple_of(t * nb, nb)
            nhr = ar * hr - ai * hi + xs[pl.ds(r0, nb), 0:hw]
            nhi = ar * hi + ai * hr + xs[pl.ds(r0, nb), hw:sw]
            xs[pl.ds(r0, nb), 0:hw] = nhr
            xs[pl.ds(r0, nb), hw:sw] = nhi
            hr, hi = nhr, nhi
        h_scr[:, j * sw:j * sw + hw] = hr
        h_scr[:, j * sw + hw:(j + 1) * sw] = hi
        y_scr[j] = _dot(xs[...].astype(BF16), cb_ref[0, j])
    for b in range(nb):
        for j in range(n_sets):
            y_ref[0, b, :, j * uw:(j + 1) * uw] = y_scr[j, pl.ds(b, tt, stride=nb), :]


def _ssm_scan(x3, mods, g1, a_re, a_im, bblk, cblk, n_ctx):
    nb, s, d = x3.shape
    tt = SSM_TT
    nt = s // tt
    nct = n_ctx // tt
    fwd = np.arange(nt)
    bwd = np.concatenate([np.arange(nct)[::-1], np.arange(nct, nt)[::-1]])
    tile_of = jnp.asarray(np.concatenate([fwd, bwd]), jnp.int32)
    n_sets, uw, sw = bblk.shape[1:]
    grid_spec = pltpu.PrefetchScalarGridSpec(
        num_scalar_prefetch=1,
        grid=(2, nt),
        in_specs=[
            pl.BlockSpec((nb, tt, d), lambda dr, i, to: (0, to[dr * nt + i], 0)),
            pl.BlockSpec(mods.shape, lambda dr, i, to: (0, 0, 0)),
            pl.BlockSpec((1, d), lambda dr, i, to: (0, 0)),
            pl.BlockSpec((1, n_sets, sw // 2), lambda dr, i, to: (dr, 0, 0)),
            pl.BlockSpec((1, n_sets, sw // 2), lambda dr, i, to: (dr, 0, 0)),
            pl.BlockSpec((1, n_sets, uw, sw), lambda dr, i, to: (dr, 0, 0, 0)),
            pl.BlockSpec((1, n_sets, sw, uw), lambda dr, i, to: (dr, 0, 0, 0)),
        ],
        out_specs=pl.BlockSpec((1, nb, tt, d), lambda dr, i, to: (dr, 0, to[dr * nt + i], 0)),
        scratch_shapes=[
            pltpu.VMEM((n_sets, tt * nb, uw), F32),
            pltpu.VMEM((n_sets, tt * nb, uw), F32),
            pltpu.VMEM((nb, n_sets * sw), F32),
        ] + [pltpu.VMEM((tt * nb, sw), F32) for _ in range(n_sets)],
    )
    return pl.pallas_call(
        functools.partial(_ssm_kernel, n_ctx_tiles=nct), grid_spec=grid_spec,
        out_shape=jax.ShapeDtypeStruct((2, nb, s, d), F32),
        compiler_params=_cparams(("arbitrary", "arbitrary")), name="s5_scan",
    )(tile_of, x3, mods, g1, a_re, a_im, bblk, cblk)


def _ssm_params(lam_re, lam_im, log_dt, b_re, b_im, c_re, c_im):
    g, p = lam_re.shape[1:]
    h = b_re.shape[-1]
    ns = g // SSM_SET
    eye = jnp.eye(SSM_SET, dtype=F32)
    outs = []
    for dr in range(2):
        lr, li = lam_re[dr].astype(F32), lam_im[dr].astype(F32)
        br, bi = b_re[dr].astype(F32), b_im[dr].astype(F32)
        dt = jnp.exp(log_dt[dr].astype(F32))[:, None]
        zr, zi = lr * dt, li * dt
        mag = jnp.exp(zr)
        ar, ai = mag * jnp.cos(zi), mag * jnp.sin(zi)
        den = lr * lr + li * li
        cr = ((ar - 1.0) * lr + ai * li) / den
        ci = (ai * lr - (ar - 1.0) * li) / den
        bbr = cr[..., None] * br - ci[..., None] * bi
        bbi = cr[..., None] * bi + ci[..., None] * br

        def blk_b(w):
            w = jnp.transpose(w, (0, 2, 1)).reshape(ns, SSM_SET, h, p)
            return jnp.einsum('ab,jahp->jahbp', eye, w).reshape(ns, SSM_SET * h, SSM_SET * p)

        def blk_c(w):
            w = jnp.transpose(w.reshape(ns, SSM_SET, h, p), (0, 1, 3, 2))
            return jnp.einsum('ab,japh->japbh', eye, w).reshape(ns, SSM_SET * p, SSM_SET * h)

        bblk = jnp.concatenate([blk_b(bbr), blk_b(bbi)], axis=2)
        cblk = jnp.concatenate([blk_c(c_re[dr].astype(F32)), -blk_c(c_im[dr].astype(F32))], axis=1)
        outs.append((ar.reshape(ns, SSM_SET * p), ai.reshape(ns, SSM_SET * p), bblk, cblk))
    a_re = jnp.stack([o[0] for o in outs])
    a_im = jnp.stack([o[1] for o in outs])
    bblk = jnp.stack([o[2] for o in outs]).astype(BF16)
    cblk = jnp.stack([o[3] for o in outs]).astype(BF16)
    return a_re, a_im, bblk, cblk


def _glu_kernel(src, mrow, x_ref, yf_ref, yb_ref, *refs, sub):
    mod_refs = refs[:sub]
    g1_ref, dsk_ref, gw_ref, gb_ref, g2_ref, rwh_ref, rwl_ref, rb_ref = refs[sub:sub + 8]
    outs = refs[sub + 8:]
    d = x_ref.shape[1]
    xs, ys, mods = [], [], []
    for u in range(sub):
        rows = pl.ds(u * TM, TM)
        mod = mod_refs[u][0]
        x = x_ref[rows, :]
        h = _rms(x, g1_ref[...]) * (1.0 + mod[1:2]) + mod[0:1]
        y = yf_ref[0, rows, :] + yb_ref[0, rows, :] + dsk_ref[...] * h
        gl = 0.5 * y * (1.0 + jnp.tanh(math.sqrt(2.0 / math.pi) * (y + 0.044715 * (y * y * y))))
        z = _dot(gl.astype(BF16), gw_ref[...]) + gb_ref[...]
        xs.append(x)
        ys.append(z[:, :d] * _sigmoid(z[:, d:]))
        mods.append(mod)
    _post(xs, ys, mods, g2_ref[...], rwh_ref[...], rwl_ref[...], rb_ref[...], *outs)


def _glu_layer(xu, y2, tinfo, mods, g1, d_skip, glu_w, glu_b, g2, rwh, rwl, rb):
    rows, d = xu.shape
    n_steps = rows // TM
    n_e = rwh.shape[0]
    sub = SUB_TILES
    out_specs, out_shapes = _post_specs(n_steps, d, n_e, sub)
    grid_spec = pltpu.PrefetchScalarGridSpec(
        num_scalar_prefetch=2,
        grid=(n_steps // sub,),
        in_specs=[
            pl.BlockSpec((sub * TM, d), lambda i, src, mrow: (i, 0)),
            pl.BlockSpec((1, sub * TM, d), lambda i, src, mrow: (0, i, 0)),
            pl.BlockSpec((1, sub * TM, d), lambda i, src, mrow: (1, i, 0)),
        ] + _mod_specs(sub, d) + [
            _const_spec((1, d)),
            _const_spec((1, d)),
            _const_spec((d, 2 * d)),
            _const_spec((1, 2 * d)),
            _const_spec((1, d)),
            _const_spec((n_e, d)),
            _const_spec((n_e, d)),
            _const_spec((n_e, 1)),
        ],
        out_specs=out_specs,
    )
    return pl.pallas_call(
        functools.partial(_glu_kernel, sub=sub), grid_spec=grid_spec, out_shape=out_shapes,
        compiler_params=_cparams(("arbitrary",)), name="s5_glu",
    )(tinfo[0], tinfo[1], xu, y2, y2, *([mods] * sub), g1, d_skip, glu_w.astype(BF16), glu_b, g2, rwh, rwl, rb)


def _rope(v, cos, sin_signed, first_half):
    partner = jnp.where(first_half, pltpu.roll(v, HEAD_DIM - HEAD_DIM // 4, axis=1),
                        pltpu.roll(v, HEAD_DIM // 4, axis=1))
    return v * cos + partner * sin_signed


def _qkv_kernel(src, mrow, trow, x_ref, mod_ref, g1_ref, w_ref, qg_ref, kg_ref, cos_ref, sin_ref,
                q_ref, k_ref, v_ref):
    d = x_ref.shape[1]
    kvw = k_ref.shape[1]
    mod = mod_ref[0]
    h = _rms(x_ref[...], g1_ref[...]) * (1.0 + mod[1:2]) + mod[0:1]
    z = _dot(h.astype(BF16), w_ref[...])
    cos = cos_ref[...]
    sin = sin_ref[...]
    lane = lax.broadcasted_iota(jnp.int32, (TM, HEAD_DIM), 1)
    first_half = (lane % (HEAD_DIM // 2)) < (HEAD_DIM // 4)
    q_scale = HEAD_DIM ** -0.5
    for hd in range(d // HEAD_DIM):
        zh = z[:, hd * HEAD_DIM:(hd + 1) * HEAD_DIM]
        zh = _rope(_rms(zh, qg_ref[...]), cos, sin, first_half) * q_scale
        q_ref[:, hd * HEAD_DIM:(hd + 1) * HEAD_DIM] = zh.astype(BF16)
    for hd in range(kvw // HEAD_DIM):
        zh = z[:, d + hd * HEAD_DIM:d + (hd + 1) * HEAD_DIM]
        zh = _rope(_rms(zh, kg_ref[...]), cos, sin, first_half)
        k_ref[:, hd * HEAD_DIM:(hd + 1) * HEAD_DIM] = zh.astype(BF16)
    v_ref[...] = z[:, d + kvw:].astype(BF16)


def _attn_kernel(q_ref, k_ref, v_ref, o_ref, *, n_ctx):
    qt = pl.program_id(2)

    def attend(n_keys):
        for hd in range(ATTN_HEADS):
            cols = slice(hd * HEAD_DIM, (hd + 1) * HEAD_DIM)
            s = _dot_nt(q_ref[:, cols], k_ref[0:n_keys, :])
            m = jnp.max(s, axis=-1, keepdims=True)
            p = jnp.exp(s - m)
            den = jnp.sum(p, axis=-1, keepdims=True)
            o_ref[:, cols] = (_dot(p.astype(BF16), v_ref[0:n_keys, :]) / den).astype(BF16)

    @pl.when(qt < n_ctx // TM)
    def _():
        attend(n_ctx)

    @pl.when(qt >= n_ctx // TM)
    def _():
        attend(k_ref.shape[0])


def _wo_kernel(src, mrow, x_ref, o_ref, *refs, sub):
    mod_refs = refs[:sub]
    wo_ref, g2_ref, rwh_ref, rwl_ref, rb_ref = refs[sub:sub + 5]
    outs = refs[sub + 5:]
    tiles = [pl.ds(u * TM, TM) for u in range(sub)]
    ys = [_dot(o_ref[rows, :], wo_ref[...]) for rows in tiles]
    _post([x_ref[rows, :] for rows in tiles], ys, [m[0] for m in mod_refs],
          g2_ref[...], rwh_ref[...], rwl_ref[...], rb_ref[...], *outs)


def _rope_tables(n_ctx, n_lat):
    rows = n_lat // GRID_W
    row = jnp.repeat(jnp.arange(rows), GRID_W)
    col = jnp.tile(jnp.arange(GRID_W), rows)
    pos = jnp.stack([row, col], axis=-1).astype(F32)
    inv_freq = ROPE_THETA ** (-jnp.arange(ROPE_F, dtype=F32) / ROPE_F)
    ang = pos[:, :, None] * inv_freq
    cos, sin = jnp.cos(ang), jnp.sin(ang)
    cos_t = jnp.concatenate([cos, cos], axis=-1).reshape(n_lat, HEAD_DIM)
    sin_t = jnp.concatenate([-sin, sin], axis=-1).reshape(n_lat, HEAD_DIM)
    cos_t = jnp.concatenate([jnp.ones((n_ctx, HEAD_DIM), F32), cos_t], axis=0)
    sin_t = jnp.concatenate([jnp.zeros((n_ctx, HEAD_DIM), F32), sin_t], axis=0)
    return cos_t, sin_t


def _attn_layer(xu, tinfo, trow, mods, g1, wqkv, q_g, k_g, wo, g2, rwh, rwl, rb, nb, n_ctx):
    rows, d = xu.shape
    s = rows // nb
    n_steps = rows // TM
    n_e = rwh.shape[0]
    kvw = N_KV_HEADS * HEAD_DIM
    n_heads = d // HEAD_DIM
    rep = n_heads // N_KV_HEADS
    cos_t, sin_t = _rope_tables(n_ctx, s - n_ctx)
    grid_spec = pltpu.PrefetchScalarGridSpec(
        num_scalar_prefetch=3,
        grid=(n_steps,),
        in_specs=[
            pl.BlockSpec((TM, d), lambda i, src, mrow, tr: (i, 0)),
            pl.BlockSpec((1, 6, d), lambda i, src, mrow, tr: (mrow[i], 0, 0)),
            _const_spec((1, d)),
            _const_spec((d, d + 2 * kvw)),
            _const_spec((1, HEAD_DIM)),
            _const_spec((1, HEAD_DIM)),
            pl.BlockSpec((TM, HEAD_DIM), lambda i, src, mrow, tr: (tr[i], 0)),
            pl.BlockSpec((TM, HEAD_DIM), lambda i, src, mrow, tr: (tr[i], 0)),
        ],
        out_specs=[
            pl.BlockSpec((TM, d), lambda i, *_: (i, 0)),
            pl.BlockSpec((TM, kvw), lambda i, *_: (i, 0)),
            pl.BlockSpec((TM, kvw), lambda i, *_: (i, 0)),
        ],
    )
    q, k, v = pl.pallas_call(
        _qkv_kernel, grid_spec=grid_spec,
        out_shape=[jax.ShapeDtypeStruct((rows, d), BF16),
                   jax.ShapeDtypeStruct((rows, kvw), BF16),
                   jax.ShapeDtypeStruct((rows, kvw), BF16)],
        compiler_params=_cparams(("arbitrary",)), name="attn_qkv",
    )(tinfo[0], tinfo[1], trow, xu, mods, g1, wqkv.astype(BF16), q_g, k_g, cos_t, sin_t)

    tpb = s // TM
    o = pl.pallas_call(
        functools.partial(_attn_kernel, n_ctx=n_ctx),
        grid=(nb, n_heads // ATTN_HEADS, tpb),
        in_specs=[
            pl.BlockSpec((TM, ATTN_HEADS * HEAD_DIM), lambda b, hp, t: (b * tpb + t, hp)),
            pl.BlockSpec((s, HEAD_DIM), lambda b, hp, t: (b, hp * ATTN_HEADS // rep)),
            pl.BlockSpec((s, HEAD_DIM), lambda b, hp, t: (b, hp * ATTN_HEADS // rep)),
        ],
        out_specs=pl.BlockSpec((TM, ATTN_HEADS * HEAD_DIM), lambda b, hp, t: (b * tpb + t, hp)),
        out_shape=jax.ShapeDtypeStruct((rows, d), BF16),
        compiler_params=_cparams(("arbitrary", "arbitrary", "arbitrary")), name="attn_core",
    )(q, k, v)

    sub = SUB_TILES
    out_specs, out_shapes = _post_specs(n_steps, d, n_e, sub)
    grid_spec = pltpu.PrefetchScalarGridSpec(
        num_scalar_prefetch=2,
        grid=(n_steps // sub,),
        in_specs=[
            pl.BlockSpec((sub * TM, d), lambda i, src, mrow: (i, 0)),
            pl.BlockSpec((sub * TM, d), lambda i, src, mrow: (i, 0)),
        ] + _mod_specs(sub, d) + [
            _const_spec((d, d)),
            _const_spec((1, d)),
            _const_spec((n_e, d)),
            _const_spec((n_e, d)),
            _const_spec((n_e, 1)),
        ],
        out_specs=out_specs,
    )
    return pl.pallas_call(
        functools.partial(_wo_kernel, sub=sub), grid_spec=grid_spec, out_shape=out_shapes,
        compiler_params=_cparams(("arbitrary",)), name="attn_out",
    )(tinfo[0], tinfo[1], xu, o, *([mods] * sub), wo.astype(BF16), g2, rwh, rwl, rb)


def _loc_rows(n_e):
    return -(-(TM * TOP_K + (SEG_ALIGN - 1) * n_e) // 32) * 32


def _store_rows(ref, v):
    r = v.shape[0]
    slabs = ref.shape[0] // r
    for j in range(slabs):
        ref[pl.ds(j, r, stride=slabs), :] = v[:, j * LANES:(j + 1) * LANES]


def _load_rows(ref, r):
    slabs = ref.shape[0] // r
    return jnp.concatenate([ref[pl.ds(j, r, stride=slabs), :] for j in range(slabs)], axis=1)


def _pack_pairs(v, rounded):
    half = v.shape[1] // 2
    lo, hi = v[:, :half], v[:, half:]
    if not rounded:
        lo, hi = lo.astype(BF16).astype(F32), hi.astype(BF16).astype(F32)
    lo = lax.bitcast_convert_type(lo, jnp.uint32)
    hi = lax.bitcast_convert_type(hi, jnp.uint32)
    return (hi & jnp.uint32(0xFFFF0000)) | (lo >> 16)


def _unpack_pairs(p):
    lo = lax.bitcast_convert_type(p << 16, F32)
    hi = lax.bitcast_convert_type(p & jnp.uint32(0xFFFF0000), F32)
    return jnp.concatenate([lo, hi], axis=1).astype(BF16)


def _tile_rows(v):
    return pl.multiple_of(v * (SUBLANES // SEG_ALIGN), SUBLANES)


def _start_segments(t, n_e, loc, len8, dst, make_copy):
    for e in range(n_e):
        n = len8[t * n_e + e]

        @pl.when(n > 0)
        def _(e=e, n=n):
            make_copy(_tile_rows(loc[t * n_e + e]), _tile_rows(dst[t * n_e + e]), _tile_rows(n)).start()


def _dispatch_kernel(loc, len8, dst, tot, padoff, padlen, nused, h2_ref, route_ref, xs_ref,
                     buf, zbuf, sems, zsem, *, n_e, nb_max):
    i = pl.program_id(0)
    n_steps = pl.num_programs(0)
    slot = i % 2
    sub = buf.shape[1]
    lrows = buf.shape[2] // (SUBLANES // SEG_ALIGN)
    iota_p = lax.broadcasted_iota(jnp.int16, (lrows, TM), 0)
    for u in range(sub):
        pos = route_ref[u, 0:TOP_K, :].astype(jnp.int32).astype(jnp.int16)
        perm = jnp.zeros((lrows, TM), BF16)
        for k in range(TOP_K):
            perm = jnp.where(iota_p == pos[k:k + 1], jnp.ones((), BF16), perm)
        _store_rows(buf.at[slot, u], _pack_pairs(_dot(perm, h2_ref[pl.ds(u * TM, TM), :]), rounded=True))

    def seg_copy(s, u):
        def make(a, g, n):
            return pltpu.make_async_copy(buf.at[s, u, pl.ds(a, n)], xs_ref.at[pl.ds(g, n)], sems.at[s, u])
        return make

    def wait_tiles(step, s):
        for u in range(sub):
            seg_copy(s, u)(0, 0, _tile_rows(tot[sub * step + u])).wait()

    for u in range(sub):
        _start_segments(sub * i + u, n_e, loc, len8, dst, seg_copy(slot, u))

    @pl.when(i > 0)
    def _():
        wait_tiles(i - 1, 1 - slot)

    @pl.when(i == n_steps - 1)
    def _():
        wait_tiles(i, slot)

    @pl.when(i == n_steps - 1)
    def _():
        zbuf[...] = jnp.zeros_like(zbuf)

        def zero_rows(g, n):
            return pltpu.make_async_copy(zbuf.at[pl.ds(0, n)], xs_ref.at[pl.ds(g, n)], zsem)

        def pad(e, total):
            n = padlen[e]

            @pl.when(n > 0)
            def _():
                zero_rows(_tile_rows(padoff[e]), _tile_rows(n)).start()
            return total + n

        total = lax.fori_loop(0, n_e, pad, 0)

        def blank(b, carry):
            zero_rows(_tile_rows(b * BLK), zbuf.shape[0]).start()
            return carry

        lax.fori_loop(nused[0], nb_max, blank, 0)
        total = _tile_rows(total + (nb_max - nused[0]) * BLK)

        @pl.when(total > 0)
        def _():
            pltpu.make_async_copy(xs_ref.at[pl.ds(0, total)], xs_ref.at[pl.ds(0, total)], zsem).wait()


def _expert_kernel(blk, blk_e, valid, nused, xs_ref, w1_ref, b1_ref, w2_ref, b2_ref, ys_ref, w1b, w2b):
    i = pl.program_id(0)
    f = w2b.shape[0]

    @pl.when(i >= nused[0])
    def _():
        ys_ref[...] = jnp.zeros_like(ys_ref)

    @pl.when(i < nused[0])
    def _():
        first = jnp.logical_or(i == 0, blk_e[i] != blk_e[jnp.maximum(i - 1, 0)])

        @pl.when(first)
        def _():
            w1b[...] = w1_ref[0, 0].astype(BF16)
            w2b[...] = w2_ref[0, 0].astype(BF16)

        row = lax.broadcasted_iota(jnp.int32, (BLK, 1), 0)
        x = _unpack_pairs(jnp.where(row < valid[i], _load_rows(xs_ref, BLK), jnp.uint32(0)))
        z = _dot(x, w1b[...]) + b1_ref[0, 0]
        glu = jnp.minimum(z[:, :f], SWIGLU_LIMIT)
        lin = jnp.clip(z[:, f:], -SWIGLU_LIMIT, SWIGLU_LIMIT)
        act = glu * _sigmoid(SWIGLU_ALPHA * glu) * (lin + 1.0)
        y = _dot(act.astype(BF16), w2b[...]) + b2_ref[0, 0]
        _store_rows(ys_ref, _pack_pairs(y, rounded=False))


def _combine_kernel(loc, len8, dst, tot, mrow, ys_ref, route_ref, x1_ref, *refs, n_e):
    buf, sems = refs[-2:]
    out_ref = refs[-3]
    mod_refs = refs[:-3]
    i = pl.program_id(0)
    n_steps = pl.num_programs(0)
    slot = i % 2
    sub = buf.shape[1]
    lrows = buf.shape[2] // (SUBLANES // SEG_ALIGN)

    def seg_copy(s, u):
        def make(a, g, n):
            return pltpu.make_async_copy(ys_ref.at[pl.ds(g, n)], buf.at[s, u, pl.ds(a, n)], sems.at[s, u])
        return make

    @pl.when(i == 0)
    def _():
        for u in range(sub):
            _start_segments(u, n_e, loc, len8, dst, seg_copy(0, u))

    @pl.when(i + 1 < n_steps)
    def _():
        for u in range(sub):
            _start_segments(sub * (i + 1) + u, n_e, loc, len8, dst, seg_copy(1 - slot, u))

    iota_p = lax.broadcasted_iota(jnp.int16, (lrows, TM), 0)
    row = lax.broadcasted_iota(jnp.int32, (lrows, 1), 0)
    totals = [tot[sub * i + u] for u in range(sub)]
    for u in range(sub):
        seg_copy(slot, u)(0, 0, _tile_rows(totals[u])).wait()
    for u in range(sub):
        total = totals[u]
        pos = route_ref[u, 0:TOP_K, :].astype(jnp.int32).astype(jnp.int16)
        gate = route_ref[u, TOP_K:2 * TOP_K, :].astype(BF16)
        gt = jnp.zeros((lrows, TM), BF16)
        for k in range(TOP_K):
            gt = jnp.where(iota_p == pos[k:k + 1], gate[k:k + 1], gt)
        ysl = _unpack_pairs(jnp.where(row < total, _load_rows(buf.at[slot, u], lrows), jnp.uint32(0)))
        rows = pl.ds(u * TM, TM)
        out_ref[rows, :] = x1_ref[rows, :] + mod_refs[u][0][5:6] * _dot_tn(gt, ysl)


def _moe(x1, h2, route, cnt, mods, mrow, layer, w1, b1, w2, b2):
    rows, d = x1.shape
    n_tiles = rows // TM
    depth, n_e, _, f2 = w1.shape
    lrows = _loc_rows(n_e)
    nb_max = (rows * TOP_K + (SEG_ALIGN - 1) * n_tiles * n_e) // BLK + n_e
    cap = nb_max * BLK
    slabs = d // 2 // LANES
    assert slabs * SEG_ALIGN == SUBLANES

    i32 = lambda v: v.astype(jnp.int32)
    cnt = i32(cnt.reshape(n_tiles, n_e))
    c8 = (cnt + SEG_ALIGN - 1) // SEG_ALIGN * SEG_ALIGN
    loc = jnp.cumsum(c8, axis=1) - c8
    tot_tile = jnp.sum(c8, axis=1)
    tot = jnp.sum(c8, axis=0)
    nblk = (tot + BLK - 1) // BLK
    blk_end = jnp.cumsum(nblk)
    blk_start = blk_end - nblk
    dst = (blk_start * BLK)[None, :] + jnp.cumsum(c8, axis=0) - c8
    nused = blk_end[-1]
    bid = jnp.arange(nb_max, dtype=jnp.int32)
    bidc = jnp.minimum(bid, nused - 1)
    blk_e = jnp.minimum(jnp.sum(i32(bidc[:, None] >= blk_end[None, :]), axis=1), n_e - 1)
    valid = jnp.clip(tot[blk_e] - (bidc - blk_start[blk_e]) * BLK, 0, BLK)
    padoff = blk_start * BLK + tot
    padlen = nblk * BLK - tot
    loc, len8, dst = i32(loc.reshape(-1)), i32(c8.reshape(-1)), i32(dst.reshape(-1))
    nused = i32(nused.reshape(1))
    sub = SUB_TILES
    row_buf = pltpu.VMEM((2, sub, lrows * slabs, LANES), jnp.uint32)

    xs = pl.pallas_call(
        functools.partial(_dispatch_kernel, n_e=n_e, nb_max=nb_max),
        grid_spec=pltpu.PrefetchScalarGridSpec(
            num_scalar_prefetch=7,
            grid=(n_tiles // sub,),
            in_specs=[
                pl.BlockSpec((sub * TM, d), lambda i, *_: (i, 0)),
                pl.BlockSpec((sub, 2 * TOP_K, TM), lambda i, *_: (i, 0, 0)),
            ],
            out_specs=pl.BlockSpec(memory_space=pl.ANY),
            scratch_shapes=[row_buf, pltpu.VMEM((BLK * slabs, LANES), jnp.uint32),
                            pltpu.SemaphoreType.DMA((2, sub)), pltpu.SemaphoreType.DMA(())],
        ),
        out_shape=jax.ShapeDtypeStruct((cap * slabs, LANES), jnp.uint32),
        compiler_params=_cparams(("arbitrary",)), name="moe_dispatch",
    )(loc, len8, dst, i32(tot_tile), i32(padoff), i32(padlen), nused, h2, route)

    ys = pl.pallas_call(
        _expert_kernel,
        grid_spec=pltpu.PrefetchScalarGridSpec(
            num_scalar_prefetch=4,
            grid=(nb_max,),
            in_specs=[
                pl.BlockSpec((BLK * slabs, LANES), lambda i, blk, be, va, nu: (blk[i], 0)),
                pl.BlockSpec((1, 1, d, f2), lambda i, blk, be, va, nu: (layer, be[i], 0, 0)),
                pl.BlockSpec((1, 1, 1, f2), lambda i, blk, be, va, nu: (layer, be[i], 0, 0)),
                pl.BlockSpec((1, 1, f2 // 2, d), lambda i, blk, be, va, nu: (layer, be[i], 0, 0)),
                pl.BlockSpec((1, 1, 1, d), lambda i, blk, be, va, nu: (layer, be[i], 0, 0)),
            ],
            out_specs=pl.BlockSpec((BLK * slabs, LANES), lambda i, blk, be, va, nu: (i, 0)),
            scratch_shapes=[pltpu.VMEM((d, f2), BF16), pltpu.VMEM((f2 // 2, d), BF16)],
        ),
        out_shape=jax.ShapeDtypeStruct((cap * slabs, LANES), jnp.uint32),
        compiler_params=_cparams(("arbitrary",)), name="moe_experts",
    )(i32(bidc), i32(blk_e), i32(valid), nused, xs, w1, b1.reshape(depth, n_e, 1, f2), w2,
      b2.reshape(depth, n_e, 1, d))

    return pl.pallas_call(
        functools.partial(_combine_kernel, n_e=n_e),
        grid_spec=pltpu.PrefetchScalarGridSpec(
            num_scalar_prefetch=5,
            grid=(n_tiles // sub,),
            in_specs=[
                pl.BlockSpec(memory_space=pl.ANY),
                pl.BlockSpec((sub, 2 * TOP_K, TM), lambda i, *_: (i, 0, 0)),
                pl.BlockSpec((sub * TM, d), lambda i, *_: (i, 0)),
            ] + [pl.BlockSpec((1, 6, d), lambda i, lo, le, ds, to, mr, u=u: (mr[sub * i + u], 0, 0))
                 for u in range(sub)],
            out_specs=pl.BlockSpec((sub * TM, d), lambda i, *_: (i, 0)),
            scratch_shapes=[row_buf, pltpu.SemaphoreType.DMA((2, sub))],
        ),
        out_shape=jax.ShapeDtypeStruct((rows, d), F32),
        compiler_params=_cparams(("arbitrary",)), name="moe_combine",
    )(loc, len8, dst, i32(tot_tile), mrow, ys, route, x1, *([mods] * sub))


def _tile_info(nb, n_ctx, n_lat, latent_only):
    tpb = (n_ctx + n_lat) // TM
    ct = n_ctx // TM
    src, mrow, hp, hn, trow = [], [], [], [], []
    for b in range(nb):
        for j in range(ct if latent_only else 0, tpb):
            is_ctx = j < ct
            src.append(b * tpb + j)
            mrow.append(nb if is_ctx else b)
            hp.append(0 if j in (0, ct) else 1)
            hn.append(0 if j in (ct - 1, tpb - 1) else 1)
            trow.append(j)
    mk = lambda v: jnp.asarray(np.asarray(v, np.int32))
    return (mk(src), mk(mrow), mk(hp), mk(hn)), mk(trow)


def kernel(x, c, ctx, c_ctx, ada_w, ada_b, norm1_g, norm2_g, pool_w, pool_scale, ssm_lam_re, ssm_lam_im,
           ssm_log_dt, ssm_b_re, ssm_b_im, ssm_c_re, ssm_c_im, ssm_d, ssm_glu_w, ssm_glu_b, attn_wqkv,
           attn_q_g, attn_k_g, attn_wo, router_w, router_b, moe_w1, moe_b1, moe_w2, moe_b2):
    nb, n_lat, d = x.shape
    n_ctx = ctx.shape[1]
    depth = ada_w.shape[0]
    n_e = router_w.shape[-1]
    assert nb == 8 and n_ctx % TM == 0 and n_lat % TM == 0 and n_lat % GRID_W == 0
    s = n_ctx + n_lat

    c16 = jnp.concatenate([c, c_ctx[None, :], jnp.zeros((16 - nb - 1, d), F32)], axis=0)
    mods_all = _ada_mods(c16, ada_w, ada_b).reshape(depth, 16, 6, d)

    xu = jnp.concatenate([ctx, x], axis=1).reshape(nb * s, d)
    tinfo_u, trow_u = _tile_info(nb, n_ctx, n_lat, latent_only=False)
    tinfo_l, _ = _tile_info(nb, n_ctx, n_lat, latent_only=True)

    for i in range(depth):
        kind, j = i % 3, i // 3
        last = i == depth - 1
        mods = mods_all[i]
        g1 = norm1_g[i].reshape(1, d)
        g2 = norm2_g[i].reshape(1, d)
        rwt = router_w[i].T
        rwh = rwt.astype(BF16)
        rwl = (rwt - rwh.astype(F32)).astype(BF16)
        rb = router_b[i].reshape(n_e, 1)
        if kind == 0:
            tinfo = tinfo_l if last else tinfo_u
            x1, h2, route, cnt = _pool_layer(xu, tinfo, mods, g1, pool_w[j], pool_scale[j].reshape(1, d),
                                             g2, rwh, rwl, rb)
            mrow = tinfo[1]
        elif kind == 1:
            assert not last
            a_re, a_im, bblk, cblk = _ssm_params(ssm_lam_re[j], ssm_lam_im[j], ssm_log_dt[j], ssm_b_re[j],
                                                 ssm_b_im[j], ssm_c_re[j], ssm_c_im[j])
            y2 = _ssm_scan(xu.reshape(nb, s, d), mods, g1, a_re, a_im, bblk, cblk, n_ctx)
            x1, h2, route, cnt = _glu_layer(xu, y2.reshape(2, nb * s, d), tinfo_u, mods, g1,
                                            ssm_d[j].reshape(1, d), ssm_glu_w[j], ssm_glu_b[j].reshape(1, 2 * d),
                                            g2, rwh, rwl, rb)
            mrow = tinfo_u[1]
        else:
            assert not last
            x1, h2, route, cnt = _attn_layer(xu, tinfo_u, trow_u, mods, g1, attn_wqkv[j],
                                             attn_q_g[j].reshape(1, HEAD_DIM), attn_k_g[j].reshape(1, HEAD_DIM),
                                             attn_wo[j], g2, rwh, rwl, rb, nb, n_ctx)
            mrow = tinfo_u[1]
        xu = _moe(x1, h2, route, cnt, mods, mrow, i, moe_w1, moe_b1, moe_w2, moe_b2)
    if xu.shape[0] == nb * n_lat:
        return xu.reshape(nb, n_lat, d)
    return xu.reshape(nb, s, d)[:, n_ctx:, :]
```

```python
import functools
import math

import numpy as np
import jax
import jax.numpy as jnp
from jax import lax
from jax.experimental import pallas as pl
from jax.experimental.pallas import tpu as pltpu

F32 = jnp.float32
BF16 = jnp.bfloat16

GRID_W = 64
NORM_EPS = 1e-6
POOL_WINDOWS = (2, 4, 8, 16)
SSM_H = 16
SSM_P = 64
SSM_SET = 8
SSM_TT = 64
HEAD_DIM = 128
N_KV_HEADS = 2
ATTN_HEADS = 4
ROPE_F = HEAD_DIM // 4
ROPE_THETA = 10000.0
TOP_K = 4
SWIGLU_ALPHA = 1.702
SWIGLU_LIMIT = 7.0

TM = 256
SUB_TILES = 4
BLK = 512
SUBLANES = 8
LANES = 128
SEG_ALIGN = 2
VMEM_LIMIT = 56 * 1024 * 1024


def _cparams(sem, vmem=VMEM_LIMIT):
    return pltpu.CompilerParams(dimension_semantics=sem, vmem_limit_bytes=vmem)


def _rms(x, g):
    return x * lax.rsqrt(jnp.mean(x * x, axis=-1, keepdims=True) + NORM_EPS) * g


def _sigmoid(x):
    return 1.0 / (1.0 + jnp.exp(-x))


def _dot(a, b):
    return jnp.dot(a, b, preferred_element_type=F32)


def _dot_nt(a, b):
    return lax.dot_general(a, b, (((1,), (1,)), ((), ())), preferred_element_type=F32)


def _dot_tn(a, b):
    return lax.dot_general(a, b, (((0,), (0,)), ((), ())), preferred_element_type=F32)


def _ada_kernel(c_ref, w_ref, b_ref, o_ref):
    c = c_ref[...]
    s = c * _sigmoid(c)
    o_ref[0] = jnp.dot(s, w_ref[0], preferred_element_type=F32,
                       precision=lax.Precision.HIGHEST) + b_ref[0]


def _ada_mods(c16, ada_w, ada_b):
    depth, d, six_d = ada_w.shape
    tn = d
    return pl.pallas_call(
        _ada_kernel,
        grid=(depth, six_d // tn),
        in_specs=[
            pl.BlockSpec((16, d), lambda l, j: (0, 0)),
            pl.BlockSpec((1, d, tn), lambda l, j: (l, 0, j)),
            pl.BlockSpec((1, 1, tn), lambda l, j: (l, 0, j)),
        ],
        out_specs=pl.BlockSpec((1, 16, tn), lambda l, j: (l, 0, j)),
        out_shape=jax.ShapeDtypeStruct((depth, 16, six_d), F32),
        compiler_params=_cparams(("arbitrary", "arbitrary")),
        name="ada_mods",
    )(c16, ada_w, ada_b.reshape(depth, 1, six_d))


def _post_logits(x, y, mod, g2, rwh, rwl, rb, x1_ref, h2_ref):
    x1 = x + mod[2:3] * y
    x1_ref[...] = x1
    h2 = _rms(x1, g2) * (1.0 + mod[4:5]) + mod[3:4]
    h2_ref[...] = h2.astype(BF16)
    hh = h2.astype(BF16)
    hl = (h2 - hh.astype(F32)).astype(BF16)
    return _dot_nt(rwh, hh) + _dot_nt(rwh, hl) + _dot_nt(rwl, hh) + rb


def _post_route(logits, route_ref, cnt_ref):
    n_e = logits.shape[0]
    iota_e = lax.broadcasted_iota(jnp.int32, (n_e, TM), 0)
    vals, onehots = [], []
    l = logits
    for _ in range(TOP_K):
        m = jnp.max(l, axis=0, keepdims=True)
        idx = jnp.min(jnp.where(l == m, iota_e, n_e), axis=0, keepdims=True)
        sel = iota_e == idx
        vals.append(m)
        onehots.append(sel)
        l = jnp.where(sel, -jnp.inf, l)
    ex = [jnp.exp(v - vals[0]) for v in vals]
    den = ex[0] + ex[1] + ex[2] + ex[3]
    gates = [e / den for e in ex]
    member = jnp.zeros((n_e, TM), F32)
    for sel in onehots:
        member = member + jnp.where(sel, 1.0, 0.0)
    r_i = lax.broadcasted_iota(jnp.int32, (TM, TM), 0)
    c_i = lax.broadcasted_iota(jnp.int32, (TM, TM), 1)
    upper = jnp.where(r_i < c_i, 1.0, 0.0).astype(BF16)
    cum = _dot(member.astype(BF16), upper)
    cnt = jnp.sum(member, axis=1, keepdims=True)
    cnt_ref[0] = cnt
    c8 = jnp.floor((cnt + (SEG_ALIGN - 1)) * (1.0 / SEG_ALIGN)) * SEG_ALIGN
    e_r = lax.broadcasted_iota(jnp.int32, (n_e, n_e), 0)
    e_c = lax.broadcasted_iota(jnp.int32, (n_e, n_e), 1)
    lower = jnp.where(e_c < e_r, 1.0, 0.0).astype(BF16)
    seg = _dot(lower, jnp.broadcast_to(c8, (n_e, TM)).astype(BF16))
    base = seg + cum
    rows = []
    for sel in onehots:
        rows.append(jnp.sum(jnp.where(sel, base, 0.0), axis=0, keepdims=True))
    route_ref[0] = jnp.concatenate(rows + gates, axis=0)


def _post_specs(n_tiles, d, n_e, sub=1):
    specs = [
        pl.BlockSpec((sub * TM, d), lambda i, *_: (i, 0)),
        pl.BlockSpec((sub * TM, d), lambda i, *_: (i, 0)),
        pl.BlockSpec((sub, 2 * TOP_K, TM), lambda i, *_: (i, 0, 0)),
        pl.BlockSpec((sub, n_e, 1), lambda i, *_: (i, 0, 0)),
    ]
    shapes = [
        jax.ShapeDtypeStruct((n_tiles * TM, d), F32),
        jax.ShapeDtypeStruct((n_tiles * TM, d), BF16),
        jax.ShapeDtypeStruct((n_tiles, 2 * TOP_K, TM), F32),
        jax.ShapeDtypeStruct((n_tiles, n_e, 1), F32),
    ]
    return specs, shapes


def _post(xs, ys, mods, g2, rwh, rwl, rb, x1_ref, h2_ref, route_ref, cnt_ref):
    logits = []
    for u, (x, y, mod) in enumerate(zip(xs, ys, mods)):
        rows = pl.ds(u * TM, TM)
        logits.append(_post_logits(x, y, mod, g2, rwh, rwl, rb, x1_ref.at[rows], h2_ref.at[rows]))
    for u, l in enumerate(logits):
        _post_route(l, route_ref.at[pl.ds(u, 1)], cnt_ref.at[pl.ds(u, 1)])


def _mod_specs(sub, d):
    return [pl.BlockSpec((1, 6, d), lambda i, src, mrow, *_, u=u: (mrow[sub * i + u], 0, 0))
            for u in range(sub)]


def _const_spec(shape):
    nd = len(shape)
    return pl.BlockSpec(shape, lambda i, *_: (0,) * nd)


def _pool_kernel(src, mrow, hp, hn, *refs, sub):
    x_refs = refs[:sub]
    xp_ref, xn_ref = refs[sub:sub + 2]
    mod_refs = refs[sub + 2:2 * sub + 2]
    g1_ref, pw_ref, ps_ref, g2_ref, rwh_ref, rwl_ref, rb_ref = refs[2 * sub + 2:2 * sub + 9]
    outs = refs[2 * sub + 9:2 * sub + 13]
    hh_scr = refs[2 * sub + 13]
    i = pl.program_id(0)
    d = x_refs[0].shape[1]
    gc = d // len(POOL_WINDOWS)
    halo = SUBLANES
    g1 = g1_ref[...]

    def pre(v, mod):
        return _rms(v, g1) * (1.0 + mod[1:2]) + mod[0:1]

    mods = [m[0] for m in mod_refs]
    xs = [r[...] for r in x_refs]
    hs = [pre(x, mod) for x, mod in zip(xs, mods)]
    row = lax.broadcasted_iota(jnp.int32, (TM, 1), 0)
    ys = []
    for u in range(sub):
        has_prev = hp[sub * i + u] > 0
        has_next = hn[sub * i + u] > 0
        before = pre(xp_ref[...], mods[0]) if u == 0 else hs[u - 1][TM - halo:, :]
        after = pre(xn_ref[...], mods[-1]) if u == sub - 1 else hs[u + 1][:halo, :]
        hh_scr[u, 0:halo, :] = jnp.where(has_prev, before, 0.0)
        hh_scr[u, halo:halo + TM, :] = hs[u]
        hh_scr[u, halo + TM:2 * halo + TM, :] = jnp.where(has_next, after, 0.0)
        parts = []
        for g, win in enumerate(POOL_WINDOWS):
            half = win // 2
            c0 = g * gc
            acc = hh_scr[u, pl.ds(halo - half, TM), c0:c0 + gc]
            for j in range(-half + 1, half):
                acc = acc + hh_scr[u, pl.ds(halo + j, TM), c0:c0 + gc]
            lo_clip = jnp.where(has_prev, 0, jnp.maximum(half - row, 0))
            hi_clip = jnp.where(has_next, 0, jnp.maximum(row + half - TM, 0))
            cnt = (win - lo_clip - hi_clip).astype(F32)
            diff = acc / cnt - hs[u][:, c0:c0 + gc]
            parts.append(_dot(diff.astype(BF16), pw_ref[g]))
        ys.append(jnp.concatenate(parts, axis=1) * ps_ref[...])
    _post(xs, ys, mods, g2_ref[...], rwh_ref[...], rwl_ref[...], rb_ref[...], *outs)


def _pool_layer(xu, tinfo, mods, g1, pool_w, pool_scale, g2, rwh, rwl, rb):
    rows, d = xu.shape
    n_steps = tinfo[0].shape[0]
    n_e = rwh.shape[0]
    gc = pool_w.shape[-1]
    rpb = TM // SUBLANES
    sub = SUB_TILES
    out_specs, out_shapes = _post_specs(n_steps, d, n_e, sub)
    grid_spec = pltpu.PrefetchScalarGridSpec(
        num_scalar_prefetch=4,
        grid=(n_steps // sub,),
        in_specs=[
            pl.BlockSpec((TM, d), lambda i, src, mrow, hp, hn, u=u: (src[sub * i + u], 0)) for u in range(sub)
        ] + [
            pl.BlockSpec((SUBLANES, d),
                         lambda i, src, mrow, hp, hn: (jnp.maximum(src[sub * i] * rpb - 1, 0), 0)),
            pl.BlockSpec((SUBLANES, d),
                         lambda i, src, mrow, hp, hn: (jnp.minimum((src[sub * i + sub - 1] + 1) * rpb,
                                                                   rows // SUBLANES - 1), 0)),
        ] + _mod_specs(sub, d) + [
            _const_spec((1, d)),
            _const_spec((len(POOL_WINDOWS), gc, gc)),
            _const_spec((1, d)),
            _const_spec((1, d)),
            _const_spec((n_e, d)),
            _const_spec((n_e, d)),
            _const_spec((n_e, 1)),
        ],
        out_specs=out_specs,
        scratch_shapes=[pltpu.VMEM((sub, TM + 2 * SUBLANES, d), F32)],
    )
    return pl.pallas_call(
        functools.partial(_pool_kernel, sub=sub), grid_spec=grid_spec, out_shape=out_shapes,
        compiler_params=_cparams(("arbitrary",)), name="pool_mixer",
    )(*tinfo, *([xu] * (sub + 2)), *([mods] * sub), g1, pool_w.astype(BF16), pool_scale, g2, rwh, rwl, rb)


def _ssm_kernel(tile_of, x_ref, mod_ref, g1_ref, are_ref, aim_ref, bb_ref, cb_ref, y_ref,
                u_scr, y_scr, h_scr, *xs_scrs, n_ctx_tiles):
    dr = pl.program_id(0)
    i = pl.program_id(1)
    nb, tt, d = x_ref.shape
    n_sets = bb_ref.shape[1]
    sw = bb_ref.shape[3]
    hw = sw // 2
    uw = bb_ref.shape[2]
    is_ctx = tile_of[dr * pl.num_programs(1) + i] < n_ctx_tiles

    @pl.when(i == 0)
    def _():
        h_scr[...] = jnp.zeros_like(h_scr)

    g1 = g1_ref[...]
    for b in range(nb):
        shift = jnp.where(is_ctx, mod_ref[nb, 0:1, :], mod_ref[b, 0:1, :])
        scale = jnp.where(is_ctx, mod_ref[nb, 1:2, :], mod_ref[b, 1:2, :])
        hb = _rms(x_ref[b], g1) * (1.0 + scale) + shift
        for j in range(n_sets):
            u_scr[j, pl.ds(b, tt, stride=nb), :] = hb[:, j * uw:(j + 1) * uw]
    for j in range(n_sets):
        xs_scrs[j][...] = _dot(u_scr[j].astype(BF16), bb_ref[0, j])
    for j in range(n_sets):
        xs = xs_scrs[j]
        ar = jnp.broadcast_to(are_ref[0, j:j + 1, :], (nb, hw))
        ai = jnp.broadcast_to(aim_ref[0, j:j + 1, :], (nb, hw))
        hr = h_scr[:, j * sw:j * sw + hw]
        hi = h_scr[:, j * sw + hw:(j + 1) * sw]
        for step in range(tt):
            t = step + dr * (tt - 1 - 2 * step)
            r0 = pl.multiple_of(t * nb, nb)
            nhr = ar * hr - ai * hi + xs[pl.ds(r0, nb), 0:hw]
            nhi = ar * hi + ai * hr + xs[pl.ds(r0, nb), hw:sw]
            xs[pl.ds(r0, nb), 0:hw] = nhr
            xs[pl.ds(r0, nb), hw:sw] = nhi
            hr, hi = nhr, nhi
        h_scr[:, j * sw:j * sw + hw] = hr
        h_scr[:, j * sw + hw:(j + 1) * sw] = hi
        y_scr[j] = _dot(xs[...].astype(BF16), cb_ref[0, j])
    for b in range(nb):
        for j in range(n_sets):
            y_ref[0, b, :, j * uw:(j + 1) * uw] = y_scr[j, pl.ds(b, tt, stride=nb), :]


def _ssm_scan(x3, mods, g1, a_re, a_im, bblk, cblk, n_ctx):
    nb, s, d = x3.shape
    tt = SSM_TT
    nt = s // tt
    nct = n_ctx // tt
    fwd = np.arange(nt)
    bwd = np.concatenate([np.arange(nct)[::-1], np.arange(nct, nt)[::-1]])
    tile_of = jnp.asarray(np.concatenate([fwd, bwd]), jnp.int32)
    n_sets, uw, sw = bblk.shape[1:]
    grid_spec = pltpu.PrefetchScalarGridSpec(
        num_scalar_prefetch=1,
        grid=(2, nt),
        in_specs=[
            pl.BlockSpec((nb, tt, d), lambda dr, i, to: (0, to[dr * nt + i], 0)),
            pl.BlockSpec(mods.shape, lambda dr, i, to: (0, 0, 0)),
            pl.BlockSpec((1, d), lambda dr, i, to: (0, 0)),
            pl.BlockSpec((1, n_sets, sw // 2), lambda dr, i, to: (dr, 0, 0)),
            pl.BlockSpec((1, n_sets, sw // 2), lambda dr, i, to: (dr, 0, 0)),
            pl.BlockSpec((1, n_sets, uw, sw), lambda dr, i, to: (dr, 0, 0, 0)),
            pl.BlockSpec((1, n_sets, sw, uw), lambda dr, i, to: (dr, 0, 0, 0)),
        ],
        out_specs=pl.BlockSpec((1, nb, tt, d), lambda dr, i, to: (dr, 0, to[dr * nt + i], 0)),
        scratch_shapes=[
            pltpu.VMEM((n_sets, tt * nb, uw), F32),
            pltpu.VMEM((n_sets, tt * nb, uw), F32),
            pltpu.VMEM((nb, n_sets * sw), F32),
        ] + [pltpu.VMEM((tt * nb, sw), F32) for _ in range(n_sets)],
    )
    return pl.pallas_call(
        functools.partial(_ssm_kernel, n_ctx_tiles=nct), grid_spec=grid_spec,
        out_shape=jax.ShapeDtypeStruct((2, nb, s, d), F32),
        compiler_params=_cparams(("arbitrary", "arbitrary")), name="s5_scan",
    )(tile_of, x3, mods, g1, a_re, a_im, bblk, cblk)


def _ssm_params(lam_re, lam_im, log_dt, b_re, b_im, c_re, c_im):
    g, p = lam_re.shape[1:]
    h = b_re.shape[-1]
    ns = g // SSM_SET
    eye = jnp.eye(SSM_SET, dtype=F32)
    outs = []
    for dr in range(2):
        lr, li = lam_re[dr].astype(F32), lam_im[dr].astype(F32)
        br, bi = b_re[dr].astype(F32), b_im[dr].astype(F32)
        dt = jnp.exp(log_dt[dr].astype(F32))[:, None]
        zr, zi = lr * dt, li * dt
        mag = jnp.exp(zr)
        ar, ai = mag * jnp.cos(zi), mag * jnp.sin(zi)
        den = lr * lr + li * li
        cr = ((ar - 1.0) * lr + ai * li) / den
        ci = (ai * lr - (ar - 1.0) * li) / den
        bbr = cr[..., None] * br - ci[..., None] * bi
        bbi = cr[..., None] * bi + ci[..., None] * br

        def blk_b(w):
            w = jnp.transpose(w, (0, 2, 1)).reshape(ns, SSM_SET, h, p)
            return jnp.einsum('ab,jahp->jahbp', eye, w).reshape(ns, SSM_SET * h, SSM_SET * p)

        def blk_c(w):
            w = jnp.transpose(w.reshape(ns, SSM_SET, h, p), (0, 1, 3, 2))
            return jnp.einsum('ab,japh->japbh', eye, w).reshape(ns, SSM_SET * p, SSM_SET * h)

        bblk = jnp.concatenate([blk_b(bbr), blk_b(bbi)], axis=2)
        cblk = jnp.concatenate([blk_c(c_re[dr].astype(F32)), -blk_c(c_im[dr].astype(F32))], axis=1)
        outs.append((ar.reshape(ns, SSM_SET * p), ai.reshape(ns, SSM_SET * p), bblk, cblk))
    a_re = jnp.stack([o[0] for o in outs])
    a_im = jnp.stack([o[1] for o in outs])
    bblk = jnp.stack([o[2] for o in outs]).astype(BF16)
    cblk = jnp.stack([o[3] for o in outs]).astype(BF16)
    return a_re, a_im, bblk, cblk


def _glu_kernel(src, mrow, x_ref, yf_ref, yb_ref, *refs, sub):
    mod_refs = refs[:sub]
    g1_ref, dsk_ref, gw_ref, gb_ref, g2_ref, rwh_ref, rwl_ref, rb_ref = refs[sub:sub + 8]
    outs = refs[sub + 8:]
    d = x_ref.shape[1]
    xs, ys, mods = [], [], []
    for u in range(sub):
        rows = pl.ds(u * TM, TM)
        mod = mod_refs[u][0]
        x = x_ref[rows, :]
        h = _rms(x, g1_ref[...]) * (1.0 + mod[1:2]) + mod[0:1]
        y = yf_ref[0, rows, :] + yb_ref[0, rows, :] + dsk_ref[...] * h
        gl = 0.5 * y * (1.0 + jnp.tanh(math.sqrt(2.0 / math.pi) * (y + 0.044715 * (y * y * y))))
        z = _dot(gl.astype(BF16), gw_ref[...]) + gb_ref[...]
        xs.append(x)
        ys.append(z[:, :d] * _sigmoid(z[:, d:]))
        mods.append(mod)
    _post(xs, ys, mods, g2_ref[...], rwh_ref[...], rwl_ref[...], rb_ref[...], *outs)


def _glu_layer(xu, y2, tinfo, mods, g1, d_skip, glu_w, glu_b, g2, rwh, rwl, rb):
    rows, d = xu.shape
    n_steps = rows // TM
    n_e = rwh.shape[0]
    sub = SUB_TILES
    out_specs, out_shapes = _post_specs(n_steps, d, n_e, sub)
    grid_spec = pltpu.PrefetchScalarGridSpec(
        num_scalar_prefetch=2,
        grid=(n_steps // sub,),
        in_specs=[
            pl.BlockSpec((sub * TM, d), lambda i, src, mrow: (i, 0)),
            pl.BlockSpec((1, sub * TM, d), lambda i, src, mrow: (0, i, 0)),
            pl.BlockSpec((1, sub * TM, d), lambda i, src, mrow: (1, i, 0)),
        ] + _mod_specs(sub, d) + [
            _const_spec((1, d)),
            _const_spec((1, d)),
            _const_spec((d, 2 * d)),
            _const_spec((1, 2 * d)),
            _const_spec((1, d)),
            _const_spec((n_e, d)),
            _const_spec((n_e, d)),
            _const_spec((n_e, 1)),
        ],
        out_specs=out_specs,
    )
    return pl.pallas_call(
        functools.partial(_glu_kernel, sub=sub), grid_spec=grid_spec, out_shape=out_shapes,
        compiler_params=_cparams(("arbitrary",)), name="s5_glu",
    )(tinfo[0], tinfo[1], xu, y2, y2, *([mods] * sub), g1, d_skip, glu_w.astype(BF16), glu_b, g2, rwh, rwl, rb)


def _rope(v, cos, sin_signed, first_half):
    partner = jnp.where(first_half, pltpu.roll(v, HEAD_DIM - HEAD_DIM // 4, axis=1),
                        pltpu.roll(v, HEAD_DIM // 4, axis=1))
    return v * cos + partner * sin_signed


def _qkv_kernel(src, mrow, trow, x_ref, *refs, sub):
    mod_refs = refs[:sub]
    cos_refs = refs[sub:2 * sub]
    sin_refs = refs[2 * sub:3 * sub]
    g1_ref, w_ref, qg_ref, kg_ref, q_ref, k_ref, v_ref = refs[3 * sub:]
    d = x_ref.shape[1]
    kvw = k_ref.shape[1]
    lane = lax.broadcasted_iota(jnp.int32, (TM, HEAD_DIM), 1)
    first_half = (lane % (HEAD_DIM // 2)) < (HEAD_DIM // 4)
    q_scale = HEAD_DIM ** -0.5
    zs = []
    for u in range(sub):
        mod = mod_refs[u][0]
        h = _rms(x_ref[pl.ds(u * TM, TM), :], g1_ref[...]) * (1.0 + mod[1:2]) + mod[0:1]
        zs.append(_dot(h.astype(BF16), w_ref[...]))
    for u, z in enumerate(zs):
        rows = pl.ds(u * TM, TM)
        cos = cos_refs[u][...]
        sin = sin_refs[u][...]
        for hd in range(d // HEAD_DIM):
            zh = z[:, hd * HEAD_DIM:(hd + 1) * HEAD_DIM]
            zh = _rope(_rms(zh, qg_ref[...]), cos, sin, first_half) * q_scale
            q_ref[rows, hd * HEAD_DIM:(hd + 1) * HEAD_DIM] = zh.astype(BF16)
        for hd in range(kvw // HEAD_DIM):
            zh = z[:, d + hd * HEAD_DIM:d + (hd + 1) * HEAD_DIM]
            zh = _rope(_rms(zh, kg_ref[...]), cos, sin, first_half)
            k_ref[rows, hd * HEAD_DIM:(hd + 1) * HEAD_DIM] = zh.astype(BF16)
        v_ref[rows, :] = z[:, d + kvw:].astype(BF16)


def _attn_kernel(q_ref, k_ref, v_ref, o_ref, *, n_ctx):
    qt = pl.program_id(2)

    def attend(n_keys):
        for hd in range(ATTN_HEADS):
            cols = slice(hd * HEAD_DIM, (hd + 1) * HEAD_DIM)
            s = _dot_nt(q_ref[:, cols], k_ref[0:n_keys, :])
            m = jnp.max(s, axis=-1, keepdims=True)
            p = jnp.exp(s - m)
            den = jnp.sum(p, axis=-1, keepdims=True)
            o_ref[:, cols] = (_dot(p.astype(BF16), v_ref[0:n_keys, :]) / den).astype(BF16)

    @pl.when(qt < n_ctx // TM)
    def _():
        attend(n_ctx)

    @pl.when(qt >= n_ctx // TM)
    def _():
        attend(k_ref.shape[0])


def _wo_kernel(src, mrow, x_ref, o_ref, *refs, sub):
    mod_refs = refs[:sub]
    wo_ref, g2_ref, rwh_ref, rwl_ref, rb_ref = refs[sub:sub + 5]
    outs = refs[sub + 5:]
    tiles = [pl.ds(u * TM, TM) for u in range(sub)]
    ys = [_dot(o_ref[rows, :], wo_ref[...]) for rows in tiles]
    _post([x_ref[rows, :] for rows in tiles], ys, [m[0] for m in mod_refs],
          g2_ref[...], rwh_ref[...], rwl_ref[...], rb_ref[...], *outs)


def _rope_tables(n_ctx, n_lat):
    rows = n_lat // GRID_W
    row = jnp.repeat(jnp.arange(rows), GRID_W)
    col = jnp.tile(jnp.arange(GRID_W), rows)
    pos = jnp.stack([row, col], axis=-1).astype(F32)
    inv_freq = ROPE_THETA ** (-jnp.arange(ROPE_F, dtype=F32) / ROPE_F)
    ang = pos[:, :, None] * inv_freq
    cos, sin = jnp.cos(ang), jnp.sin(ang)
    cos_t = jnp.concatenate([cos, cos], axis=-1).reshape(n_lat, HEAD_DIM)
    sin_t = jnp.concatenate([-sin, sin], axis=-1).reshape(n_lat, HEAD_DIM)
    cos_t = jnp.concatenate([jnp.ones((n_ctx, HEAD_DIM), F32), cos_t], axis=0)
    sin_t = jnp.concatenate([jnp.zeros((n_ctx, HEAD_DIM), F32), sin_t], axis=0)
    return cos_t, sin_t


def _attn_layer(xu, tinfo, trow, mods, g1, wqkv, q_g, k_g, wo, g2, rwh, rwl, rb, nb, n_ctx):
    rows, d = xu.shape
    s = rows // nb
    n_steps = rows // TM
    n_e = rwh.shape[0]
    kvw = N_KV_HEADS * HEAD_DIM
    n_heads = d // HEAD_DIM
    rep = n_heads // N_KV_HEADS
    cos_t, sin_t = _rope_tables(n_ctx, s - n_ctx)
    sub = SUB_TILES
    table_specs = [pl.BlockSpec((TM, HEAD_DIM), lambda i, src, mrow, tr, u=u: (tr[sub * i + u], 0))
                   for u in range(sub)]
    grid_spec = pltpu.PrefetchScalarGridSpec(
        num_scalar_prefetch=3,
        grid=(n_steps // sub,),
        in_specs=[pl.BlockSpec((sub * TM, d), lambda i, src, mrow, tr: (i, 0))]
        + _mod_specs(sub, d) + table_specs + table_specs + [
            _const_spec((1, d)),
            _const_spec((d, d + 2 * kvw)),
            _const_spec((1, HEAD_DIM)),
            _const_spec((1, HEAD_DIM)),
        ],
        out_specs=[
            pl.BlockSpec((sub * TM, d), lambda i, *_: (i, 0)),
            pl.BlockSpec((sub * TM, kvw), lambda i, *_: (i, 0)),
            pl.BlockSpec((sub * TM, kvw), lambda i, *_: (i, 0)),
        ],
    )
    q, k, v = pl.pallas_call(
        functools.partial(_qkv_kernel, sub=sub), grid_spec=grid_spec,
        out_shape=[jax.ShapeDtypeStruct((rows, d), BF16),
                   jax.ShapeDtypeStruct((rows, kvw), BF16),
                   jax.ShapeDtypeStruct((rows, kvw), BF16)],
        compiler_params=_cparams(("arbitrary",)), name="attn_qkv",
    )(tinfo[0], tinfo[1], trow, xu, *([mods] * sub), *([cos_t] * sub), *([sin_t] * sub),
      g1, wqkv.astype(BF16), q_g, k_g)

    tpb = s // TM
    o = pl.pallas_call(
        functools.partial(_attn_kernel, n_ctx=n_ctx),
        grid=(nb, n_heads // ATTN_HEADS, tpb),
        in_specs=[
            pl.BlockSpec((TM, ATTN_HEADS * HEAD_DIM), lambda b, hp, t: (b * tpb + t, hp)),
            pl.BlockSpec((s, HEAD_DIM), lambda b, hp, t: (b, hp * ATTN_HEADS // rep)),
            pl.BlockSpec((s, HEAD_DIM), lambda b, hp, t: (b, hp * ATTN_HEADS // rep)),
        ],
        out_specs=pl.BlockSpec((TM, ATTN_HEADS * HEAD_DIM), lambda b, hp, t: (b * tpb + t, hp)),
        out_shape=jax.ShapeDtypeStruct((rows, d), BF16),
        compiler_params=_cparams(("arbitrary", "arbitrary", "arbitrary")), name="attn_core",
    )(q, k, v)

    sub = SUB_TILES
    out_specs, out_shapes = _post_specs(n_steps, d, n_e, sub)
    grid_spec = pltpu.PrefetchScalarGridSpec(
        num_scalar_prefetch=2,
        grid=(n_steps // sub,),
        in_specs=[
            pl.BlockSpec((sub * TM, d), lambda i, src, mrow: (i, 0)),
            pl.BlockSpec((sub * TM, d), lambda i, src, mrow: (i, 0)),
        ] + _mod_specs(sub, d) + [
            _const_spec((d, d)),
            _const_spec((1, d)),
            _const_spec((n_e, d)),
            _const_spec((n_e, d)),
            _const_spec((n_e, 1)),
        ],
        out_specs=out_specs,
    )
    return pl.pallas_call(
        functools.partial(_wo_kernel, sub=sub), grid_spec=grid_spec, out_shape=out_shapes,
        compiler_params=_cparams(("arbitrary",)), name="attn_out",
    )(tinfo[0], tinfo[1], xu, o, *([mods] * sub), wo.astype(BF16), g2, rwh, rwl, rb)


def _loc_rows(n_e):
    return -(-(TM * TOP_K + (SEG_ALIGN - 1) * n_e) // 32) * 32


def _store_rows(ref, v):
    r = v.shape[0]
    slabs = ref.shape[0] // r
    for j in range(slabs):
        ref[pl.ds(j, r, stride=slabs), :] = v[:, j * LANES:(j + 1) * LANES]


def _load_rows(ref, r):
    slabs = ref.shape[0] // r
    return jnp.concatenate([ref[pl.ds(j, r, stride=slabs), :] for j in range(slabs)], axis=1)


def _pack_pairs(v, rounded):
    half = v.shape[1] // 2
    lo, hi = v[:, :half], v[:, half:]
    if not rounded:
        lo, hi = lo.astype(BF16).astype(F32), hi.astype(BF16).astype(F32)
    lo = lax.bitcast_convert_type(lo, jnp.uint32)
    hi = lax.bitcast_convert_type(hi, jnp.uint32)
    return (hi & jnp.uint32(0xFFFF0000)) | (lo >> 16)


def _unpack_pairs(p):
    lo = lax.bitcast_convert_type(p << 16, F32)
    hi = lax.bitcast_convert_type(p & jnp.uint32(0xFFFF0000), F32)
    return jnp.concatenate([lo, hi], axis=1).astype(BF16)


def _tile_rows(v):
    return pl.multiple_of(v * (SUBLANES // SEG_ALIGN), SUBLANES)


def _start_segments(t, n_e, loc, len8, dst, make_copy):
    for e in range(n_e):
        n = len8[t * n_e + e]

        @pl.when(n > 0)
        def _(e=e, n=n):
            make_copy(_tile_rows(loc[t * n_e + e]), _tile_rows(dst[t * n_e + e]), _tile_rows(n)).start()


def _dispatch_kernel(loc, len8, dst, tot, padoff, padlen, nused, h2_ref, route_ref, xs_ref,
                     buf, zbuf, sems, zsem, *, n_e, nb_max):
    i = pl.program_id(0)
    n_steps = pl.num_programs(0)
    slot = i % 2
    sub = buf.shape[1]
    lrows = buf.shape[2] // (SUBLANES // SEG_ALIGN)
    iota_p = lax.broadcasted_iota(jnp.int16, (lrows, TM), 0)
    for u in range(sub):
        pos = route_ref[u, 0:TOP_K, :].astype(jnp.int32).astype(jnp.int16)
        perm = jnp.zeros((lrows, TM), BF16)
        for k in range(TOP_K):
            perm = jnp.where(iota_p == pos[k:k + 1], jnp.ones((), BF16), perm)
        _store_rows(buf.at[slot, u], _pack_pairs(_dot(perm, h2_ref[pl.ds(u * TM, TM), :]), rounded=True))

    def seg_copy(s, u):
        def make(a, g, n):
            return pltpu.make_async_copy(buf.at[s, u, pl.ds(a, n)], xs_ref.at[pl.ds(g, n)], sems.at[s, u])
        return make

    def wait_tiles(step, s):
        for u in range(sub):
            seg_copy(s, u)(0, 0, _tile_rows(tot[sub * step + u])).wait()

    for u in range(sub):
        _start_segments(sub * i + u, n_e, loc, len8, dst, seg_copy(slot, u))

    @pl.when(i > 0)
    def _():
        wait_tiles(i - 1, 1 - slot)

    @pl.when(i == n_steps - 1)
    def _():
        wait_tiles(i, slot)

    @pl.when(i == n_steps - 1)
    def _():
        zbuf[...] = jnp.zeros_like(zbuf)

        def zero_rows(g, n):
            return pltpu.make_async_copy(zbuf.at[pl.ds(0, n)], xs_ref.at[pl.ds(g, n)], zsem)

        def pad(e, total):
            n = padlen[e]

            @pl.when(n > 0)
            def _():
                zero_rows(_tile_rows(padoff[e]), _tile_rows(n)).start()
            return total + n

        total = lax.fori_loop(0, n_e, pad, 0)

        def blank(b, carry):
            zero_rows(_tile_rows(b * BLK), zbuf.shape[0]).start()
            return carry

        lax.fori_loop(nused[0], nb_max, blank, 0)
        total = _tile_rows(total + (nb_max - nused[0]) * BLK)

        @pl.when(total > 0)
        def _():
            pltpu.make_async_copy(xs_ref.at[pl.ds(0, total)], xs_ref.at[pl.ds(0, total)], zsem).wait()


def _expert_kernel(blk, blk_e, valid, nused, xs_ref, w1_ref, b1_ref, w2_ref, b2_ref, ys_ref, w1b, w2b):
    i = pl.program_id(0)
    f = w2b.shape[0]

    @pl.when(i >= nused[0])
    def _():
        ys_ref[...] = jnp.zeros_like(ys_ref)

    @pl.when(i < nused[0])
    def _():
        first = jnp.logical_or(i == 0, blk_e[i] != blk_e[jnp.maximum(i - 1, 0)])

        @pl.when(first)
        def _():
            w1b[...] = w1_ref[0, 0].astype(BF16)
            w2b[...] = w2_ref[0, 0].astype(BF16)

        row = lax.broadcasted_iota(jnp.int32, (BLK, 1), 0)
        x = _unpack_pairs(jnp.where(row < valid[i], _load_rows(xs_ref, BLK), jnp.uint32(0)))
        z = _dot(x, w1b[...]) + b1_ref[0, 0]
        glu = jnp.minimum(z[:, :f], SWIGLU_LIMIT)
        lin = jnp.clip(z[:, f:], -SWIGLU_LIMIT, SWIGLU_LIMIT)
        act = glu * _sigmoid(SWIGLU_ALPHA * glu) * (lin + 1.0)
        y = _dot(act.astype(BF16), w2b[...]) + b2_ref[0, 0]
        _store_rows(ys_ref, _pack_pairs(y, rounded=False))


def _combine_kernel(loc, len8, dst, tot, mrow, ys_ref, route_ref, x1_ref, *refs, n_e):
    buf, sems = refs[-2:]
    out_ref = refs[-3]
    mod_refs = refs[:-3]
    i = pl.program_id(0)
    n_steps = pl.num_programs(0)
    slot = i % 2
    sub = buf.shape[1]
    lrows = buf.shape[2] // (SUBLANES // SEG_ALIGN)

    def seg_copy(s, u):
        def make(a, g, n):
            return pltpu.make_async_copy(ys_ref.at[pl.ds(g, n)], buf.at[s, u, pl.ds(a, n)], sems.at[s, u])
        return make

    @pl.when(i == 0)
    def _():
        for u in range(sub):
            _start_segments(u, n_e, loc, len8, dst, seg_copy(0, u))

    @pl.when(i + 1 < n_steps)
    def _():
        for u in range(sub):
            _start_segments(sub * (i + 1) + u, n_e, loc, len8, dst, seg_copy(1 - slot, u))

    iota_p = lax.broadcasted_iota(jnp.int16, (lrows, TM), 0)
    row = lax.broadcasted_iota(jnp.int32, (lrows, 1), 0)
    totals = [tot[sub * i + u] for u in range(sub)]
    for u in range(sub):
        seg_copy(slot, u)(0, 0, _tile_rows(totals[u])).wait()
    for u in range(sub):
        total = totals[u]
        pos = route_ref[u, 0:TOP_K, :].astype(jnp.int32).astype(jnp.int16)
        gate = route_ref[u, TOP_K:2 * TOP_K, :].astype(BF16)
        gt = jnp.zeros((lrows, TM), BF16)
        for k in range(TOP_K):
            gt = jnp.where(iota_p == pos[k:k + 1], gate[k:k + 1], gt)
        ysl = _unpack_pairs(jnp.where(row < total, _load_rows(buf.at[slot, u], lrows), jnp.uint32(0)))
        rows = pl.ds(u * TM, TM)
        out_ref[rows, :] = x1_ref[rows, :] + mod_refs[u][0][5:6] * _dot_tn(gt, ysl)


def _moe(x1, h2, route, cnt, mods, mrow, layer, w1, b1, w2, b2):
    rows, d = x1.shape
    n_tiles = rows // TM
    depth, n_e, _, f2 = w1.shape
    lrows = _loc_rows(n_e)
    nb_max = (rows * TOP_K + (SEG_ALIGN - 1) * n_tiles * n_e) // BLK + n_e
    cap = nb_max * BLK
    slabs = d // 2 // LANES
    assert slabs * SEG_ALIGN == SUBLANES

    i32 = lambda v: v.astype(jnp.int32)
    cnt = i32(cnt.reshape(n_tiles, n_e))
    c8 = (cnt + SEG_ALIGN - 1) // SEG_ALIGN * SEG_ALIGN
    loc = jnp.cumsum(c8, axis=1) - c8
    tot_tile = jnp.sum(c8, axis=1)
    tot = jnp.sum(c8, axis=0)
    nblk = (tot + BLK - 1) // BLK
    blk_end = jnp.cumsum(nblk)
    blk_start = blk_end - nblk
    dst = (blk_start * BLK)[None, :] + jnp.cumsum(c8, axis=0) - c8
    nused = blk_end[-1]
    bid = jnp.arange(nb_max, dtype=jnp.int32)
    bidc = jnp.minimum(bid, nused - 1)
    blk_e = jnp.minimum(jnp.sum(i32(bidc[:, None] >= blk_end[None, :]), axis=1), n_e - 1)
    valid = jnp.clip(tot[blk_e] - (bidc - blk_start[blk_e]) * BLK, 0, BLK)
    padoff = blk_start * BLK + tot
    padlen = nblk * BLK - tot
    loc, len8, dst = i32(loc.reshape(-1)), i32(c8.reshape(-1)), i32(dst.reshape(-1))
    nused = i32(nused.reshape(1))
    sub = SUB_TILES
    row_buf = pltpu.VMEM((2, sub, lrows * slabs, LANES), jnp.uint32)

    xs = pl.pallas_call(
        functools.partial(_dispatch_kernel, n_e=n_e, nb_max=nb_max),
        grid_spec=pltpu.PrefetchScalarGridSpec(
            num_scalar_prefetch=7,
            grid=(n_tiles // sub,),
            in_specs=[
                pl.BlockSpec((sub * TM, d), lambda i, *_: (i, 0)),
                pl.BlockSpec((sub, 2 * TOP_K, TM), lambda i, *_: (i, 0, 0)),
            ],
            out_specs=pl.BlockSpec(memory_space=pl.ANY),
            scratch_shapes=[row_buf, pltpu.VMEM((BLK * slabs, LANES), jnp.uint32),
                            pltpu.SemaphoreType.DMA((2, sub)), pltpu.SemaphoreType.DMA(())],
        ),
        out_shape=jax.ShapeDtypeStruct((cap * slabs, LANES), jnp.uint32),
        compiler_params=_cparams(("arbitrary",)), name="moe_dispatch",
    )(loc, len8, dst, i32(tot_tile), i32(padoff), i32(padlen), nused, h2, route)

    ys = pl.pallas_call(
        _expert_kernel,
        grid_spec=pltpu.PrefetchScalarGridSpec(
            num_scalar_prefetch=4,
            grid=(nb_max,),
            in_specs=[
                pl.BlockSpec((BLK * slabs, LANES), lambda i, blk, be, va, nu: (blk[i], 0)),
                pl.BlockSpec((1, 1, d, f2), lambda i, blk, be, va, nu: (layer, be[i], 0, 0)),
                pl.BlockSpec((1, 1, 1, f2), lambda i, blk, be, va, nu: (layer, be[i], 0, 0)),
                pl.BlockSpec((1, 1, f2 // 2, d), lambda i, blk, be, va, nu: (layer, be[i], 0, 0)),
                pl.BlockSpec((1, 1, 1, d), lambda i, blk, be, va, nu: (layer, be[i], 0, 0)),
            ],
            out_specs=pl.BlockSpec((BLK * slabs, LANES), lambda i, blk, be, va, nu: (i, 0)),
            scratch_shapes=[pltpu.VMEM((d, f2), BF16), pltpu.VMEM((f2 // 2, d), BF16)],
        ),
        out_shape=jax.ShapeDtypeStruct((cap * slabs, LANES), jnp.uint32),
        compiler_params=_cparams(("arbitrary",)), name="moe_experts",
    )(i32(bidc), i32(blk_e), i32(valid), nused, xs, w1, b1.reshape(depth, n_e, 1, f2), w2,
      b2.reshape(depth, n_e, 1, d))

    return pl.pallas_call(
        functools.partial(_combine_kernel, n_e=n_e),
        grid_spec=pltpu.PrefetchScalarGridSpec(
            num_scalar_prefetch=5,
            grid=(n_tiles // sub,),
            in_specs=[
                pl.BlockSpec(memory_space=pl.ANY),
                pl.BlockSpec((sub, 2 * TOP_K, TM), lambda i, *_: (i, 0, 0)),
                pl.BlockSpec((sub * TM, d), lambda i, *_: (i, 0)),
            ] + [pl.BlockSpec((1, 6, d), lambda i, lo, le, ds, to, mr, u=u: (mr[sub * i + u], 0, 0))
                 for u in range(sub)],
            out_specs=pl.BlockSpec((sub * TM, d), lambda i, *_: (i, 0)),
            scratch_shapes=[row_buf, pltpu.SemaphoreType.DMA((2, sub))],
        ),
        out_shape=jax.ShapeDtypeStruct((rows, d), F32),
        compiler_params=_cparams(("arbitrary",)), name="moe_combine",
    )(loc, len8, dst, i32(tot_tile), mrow, ys, route, x1, *([mods] * sub))


def _tile_info(nb, n_ctx, n_lat, latent_only):
    tpb = (n_ctx + n_lat) // TM
    ct = n_ctx // TM
    src, mrow, hp, hn, trow = [], [], [], [], []
    for b in range(nb):
        for j in range(ct if latent_only else 0, tpb):
            is_ctx = j < ct
            src.append(b * tpb + j)
            mrow.append(nb if is_ctx else b)
            hp.append(0 if j in (0, ct) else 1)
            hn.append(0 if j in (ct - 1, tpb - 1) else 1)
            trow.append(j)
    mk = lambda v: jnp.asarray(np.asarray(v, np.int32))
    return (mk(src), mk(mrow), mk(hp), mk(hn)), mk(trow)


def kernel(x, c, ctx, c_ctx, ada_w, ada_b, norm1_g, norm2_g, pool_w, pool_scale, ssm_lam_re, ssm_lam_im,
           ssm_log_dt, ssm_b_re, ssm_b_im, ssm_c_re, ssm_c_im, ssm_d, ssm_glu_w, ssm_glu_b, attn_wqkv,
           attn_q_g, attn_k_g, attn_wo, router_w, router_b, moe_w1, moe_b1, moe_w2, moe_b2):
    nb, n_lat, d = x.shape
    n_ctx = ctx.shape[1]
    depth = ada_w.shape[0]
    n_e = router_w.shape[-1]
    assert nb == 8 and n_ctx % TM == 0 and n_lat % TM == 0 and n_lat % GRID_W == 0
    s = n_ctx + n_lat

    c16 = jnp.concatenate([c, c_ctx[None, :], jnp.zeros((16 - nb - 1, d), F32)], axis=0)
    mods_all = _ada_mods(c16, ada_w, ada_b).reshape(depth, 16, 6, d)

    xu = jnp.concatenate([ctx, x], axis=1).reshape(nb * s, d)
    tinfo_u, trow_u = _tile_info(nb, n_ctx, n_lat, latent_only=False)
    tinfo_l, _ = _tile_info(nb, n_ctx, n_lat, latent_only=True)

    for i in range(depth):
        kind, j = i % 3, i // 3
        last = i == depth - 1
        mods = mods_all[i]
        g1 = norm1_g[i].reshape(1, d)
        g2 = norm2_g[i].reshape(1, d)
        rwt = router_w[i].T
        rwh = rwt.astype(BF16)
        rwl = (rwt - rwh.astype(F32)).astype(BF16)
        rb = router_b[i].reshape(n_e, 1)
        if kind == 0:
            tinfo = tinfo_l if last else tinfo_u
            x1, h2, route, cnt = _pool_layer(xu, tinfo, mods, g1, pool_w[j], pool_scale[j].reshape(1, d),
                                             g2, rwh, rwl, rb)
            mrow = tinfo[1]
        elif kind == 1:
            assert not last
            a_re, a_im, bblk, cblk = _ssm_params(ssm_lam_re[j], ssm_lam_im[j], ssm_log_dt[j], ssm_b_re[j],
                                                 ssm_b_im[j], ssm_c_re[j], ssm_c_im[j])
            y2 = _ssm_scan(xu.reshape(nb, s, d), mods, g1, a_re, a_im, bblk, cblk, n_ctx)
            x1, h2, route, cnt = _glu_layer(xu, y2.reshape(2, nb * s, d), tinfo_u, mods, g1,
                                            ssm_d[j].reshape(1, d), ssm_glu_w[j], ssm_glu_b[j].reshape(1, 2 * d),
                                            g2, rwh, rwl, rb)
            mrow = tinfo_u[1]
        else:
            assert not last
            x1, h2, route, cnt = _attn_layer(xu, tinfo_u, trow_u, mods, g1, attn_wqkv[j],
                                             attn_q_g[j].reshape(1, HEAD_DIM), attn_k_g[j].reshape(1, HEAD_DIM),
                                             attn_wo[j], g2, rwh, rwl, rb, nb, n_ctx)
            mrow = tinfo_u[1]
        xu = _moe(x1, h2, route, cnt, mods, mrow, i, moe_w1, moe_b1, moe_w2, moe_b2)
    if xu.shape[0] == nb * n_lat:
        return xu.reshape(nb, n_lat, d)
    return xu.reshape(nb, s, d)[:, n_ctx:, :]
```

```python
import functools
import math

import numpy as np
import jax
import jax.numpy as jnp
from jax import lax
from jax.experimental import pallas as pl
from jax.experimental.pallas import tpu as pltpu

F32 = jnp.float32
BF16 = jnp.bfloat16

GRID_W = 64
NORM_EPS = 1e-6
POOL_WINDOWS = (2, 4, 8, 16)
SSM_H = 16
SSM_P = 64
SSM_SET = 8
SSM_TT = 64
HEAD_DIM = 128
N_KV_HEADS = 2
ATTN_HEADS = 4
ROPE_F = HEAD_DIM // 4
ROPE_THETA = 10000.0
TOP_K = 4
SWIGLU_ALPHA = 1.702
SWIGLU_LIMIT = 7.0

TM = 256
SUB_TILES = 4
BLK = 1024
SUBLANES = 8
LANES = 128
SEG_ALIGN = 2
VMEM_LIMIT = 56 * 1024 * 1024


def _cparams(sem, vmem=VMEM_LIMIT):
    return pltpu.CompilerParams(dimension_semantics=sem, vmem_limit_bytes=vmem)


def _rms(x, g):
    return x * lax.rsqrt(jnp.mean(x * x, axis=-1, keepdims=True) + NORM_EPS) * g


def _sigmoid(x):
    return 1.0 / (1.0 + jnp.exp(-x))


def _dot(a, b):
    return jnp.dot(a, b, preferred_element_type=F32)


def _dot_nt(a, b):
    return lax.dot_general(a, b, (((1,), (1,)), ((), ())), preferred_element_type=F32)


def _dot_tn(a, b):
    return lax.dot_general(a, b, (((0,), (0,)), ((), ())), preferred_element_type=F32)


def _ada_kernel(c_ref, w_ref, b_ref, o_ref):
    c = c_ref[...]
    s = c * _sigmoid(c)
    o_ref[0] = jnp.dot(s, w_ref[0], preferred_element_type=F32,
                       precision=lax.Precision.HIGHEST) + b_ref[0]


def _ada_mods(c16, ada_w, ada_b):
    depth, d, six_d = ada_w.shape
    tn = d
    return pl.pallas_call(
        _ada_kernel,
        grid=(depth, six_d // tn),
        in_specs=[
            pl.BlockSpec((16, d), lambda l, j: (0, 0)),
            pl.BlockSpec((1, d, tn), lambda l, j: (l, 0, j)),
            pl.BlockSpec((1, 1, tn), lambda l, j: (l, 0, j)),
        ],
        out_specs=pl.BlockSpec((1, 16, tn), lambda l, j: (l, 0, j)),
        out_shape=jax.ShapeDtypeStruct((depth, 16, six_d), F32),
        compiler_params=_cparams(("arbitrary", "arbitrary")),
        name="ada_mods",
    )(c16, ada_w, ada_b.reshape(depth, 1, six_d))


def _post_logits(x, y, mod, g2, rwh, rwl, rb, x1_ref, h2_ref):
    x1 = x + mod[2:3] * y
    x1_ref[...] = x1
    h2 = _rms(x1, g2) * (1.0 + mod[4:5]) + mod[3:4]
    h2_ref[...] = h2.astype(BF16)
    hh = h2.astype(BF16)
    hl = (h2 - hh.astype(F32)).astype(BF16)
    return _dot_nt(rwh, hh) + _dot_nt(rwh, hl) + _dot_nt(rwl, hh) + rb


def _post_route(logits, route_ref, cnt_ref):
    n_e = logits.shape[0]
    iota_e = lax.broadcasted_iota(jnp.int32, (n_e, TM), 0)
    vals, onehots = [], []
    l = logits
    for _ in range(TOP_K):
        m = jnp.max(l, axis=0, keepdims=True)
        idx = jnp.min(jnp.where(l == m, iota_e, n_e), axis=0, keepdims=True)
        sel = iota_e == idx
        vals.append(m)
        onehots.append(sel)
        l = jnp.where(sel, -jnp.inf, l)
    ex = [jnp.exp(v - vals[0]) for v in vals]
    den = ex[0] + ex[1] + ex[2] + ex[3]
    gates = [e / den for e in ex]
    member = jnp.zeros((n_e, TM), F32)
    for sel in onehots:
        member = member + jnp.where(sel, 1.0, 0.0)
    r_i = lax.broadcasted_iota(jnp.int32, (TM, TM), 0)
    c_i = lax.broadcasted_iota(jnp.int32, (TM, TM), 1)
    upper = jnp.where(r_i < c_i, 1.0, 0.0).astype(BF16)
    cum = _dot(member.astype(BF16), upper)
    cnt = jnp.sum(member, axis=1, keepdims=True)
    cnt_ref[0] = cnt
    c8 = jnp.floor((cnt + (SEG_ALIGN - 1)) * (1.0 / SEG_ALIGN)) * SEG_ALIGN
    e_r = lax.broadcasted_iota(jnp.int32, (n_e, n_e), 0)
    e_c = lax.broadcasted_iota(jnp.int32, (n_e, n_e), 1)
    lower = jnp.where(e_c < e_r, 1.0, 0.0).astype(BF16)
    seg = _dot(lower, jnp.broadcast_to(c8, (n_e, TM)).astype(BF16))
    base = seg + cum
    rows = []
    for sel in onehots:
        rows.append(jnp.sum(jnp.where(sel, base, 0.0), axis=0, keepdims=True))
    route_ref[0] = jnp.concatenate(rows + gates, axis=0)


def _post_specs(n_tiles, d, n_e, sub=1):
    specs = [
        pl.BlockSpec((sub * TM, d), lambda i, *_: (i, 0)),
        pl.BlockSpec((sub * TM, d), lambda i, *_: (i, 0)),
        pl.BlockSpec((sub, 2 * TOP_K, TM), lambda i, *_: (i, 0, 0)),
        pl.BlockSpec((sub, n_e, 1), lambda i, *_: (i, 0, 0)),
    ]
    shapes = [
        jax.ShapeDtypeStruct((n_tiles * TM, d), F32),
        jax.ShapeDtypeStruct((n_tiles * TM, d), BF16),
        jax.ShapeDtypeStruct((n_tiles, 2 * TOP_K, TM), F32),
        jax.ShapeDtypeStruct((n_tiles, n_e, 1), F32),
    ]
    return specs, shapes


def _post(xs, ys, mods, g2, rwh, rwl, rb, x1_ref, h2_ref, route_ref, cnt_ref):
    logits = []
    for u, (x, y, mod) in enumerate(zip(xs, ys, mods)):
        rows = pl.ds(u * TM, TM)
        logits.append(_post_logits(x, y, mod, g2, rwh, rwl, rb, x1_ref.at[rows], h2_ref.at[rows]))
    for u, l in enumerate(logits):
        _post_route(l, route_ref.at[pl.ds(u, 1)], cnt_ref.at[pl.ds(u, 1)])


def _mod_specs(sub, d):
    return [pl.BlockSpec((1, 6, d), lambda i, src, mrow, *_, u=u: (mrow[sub * i + u], 0, 0))
            for u in range(sub)]


def _const_spec(shape):
    nd = len(shape)
    return pl.BlockSpec(shape, lambda i, *_: (0,) * nd)


def _pool_kernel(src, mrow, hp, hn, *refs, sub):
    x_refs = refs[:sub]
    xp_ref, xn_ref = refs[sub:sub + 2]
    mod_refs = refs[sub + 2:2 * sub + 2]
    g1_ref, pw_ref, ps_ref, g2_ref, rwh_ref, rwl_ref, rb_ref = refs[2 * sub + 2:2 * sub + 9]
    outs = refs[2 * sub + 9:2 * sub + 13]
    hh_scr = refs[2 * sub + 13]
    i = pl.program_id(0)
    d = x_refs[0].shape[1]
    gc = d // len(POOL_WINDOWS)
    halo = SUBLANES
    g1 = g1_ref[...]

    def pre(v, mod):
        return _rms(v, g1) * (1.0 + mod[1:2]) + mod[0:1]

    mods = [m[0] for m in mod_refs]
    xs = [r[...] for r in x_refs]
    hs = [pre(x, mod) for x, mod in zip(xs, mods)]
    row = lax.broadcasted_iota(jnp.int32, (TM, 1), 0)
    ys = []
    for u in range(sub):
        has_prev = hp[sub * i + u] > 0
        has_next = hn[sub * i + u] > 0
        before = pre(xp_ref[...], mods[0]) if u == 0 else hs[u - 1][TM - halo:, :]
        after = pre(xn_ref[...], mods[-1]) if u == sub - 1 else hs[u + 1][:halo, :]
        hh_scr[u, 0:halo, :] = jnp.where(has_prev, before, 0.0)
        hh_scr[u, halo:halo + TM, :] = hs[u]
        hh_scr[u, halo + TM:2 * halo + TM, :] = jnp.where(has_next, after, 0.0)
        parts = []
        for g, win in enumerate(POOL_WINDOWS):
            half = win // 2
            c0 = g * gc
            acc = hh_scr[u, pl.ds(halo - half, TM), c0:c0 + gc]
            for j in range(-half + 1, half):
                acc = acc + hh_scr[u, pl.ds(halo + j, TM), c0:c0 + gc]
            lo_clip = jnp.where(has_prev, 0, jnp.maximum(half - row, 0))
            hi_clip = jnp.where(has_next, 0, jnp.maximum(row + half - TM, 0))
            cnt = (win - lo_clip - hi_clip).astype(F32)
            diff = acc / cnt - hs[u][:, c0:c0 + gc]
            parts.append(_dot(diff.astype(BF16), pw_ref[g]))
        ys.append(jnp.concatenate(parts, axis=1) * ps_ref[...])
    _post(xs, ys, mods, g2_ref[...], rwh_ref[...], rwl_ref[...], rb_ref[...], *outs)


def _pool_layer(xu, tinfo, mods, g1, pool_w, pool_scale, g2, rwh, rwl, rb):
    rows, d = xu.shape
    n_steps = tinfo[0].shape[0]
    n_e = rwh.shape[0]
    gc = pool_w.shape[-1]
    rpb = TM // SUBLANES
    sub = SUB_TILES
    out_specs, out_shapes = _post_specs(n_steps, d, n_e, sub)
    grid_spec = pltpu.PrefetchScalarGridSpec(
        num_scalar_prefetch=4,
        grid=(n_steps // sub,),
        in_specs=[
            pl.BlockSpec((TM, d), lambda i, src, mrow, hp, hn, u=u: (src[sub * i + u], 0)) for u in range(sub)
        ] + [
            pl.BlockSpec((SUBLANES, d),
                         lambda i, src, mrow, hp, hn: (jnp.maximum(src[sub * i] * rpb - 1, 0), 0)),
            pl.BlockSpec((SUBLANES, d),
                         lambda i, src, mrow, hp, hn: (jnp.minimum((src[sub * i + sub - 1] + 1) * rpb,
                                                                   rows // SUBLANES - 1), 0)),
        ] + _mod_specs(sub, d) + [
            _const_spec((1, d)),
            _const_spec((len(POOL_WINDOWS), gc, gc)),
            _const_spec((1, d)),
            _const_spec((1, d)),
            _const_spec((n_e, d)),
            _const_spec((n_e, d)),
            _const_spec((n_e, 1)),
        ],
        out_specs=out_specs,
        scratch_shapes=[pltpu.VMEM((sub, TM + 2 * SUBLANES, d), F32)],
    )
    return pl.pallas_call(
        functools.partial(_pool_kernel, sub=sub), grid_spec=grid_spec, out_shape=out_shapes,
        compiler_params=_cparams(("arbitrary",)), name="pool_mixer",
    )(*tinfo, *([xu] * (sub + 2)), *([mods] * sub), g1, pool_w.astype(BF16), pool_scale, g2, rwh, rwl, rb)


def _ssm_kernel(tile_of, x_ref, mod_ref, g1_ref, are_ref, aim_ref, bb_ref, cb_ref, y_ref,
                u_scr, y_scr, h_scr, *xs_scrs, n_ctx_tiles):
    dr = pl.program_id(0)
    i = pl.program_id(1)
    nb, tt, d = x_ref.shape
    n_sets = bb_ref.shape[1]
    sw = bb_ref.shape[3]
    hw = sw // 2
    uw = bb_ref.shape[2]
    is_ctx = tile_of[dr * pl.num_programs(1) + i] < n_ctx_tiles

    @pl.when(i == 0)
    def _():
        h_scr[...] = jnp.zeros_like(h_scr)

    g1 = g1_ref[...]
    for b in range(nb):
        shift = jnp.where(is_ctx, mod_ref[nb, 0:1, :], mod_ref[b, 0:1, :])
        scale = jnp.where(is_ctx, mod_ref[nb, 1:2, :], mod_ref[b, 1:2, :])
        hb = _rms(x_ref[b], g1) * (1.0 + scale) + shift
        for j in range(n_sets):
            u_scr[j, pl.ds(b, tt, stride=nb), :] = hb[:, j * uw:(j + 1) * uw]
    for j in range(n_sets):
        xs_scrs[j][...] = _dot(u_scr[j].astype(BF16), bb_ref[0, j])
    for j in range(n_sets):
        xs = xs_scrs[j]
        ar = jnp.broadcast_to(are_ref[0, j:j + 1, :], (nb, hw))
        ai = jnp.broadcast_to(aim_ref[0, j:j + 1, :], (nb, hw))
        hr = h_scr[:, j * sw:j * sw + hw]
        hi = h_scr[:, j * sw + hw:(j + 1) * sw]
        for step in range(tt):
            t = step + dr * (tt - 1 - 2 * step)
            r0 = pl.multiple_of(t * nb, nb)
            nhr = ar * hr - ai * hi + xs[pl.ds(r0, nb), 0:hw]
            nhi = ar * hi + ai * hr + xs[pl.ds(r0, nb), hw:sw]
            xs[pl.ds(r0, nb), 0:hw] = nhr
            xs[pl.ds(r0, nb), hw:sw] = nhi
            hr, hi = nhr, nhi
        h_scr[:, j * sw:j * sw + hw] = hr
        h_scr[:, j * sw + hw:(j + 1) * sw] = hi
        y_scr[j] = _dot(xs[...].astype(BF16), cb_ref[0, j])
    for b in range(nb):
        for j in range(n_sets):
            y_ref[0, b, :, j * uw:(j + 1) * uw] = y_scr[j, pl.ds(b, tt, stride=nb), :]


def _ssm_scan(x3, mods, g1, a_re, a_im, bblk, cblk, n_ctx):
    nb, s, d = x3.shape
    tt = SSM_TT
    nt = s // tt
    nct = n_ctx // tt
    fwd = np.arange(nt)
    bwd = np.concatenate([np.arange(nct)[::-1], np.arange(nct, nt)[::-1]])
    tile_of = jnp.asarray(np.concatenate([fwd, bwd]), jnp.int32)
    n_sets, uw, sw = bblk.shape[1:]
    grid_spec = pltpu.PrefetchScalarGridSpec(
        num_scalar_prefetch=1,
        grid=(2, nt),
        in_specs=[
            pl.BlockSpec((nb, tt, d), lambda dr, i, to: (0, to[dr * nt + i], 0)),
            pl.BlockSpec(mods.shape, lambda dr, i, to: (0, 0, 0)),
            pl.BlockSpec((1, d), lambda dr, i, to: (0, 0)),
            pl.BlockSpec((1, n_sets, sw // 2), lambda dr, i, to: (dr, 0, 0)),
            pl.BlockSpec((1, n_sets, sw // 2), lambda dr, i, to: (dr, 0, 0)),
            pl.BlockSpec((1, n_sets, uw, sw), lambda dr, i, to: (dr, 0, 0, 0)),
            pl.BlockSpec((1, n_sets, sw, uw), lambda dr, i, to: (dr, 0, 0, 0)),
        ],
        out_specs=pl.BlockSpec((1, nb, tt, d), lambda dr, i, to: (dr, 0, to[dr * nt + i], 0)),
        scratch_shapes=[
            pltpu.VMEM((n_sets, tt * nb, uw), F32),
            pltpu.VMEM((n_sets, tt * nb, uw), F32),
            pltpu.VMEM((nb, n_sets * sw), F32),
        ] + [pltpu.VMEM((tt * nb, sw), F32) for _ in range(n_sets)],
    )
    return pl.pallas_call(
        functools.partial(_ssm_kernel, n_ctx_tiles=nct), grid_spec=grid_spec,
        out_shape=jax.ShapeDtypeStruct((2, nb, s, d), F32),
        compiler_params=_cparams(("arbitrary", "arbitrary")), name="s5_scan",
    )(tile_of, x3, mods, g1, a_re, a_im, bblk, cblk)


def _ssm_params(lam_re, lam_im, log_dt, b_re, b_im, c_re, c_im):
    g, p = lam_re.shape[1:]
    h = b_re.shape[-1]
    ns = g // SSM_SET
    eye = jnp.eye(SSM_SET, dtype=F32)
    outs = []
    for dr in range(2):
        lr, li = lam_re[dr].astype(F32), lam_im[dr].astype(F32)
        br, bi = b_re[dr].astype(F32), b_im[dr].astype(F32)
        dt = jnp.exp(log_dt[dr].astype(F32))[:, None]
        zr, zi = lr * dt, li * dt
        mag = jnp.exp(zr)
        ar, ai = mag * jnp.cos(zi), mag * jnp.sin(zi)
        den = lr * lr + li * li
        cr = ((ar - 1.0) * lr + ai * li) / den
        ci = (ai * lr - (ar - 1.0) * li) / den
        bbr = cr[..., None] * br - ci[..., None] * bi
        bbi = cr[..., None] * bi + ci[..., None] * br

        def blk_b(w):
            w = jnp.transpose(w, (0, 2, 1)).reshape(ns, SSM_SET, h, p)
            return jnp.einsum('ab,jahp->jahbp', eye, w).reshape(ns, SSM_SET * h, SSM_SET * p)

        def blk_c(w):
            w = jnp.transpose(w.reshape(ns, SSM_SET, h, p), (0, 1, 3, 2))
            return jnp.einsum('ab,japh->japbh', eye, w).reshape(ns, SSM_SET * p, SSM_SET * h)

        bblk = jnp.concatenate([blk_b(bbr), blk_b(bbi)], axis=2)
        cblk = jnp.concatenate([blk_c(c_re[dr].astype(F32)), -blk_c(c_im[dr].astype(F32))], axis=1)
        outs.append((ar.reshape(ns, SSM_SET * p), ai.reshape(ns, SSM_SET * p), bblk, cblk))
    a_re = jnp.stack([o[0] for o in outs])
    a_im = jnp.stack([o[1] for o in outs])
    bblk = jnp.stack([o[2] for o in outs]).astype(BF16)
    cblk = jnp.stack([o[3] for o in outs]).astype(BF16)
    return a_re, a_im, bblk, cblk


def _glu_kernel(src, mrow, x_ref, yf_ref, yb_ref, *refs, sub):
    mod_refs = refs[:sub]
    g1_ref, dsk_ref, gw_ref, gb_ref, g2_ref, rwh_ref, rwl_ref, rb_ref = refs[sub:sub + 8]
    outs = refs[sub + 8:]
    d = x_ref.shape[1]
    xs, ys, mods = [], [], []
    for u in range(sub):
        rows = pl.ds(u * TM, TM)
        mod = mod_refs[u][0]
        x = x_ref[rows, :]
        h = _rms(x, g1_ref[...]) * (1.0 + mod[1:2]) + mod[0:1]
        y = yf_ref[0, rows, :] + yb_ref[0, rows, :] + dsk_ref[...] * h
        gl = 0.5 * y * (1.0 + jnp.tanh(math.sqrt(2.0 / math.pi) * (y + 0.044715 * (y * y * y))))
        z = _dot(gl.astype(BF16), gw_ref[...]) + gb_ref[...]
        xs.append(x)
        ys.append(z[:, :d] * _sigmoid(z[:, d:]))
        mods.append(mod)
    _post(xs, ys, mods, g2_ref[...], rwh_ref[...], rwl_ref[...], rb_ref[...], *outs)


def _glu_layer(xu, y2, tinfo, mods, g1, d_skip, glu_w, glu_b, g2, rwh, rwl, rb):
    rows, d = xu.shape
    n_steps = rows // TM
    n_e = rwh.shape[0]
    sub = SUB_TILES
    out_specs, out_shapes = _post_specs(n_steps, d, n_e, sub)
    grid_spec = pltpu.PrefetchScalarGridSpec(
        num_scalar_prefetch=2,
        grid=(n_steps // sub,),
        in_specs=[
            pl.BlockSpec((sub * TM, d), lambda i, src, mrow: (i, 0)),
            pl.BlockSpec((1, sub * TM, d), lambda i, src, mrow: (0, i, 0)),
            pl.BlockSpec((1, sub * TM, d), lambda i, src, mrow: (1, i, 0)),
        ] + _mod_specs(sub, d) + [
            _const_spec((1, d)),
            _const_spec((1, d)),
            _const_spec((d, 2 * d)),
            _const_spec((1, 2 * d)),
            _const_spec((1, d)),
            _const_spec((n_e, d)),
            _const_spec((n_e, d)),
            _const_spec((n_e, 1)),
        ],
        out_specs=out_specs,
    )
    return pl.pallas_call(
        functools.partial(_glu_kernel, sub=sub), grid_spec=grid_spec, out_shape=out_shapes,
        compiler_params=_cparams(("arbitrary",)), name="s5_glu",
    )(tinfo[0], tinfo[1], xu, y2, y2, *([mods] * sub), g1, d_skip, glu_w.astype(BF16), glu_b, g2, rwh, rwl, rb)


def _rope(v, cos, sin_signed, first_half):
    partner = jnp.where(first_half, pltpu.roll(v, HEAD_DIM - HEAD_DIM // 4, axis=1),
                        pltpu.roll(v, HEAD_DIM // 4, axis=1))
    return v * cos + partner * sin_signed


def _qkv_kernel(src, mrow, trow, x_ref, *refs, sub):
    mod_refs = refs[:sub]
    cos_refs = refs[sub:2 * sub]
    sin_refs = refs[2 * sub:3 * sub]
    g1_ref, w_ref, qg_ref, kg_ref, q_ref, k_ref, v_ref = refs[3 * sub:]
    d = x_ref.shape[1]
    kvw = k_ref.shape[1]
    lane = lax.broadcasted_iota(jnp.int32, (TM, HEAD_DIM), 1)
    first_half = (lane % (HEAD_DIM // 2)) < (HEAD_DIM // 4)
    q_scale = HEAD_DIM ** -0.5
    zs = []
    for u in range(sub):
        mod = mod_refs[u][0]
        h = _rms(x_ref[pl.ds(u * TM, TM), :], g1_ref[...]) * (1.0 + mod[1:2]) + mod[0:1]
        zs.append(_dot(h.astype(BF16), w_ref[...]))
    for u, z in enumerate(zs):
        rows = pl.ds(u * TM, TM)
        cos = cos_refs[u][...]
        sin = sin_refs[u][...]
        for hd in range(d // HEAD_DIM):
            zh = z[:, hd * HEAD_DIM:(hd + 1) * HEAD_DIM]
            zh = _rope(_rms(zh, qg_ref[...]), cos, sin, first_half) * q_scale
            q_ref[rows, hd * HEAD_DIM:(hd + 1) * HEAD_DIM] = zh.astype(BF16)
        for hd in range(kvw // HEAD_DIM):
            zh = z[:, d + hd * HEAD_DIM:d + (hd + 1) * HEAD_DIM]
            zh = _rope(_rms(zh, kg_ref[...]), cos, sin, first_half)
            k_ref[rows, hd * HEAD_DIM:(hd + 1) * HEAD_DIM] = zh.astype(BF16)
        v_ref[rows, :] = z[:, d + kvw:].astype(BF16)


def _attn_kernel(q_ref, k_ref, v_ref, o_ref, *, n_ctx):
    qt = pl.program_id(2)

    def attend(n_keys):
        for hd in range(ATTN_HEADS):
            cols = slice(hd * HEAD_DIM, (hd + 1) * HEAD_DIM)
            s = _dot_nt(q_ref[:, cols], k_ref[0:n_keys, :])
            m = jnp.max(s, axis=-1, keepdims=True)
            p = jnp.exp(s - m)
            den = jnp.sum(p, axis=-1, keepdims=True)
            o_ref[:, cols] = (_dot(p.astype(BF16), v_ref[0:n_keys, :]) / den).astype(BF16)

    @pl.when(qt < n_ctx // TM)
    def _():
        attend(n_ctx)

    @pl.when(qt >= n_ctx // TM)
    def _():
        attend(k_ref.shape[0])


def _wo_kernel(src, mrow, x_ref, o_ref, *refs, sub):
    mod_refs = refs[:sub]
    wo_ref, g2_ref, rwh_ref, rwl_ref, rb_ref = refs[sub:sub + 5]
    outs = refs[sub + 5:]
    tiles = [pl.ds(u * TM, TM) for u in range(sub)]
    ys = [_dot(o_ref[rows, :], wo_ref[...]) for rows in tiles]
    _post([x_ref[rows, :] for rows in tiles], ys, [m[0] for m in mod_refs],
          g2_ref[...], rwh_ref[...], rwl_ref[...], rb_ref[...], *outs)


def _rope_tables(n_ctx, n_lat):
    rows = n_lat // GRID_W
    row = jnp.repeat(jnp.arange(rows), GRID_W)
    col = jnp.tile(jnp.arange(GRID_W), rows)
    pos = jnp.stack([row, col], axis=-1).astype(F32)
    inv_freq = ROPE_THETA ** (-jnp.arange(ROPE_F, dtype=F32) / ROPE_F)
    ang = pos[:, :, None] * inv_freq
    cos, sin = jnp.cos(ang), jnp.sin(ang)
    cos_t = jnp.concatenate([cos, cos], axis=-1).reshape(n_lat, HEAD_DIM)
    sin_t = jnp.concatenate([-sin, sin], axis=-1).reshape(n_lat, HEAD_DIM)
    cos_t = jnp.concatenate([jnp.ones((n_ctx, HEAD_DIM), F32), cos_t], axis=0)
    sin_t = jnp.concatenate([jnp.zeros((n_ctx, HEAD_DIM), F32), sin_t], axis=0)
    return cos_t, sin_t


def _attn_layer(xu, tinfo, trow, mods, g1, wqkv, q_g, k_g, wo, g2, rwh, rwl, rb, nb, n_ctx):
    rows, d = xu.shape
    s = rows // nb
    n_steps = rows // TM
    n_e = rwh.shape[0]
    kvw = N_KV_HEADS * HEAD_DIM
    n_heads = d // HEAD_DIM
    rep = n_heads // N_KV_HEADS
    cos_t, sin_t = _rope_tables(n_ctx, s - n_ctx)
    sub = SUB_TILES
    table_specs = [pl.BlockSpec((TM, HEAD_DIM), lambda i, src, mrow, tr, u=u: (tr[sub * i + u], 0))
                   for u in range(sub)]
    grid_spec = pltpu.PrefetchScalarGridSpec(
        num_scalar_prefetch=3,
        grid=(n_steps // sub,),
        in_specs=[pl.BlockSpec((sub * TM, d), lambda i, src, mrow, tr: (i, 0))]
        + _mod_specs(sub, d) + table_specs + table_specs + [
            _const_spec((1, d)),
            _const_spec((d, d + 2 * kvw)),
            _const_spec((1, HEAD_DIM)),
            _const_spec((1, HEAD_DIM)),
        ],
        out_specs=[
            pl.BlockSpec((sub * TM, d), lambda i, *_: (i, 0)),
            pl.BlockSpec((sub * TM, kvw), lambda i, *_: (i, 0)),
            pl.BlockSpec((sub * TM, kvw), lambda i, *_: (i, 0)),
        ],
    )
    q, k, v = pl.pallas_call(
        functools.partial(_qkv_kernel, sub=sub), grid_spec=grid_spec,
        out_shape=[jax.ShapeDtypeStruct((rows, d), BF16),
                   jax.ShapeDtypeStruct((rows, kvw), BF16),
                   jax.ShapeDtypeStruct((rows, kvw), BF16)],
        compiler_params=_cparams(("arbitrary",)), name="attn_qkv",
    )(tinfo[0], tinfo[1], trow, xu, *([mods] * sub), *([cos_t] * sub), *([sin_t] * sub),
      g1, wqkv.astype(BF16), q_g, k_g)

    tpb = s // TM
    o = pl.pallas_call(
        functools.partial(_attn_kernel, n_ctx=n_ctx),
        grid=(nb, n_heads // ATTN_HEADS, tpb),
        in_specs=[
            pl.BlockSpec((TM, ATTN_HEADS * HEAD_DIM), lambda b, hp, t: (b * tpb + t, hp)),
            pl.BlockSpec((s, HEAD_DIM), lambda b, hp, t: (b, hp * ATTN_HEADS // rep)),
            pl.BlockSpec((s, HEAD_DIM), lambda b, hp, t: (b, hp * ATTN_HEADS // rep)),
        ],
        out_specs=pl.BlockSpec((TM, ATTN_HEADS * HEAD_DIM), lambda b, hp, t: (b * tpb + t, hp)),
        out_shape=jax.ShapeDtypeStruct((rows, d), BF16),
        compiler_params=_cparams(("arbitrary", "arbitrary", "arbitrary")), name="attn_core",
    )(q, k, v)

    sub = SUB_TILES
    out_specs, out_shapes = _post_specs(n_steps, d, n_e, sub)
    grid_spec = pltpu.PrefetchScalarGridSpec(
        num_scalar_prefetch=2,
        grid=(n_steps // sub,),
        in_specs=[
            pl.BlockSpec((sub * TM, d), lambda i, src, mrow: (i, 0)),
            pl.BlockSpec((sub * TM, d), lambda i, src, mrow: (i, 0)),
        ] + _mod_specs(sub, d) + [
            _const_spec((d, d)),
            _const_spec((1, d)),
            _const_spec((n_e, d)),
            _const_spec((n_e, d)),
            _const_spec((n_e, 1)),
        ],
        out_specs=out_specs,
    )
    return pl.pallas_call(
        functools.partial(_wo_kernel, sub=sub), grid_spec=grid_spec, out_shape=out_shapes,
        compiler_params=_cparams(("arbitrary",)), name="attn_out",
    )(tinfo[0], tinfo[1], xu, o, *([mods] * sub), wo.astype(BF16), g2, rwh, rwl, rb)


def _loc_rows(n_e):
    return -(-(TM * TOP_K + (SEG_ALIGN - 1) * n_e) // 32) * 32


def _store_rows(ref, v):
    r = v.shape[0]
    slabs = ref.shape[0] // r
    for j in range(slabs):
        ref[pl.ds(j, r, stride=slabs), :] = v[:, j * LANES:(j + 1) * LANES]


def _load_rows(ref, r):
    slabs = ref.shape[0] // r
    return jnp.concatenate([ref[pl.ds(j, r, stride=slabs), :] for j in range(slabs)], axis=1)


def _pack_pairs(v, rounded):
    half = v.shape[1] // 2
    lo, hi = v[:, :half], v[:, half:]
    if not rounded:
        lo, hi = lo.astype(BF16).astype(F32), hi.astype(BF16).astype(F32)
    lo = lax.bitcast_convert_type(lo, jnp.uint32)
    hi = lax.bitcast_convert_type(hi, jnp.uint32)
    return (hi & jnp.uint32(0xFFFF0000)) | (lo >> 16)


def _unpack_pairs(p):
    lo = lax.bitcast_convert_type(p << 16, F32)
    hi = lax.bitcast_convert_type(p & jnp.uint32(0xFFFF0000), F32)
    return jnp.concatenate([lo, hi], axis=1).astype(BF16)


def _tile_rows(v):
    return pl.multiple_of(v * (SUBLANES // SEG_ALIGN), SUBLANES)


def _start_segments(t, n_e, loc, len8, dst, make_copy):
    for e in range(n_e):
        n = len8[t * n_e + e]

        @pl.when(n > 0)
        def _(e=e, n=n):
            make_copy(_tile_rows(loc[t * n_e + e]), _tile_rows(dst[t * n_e + e]), _tile_rows(n)).start()


def _dispatch_kernel(loc, len8, dst, tot, padoff, padlen, nused, h2_ref, route_ref, xs_ref,
                     buf, zbuf, sems, zsem, *, n_e, nb_max):
    i = pl.program_id(0)
    n_steps = pl.num_programs(0)
    slot = i % 2
    sub = buf.shape[1]
    lrows = buf.shape[2] // (SUBLANES // SEG_ALIGN)
    iota_p = lax.broadcasted_iota(jnp.int16, (lrows, TM), 0)
    for u in range(sub):
        pos = route_ref[u, 0:TOP_K, :].astype(jnp.int32).astype(jnp.int16)
        perm = jnp.zeros((lrows, TM), BF16)
        for k in range(TOP_K):
            perm = jnp.where(iota_p == pos[k:k + 1], jnp.ones((), BF16), perm)
        _store_rows(buf.at[slot, u], _pack_pairs(_dot(perm, h2_ref[pl.ds(u * TM, TM), :]), rounded=True))

    def seg_copy(s, u):
        def make(a, g, n):
            return pltpu.make_async_copy(buf.at[s, u, pl.ds(a, n)], xs_ref.at[pl.ds(g, n)], sems.at[s, u])
        return make

    def wait_tiles(step, s):
        for u in range(sub):
            seg_copy(s, u)(0, 0, _tile_rows(tot[sub * step + u])).wait()

    for u in range(sub):
        _start_segments(sub * i + u, n_e, loc, len8, dst, seg_copy(slot, u))

    @pl.when(i > 0)
    def _():
        wait_tiles(i - 1, 1 - slot)

    @pl.when(i == n_steps - 1)
    def _():
        wait_tiles(i, slot)

    @pl.when(i == n_steps - 1)
    def _():
        zbuf[...] = jnp.zeros_like(zbuf)

        def zero_rows(g, n):
            return pltpu.make_async_copy(zbuf.at[pl.ds(0, n)], xs_ref.at[pl.ds(g, n)], zsem)

        def pad(e, total):
            n = padlen[e]

            @pl.when(n > 0)
            def _():
                zero_rows(_tile_rows(padoff[e]), _tile_rows(n)).start()
            return total + n

        total = lax.fori_loop(0, n_e, pad, 0)

        def blank(b, carry):
            zero_rows(_tile_rows(b * BLK), zbuf.shape[0]).start()
            return carry

        lax.fori_loop(nused[0], nb_max, blank, 0)
        total = _tile_rows(total + (nb_max - nused[0]) * BLK)

        @pl.when(total > 0)
        def _():
            pltpu.make_async_copy(xs_ref.at[pl.ds(0, total)], xs_ref.at[pl.ds(0, total)], zsem).wait()


def _expert_kernel(blk, blk_e, valid, nused, xs_ref, w1_ref, b1_ref, w2_ref, b2_ref, ys_ref, w1b, w2b):
    i = pl.program_id(0)
    f = w2b.shape[0]

    @pl.when(i >= nused[0])
    def _():
        ys_ref[...] = jnp.zeros_like(ys_ref)

    @pl.when(i < nused[0])
    def _():
        first = jnp.logical_or(i == 0, blk_e[i] != blk_e[jnp.maximum(i - 1, 0)])

        @pl.when(first)
        def _():
            w1b[...] = w1_ref[0, 0].astype(BF16)
            w2b[...] = w2_ref[0, 0].astype(BF16)

        row = lax.broadcasted_iota(jnp.int32, (BLK, 1), 0)
        x = _unpack_pairs(jnp.where(row < valid[i], _load_rows(xs_ref, BLK), jnp.uint32(0)))
        z = _dot(x, w1b[...]) + b1_ref[0, 0]
        glu = jnp.minimum(z[:, :f], SWIGLU_LIMIT)
        lin = jnp.clip(z[:, f:], -SWIGLU_LIMIT, SWIGLU_LIMIT)
        act = glu * _sigmoid(SWIGLU_ALPHA * glu) * (lin + 1.0)
        y = _dot(act.astype(BF16), w2b[...]) + b2_ref[0, 0]
        _store_rows(ys_ref, _pack_pairs(y, rounded=False))


def _combine_kernel(loc, len8, dst, tot, mrow, ys_ref, route_ref, x1_ref, *refs, n_e):
    buf, sems = refs[-2:]
    out_ref = refs[-3]
    mod_refs = refs[:-3]
    i = pl.program_id(0)
    n_steps = pl.num_programs(0)
    slot = i % 2
    sub = buf.shape[1]
    lrows = buf.shape[2] // (SUBLANES // SEG_ALIGN)

    def seg_copy(s, u):
        def make(a, g, n):
            return pltpu.make_async_copy(ys_ref.at[pl.ds(g, n)], buf.at[s, u, pl.ds(a, n)], sems.at[s, u])
        return make

    @pl.when(i == 0)
    def _():
        for u in range(sub):
            _start_segments(u, n_e, loc, len8, dst, seg_copy(0, u))

    @pl.when(i + 1 < n_steps)
    def _():
        for u in range(sub):
            _start_segments(sub * (i + 1) + u, n_e, loc, len8, dst, seg_copy(1 - slot, u))

    iota_p = lax.broadcasted_iota(jnp.int16, (lrows, TM), 0)
    row = lax.broadcasted_iota(jnp.int32, (lrows, 1), 0)
    totals = [tot[sub * i + u] for u in range(sub)]
    for u in range(sub):
        seg_copy(slot, u)(0, 0, _tile_rows(totals[u])).wait()
    for u in range(sub):
        total = totals[u]
        pos = route_ref[u, 0:TOP_K, :].astype(jnp.int32).astype(jnp.int16)
        gate = route_ref[u, TOP_K:2 * TOP_K, :].astype(BF16)
        gt = jnp.zeros((lrows, TM), BF16)
        for k in range(TOP_K):
            gt = jnp.where(iota_p == pos[k:k + 1], gate[k:k + 1], gt)
        ysl = _unpack_pairs(jnp.where(row < total, _load_rows(buf.at[slot, u], lrows), jnp.uint32(0)))
        rows = pl.ds(u * TM, TM)
        out_ref[rows, :] = x1_ref[rows, :] + mod_refs[u][0][5:6] * _dot_tn(gt, ysl)


def _moe(x1, h2, route, cnt, mods, mrow, layer, w1, b1, w2, b2):
    rows, d = x1.shape
    n_tiles = rows // TM
    depth, n_e, _, f2 = w1.shape
    lrows = _loc_rows(n_e)
    nb_max = (rows * TOP_K + (SEG_ALIGN - 1) * n_tiles * n_e) // BLK + n_e
    cap = nb_max * BLK
    slabs = d // 2 // LANES
    assert slabs * SEG_ALIGN == SUBLANES

    i32 = lambda v: v.astype(jnp.int32)
    cnt = i32(cnt.reshape(n_tiles, n_e))
    c8 = (cnt + SEG_ALIGN - 1) // SEG_ALIGN * SEG_ALIGN
    loc = jnp.cumsum(c8, axis=1) - c8
    tot_tile = jnp.sum(c8, axis=1)
    tot = jnp.sum(c8, axis=0)
    nblk = (tot + BLK - 1) // BLK
    blk_end = jnp.cumsum(nblk)
    blk_start = blk_end - nblk
    dst = (blk_start * BLK)[None, :] + jnp.cumsum(c8, axis=0) - c8
    nused = blk_end[-1]
    bid = jnp.arange(nb_max, dtype=jnp.int32)
    bidc = jnp.minimum(bid, nused - 1)
    blk_e = jnp.minimum(jnp.sum(i32(bidc[:, None] >= blk_end[None, :]), axis=1), n_e - 1)
    valid = jnp.clip(tot[blk_e] - (bidc - blk_start[blk_e]) * BLK, 0, BLK)
    padoff = blk_start * BLK + tot
    padlen = nblk * BLK - tot
    loc, len8, dst = i32(loc.reshape(-1)), i32(c8.reshape(-1)), i32(dst.reshape(-1))
    nused = i32(nused.reshape(1))
    sub = SUB_TILES
    row_buf = pltpu.VMEM((2, sub, lrows * slabs, LANES), jnp.uint32)

    xs = pl.pallas_call(
        functools.partial(_dispatch_kernel, n_e=n_e, nb_max=nb_max),
        grid_spec=pltpu.PrefetchScalarGridSpec(
            num_scalar_prefetch=7,
            grid=(n_tiles // sub,),
            in_specs=[
                pl.BlockSpec((sub * TM, d), lambda i, *_: (i, 0)),
                pl.BlockSpec((sub, 2 * TOP_K, TM), lambda i, *_: (i, 0, 0)),
            ],
            out_specs=pl.BlockSpec(memory_space=pl.ANY),
            scratch_shapes=[row_buf, pltpu.VMEM((BLK * slabs, LANES), jnp.uint32),
                            pltpu.SemaphoreType.DMA((2, sub)), pltpu.SemaphoreType.DMA(())],
        ),
        out_shape=jax.ShapeDtypeStruct((cap * slabs, LANES), jnp.uint32),
        compiler_params=_cparams(("arbitrary",)), name="moe_dispatch",
    )(loc, len8, dst, i32(tot_tile), i32(padoff), i32(padlen), nused, h2, route)

    ys = pl.pallas_call(
        _expert_kernel,
        grid_spec=pltpu.PrefetchScalarGridSpec(
            num_scalar_prefetch=4,
            grid=(nb_max,),
            in_specs=[
                pl.BlockSpec((BLK * slabs, LANES), lambda i, blk, be, va, nu: (blk[i], 0)),
                pl.BlockSpec((1, 1, d, f2), lambda i, blk, be, va, nu: (layer, be[i], 0, 0)),
                pl.BlockSpec((1, 1, 1, f2), lambda i, blk, be, va, nu: (layer, be[i], 0, 0)),
                pl.BlockSpec((1, 1, f2 // 2, d), lambda i, blk, be, va, nu: (layer, be[i], 0, 0)),
                pl.BlockSpec((1, 1, 1, d), lambda i, blk, be, va, nu: (layer, be[i], 0, 0)),
            ],
            out_specs=pl.BlockSpec((BLK * slabs, LANES), lambda i, blk, be, va, nu: (i, 0)),
            scratch_shapes=[pltpu.VMEM((d, f2), BF16), pltpu.VMEM((f2 // 2, d), BF16)],
        ),
        out_shape=jax.ShapeDtypeStruct((cap * slabs, LANES), jnp.uint32),
        compiler_params=_cparams(("arbitrary",)), name="moe_experts",
    )(i32(bidc), i32(blk_e), i32(valid), nused, xs, w1, b1.reshape(depth, n_e, 1, f2), w2,
      b2.reshape(depth, n_e, 1, d))

    return pl.pallas_call(
        functools.partial(_combine_kernel, n_e=n_e),
        grid_spec=pltpu.PrefetchScalarGridSpec(
            num_scalar_prefetch=5,
            grid=(n_tiles // sub,),
            in_specs=[
                pl.BlockSpec(memory_space=pl.ANY),
                pl.BlockSpec((sub, 2 * TOP_K, TM), lambda i, *_: (i, 0, 0)),
                pl.BlockSpec((sub * TM, d), lambda i, *_: (i, 0)),
            ] + [pl.BlockSpec((1, 6, d), lambda i, lo, le, ds, to, mr, u=u: (mr[sub * i + u], 0, 0))
                 for u in range(sub)],
            out_specs=pl.BlockSpec((sub * TM, d), lambda i, *_: (i, 0)),
            scratch_shapes=[row_buf, pltpu.SemaphoreType.DMA((2, sub))],
        ),
        out_shape=jax.ShapeDtypeStruct((rows, d), F32),
        compiler_params=_cparams(("arbitrary",)), name="moe_combine",
    )(loc, len8, dst, i32(tot_tile), mrow, ys, route, x1, *([mods] * sub))


def _tile_info(nb, n_ctx, n_lat, latent_only):
    tpb = (n_ctx + n_lat) // TM
    ct = n_ctx // TM
    src, mrow, hp, hn, trow = [], [], [], [], []
    for b in range(nb):
        for j in range(ct if latent_only else 0, tpb):
            is_ctx = j < ct
            src.append(b * tpb + j)
            mrow.append(nb if is_ctx else b)
            hp.append(0 if j in (0, ct) else 1)
            hn.append(0 if j in (ct - 1, tpb - 1) else 1)
            trow.append(j)
    mk = lambda v: jnp.asarray(np.asarray(v, np.int32))
    return (mk(src), mk(mrow), mk(hp), mk(hn)), mk(trow)


def kernel(x, c, ctx, c_ctx, ada_w, ada_b, norm1_g, norm2_g, pool_w, pool_scale, ssm_lam_re, ssm_lam_im,
           ssm_log_dt, ssm_b_re, ssm_b_im, ssm_c_re, ssm_c_im, ssm_d, ssm_glu_w, ssm_glu_b, attn_wqkv,
           attn_q_g, attn_k_g, attn_wo, router_w, router_b, moe_w1, moe_b1, moe_w2, moe_b2):
    nb, n_lat, d = x.shape
    n_ctx = ctx.shape[1]
    depth = ada_w.shape[0]
    n_e = router_w.shape[-1]
    assert nb == 8 and n_ctx % TM == 0 and n_lat % TM == 0 and n_lat % GRID_W == 0
    s = n_ctx + n_lat

    c16 = jnp.concatenate([c, c_ctx[None, :], jnp.zeros((16 - nb - 1, d), F32)], axis=0)
    mods_all = _ada_mods(c16, ada_w, ada_b).reshape(depth, 16, 6, d)

    xu = jnp.concatenate([ctx, x], axis=1).reshape(nb * s, d)
    tinfo_u, trow_u = _tile_info(nb, n_ctx, n_lat, latent_only=False)
    tinfo_l, _ = _tile_info(nb, n_ctx, n_lat, latent_only=True)

    for i in range(depth):
        kind, j = i % 3, i // 3
        last = i == depth - 1
        mods = mods_all[i]
        g1 = norm1_g[i].reshape(1, d)
        g2 = norm2_g[i].reshape(1, d)
        rwt = router_w[i].T
        rwh = rwt.astype(BF16)
        rwl = (rwt - rwh.astype(F32)).astype(BF16)
        rb = router_b[i].reshape(n_e, 1)
        if kind == 0:
            tinfo = tinfo_l if last else tinfo_u
            x1, h2, route, cnt = _pool_layer(xu, tinfo, mods, g1, pool_w[j], pool_scale[j].reshape(1, d),
                                             g2, rwh, rwl, rb)
            mrow = tinfo[1]
        elif kind == 1:
            assert not last
            a_re, a_im, bblk, cblk = _ssm_params(ssm_lam_re[j], ssm_lam_im[j], ssm_log_dt[j], ssm_b_re[j],
                                                 ssm_b_im[j], ssm_c_re[j], ssm_c_im[j])
            y2 = _ssm_scan(xu.reshape(nb, s, d), mods, g1, a_re, a_im, bblk, cblk, n_ctx)
            x1, h2, route, cnt = _glu_layer(xu, y2.reshape(2, nb * s, d), tinfo_u, mods, g1,
                                            ssm_d[j].reshape(1, d), ssm_glu_w[j], ssm_glu_b[j].reshape(1, 2 * d),
                                            g2, rwh, rwl, rb)
            mrow = tinfo_u[1]
        else:
            assert not last
            x1, h2, route, cnt = _attn_layer(xu, tinfo_u, trow_u, mods, g1, attn_wqkv[j],
                                             attn_q_g[j].reshape(1, HEAD_DIM), attn_k_g[j].reshape(1, HEAD_DIM),
                                             attn_wo[j], g2, rwh, rwl, rb, nb, n_ctx)
            mrow = tinfo_u[1]
        xu = _moe(x1, h2, route, cnt, mods, mrow, i, moe_w1, moe_b1, moe_w2, moe_b2)
    if xu.shape[0] == nb * n_lat:
        return xu.reshape(nb, n_lat, d)
    return xu.reshape(nb, s, d)[:, n_ctx:, :]
```
